```python
import jax
import jax.numpy as jnp
from jax import lax
import numpy as np

D_MODEL = 1024
BATCH = 8
SEQ = 4096
DEPTH = 2

GRID_W = 64
CTX_LEN = 256

NA_HEADS = 8
NA_HEAD_DIM = 64
NA_WIDTH = NA_HEADS * NA_HEAD_DIM
NA_WIN_ROWS = 8
NA_WIN_COLS = 16
FNET_GROUPS = 4
FNET_GROUP_DIM = 128
FNET_WIDTH = FNET_GROUPS * FNET_GROUP_DIM
RET_HEADS = 4
RET_QK_DIM = 128
RET_V_DIM = 256
RET_QK_WIDTH = RET_HEADS * RET_QK_DIM
RET_V_WIDTH = RET_HEADS * RET_V_DIM
RET_CHUNK = 128
ROPE_BASE = 10000.0

N_BRANCHES = 3
IN_SPLITS = (NA_WIDTH, NA_WIDTH, NA_WIDTH, FNET_WIDTH, RET_QK_WIDTH, RET_QK_WIDTH, RET_V_WIDTH, RET_V_WIDTH, N_BRANCHES * D_MODEL)
IN_WIDTH = sum(IN_SPLITS)

N_EXPERTS = 256
TOP_K = 8
N_GROUPS = 8
TOPK_GROUPS = 4
EXPERTS_PER_GROUP = N_EXPERTS // N_GROUPS
EXPERT_DIM = 256
SHARED_DIM = 256
ROUTED_SCALE = 2.5
MOE_BLOCK = 128

DN_ALPHA = (2.0 * DEPTH) ** 0.25
DN_BETA = (8.0 * DEPTH) ** -0.25
LN_EPS = 1e-6
GN_EPS = 1e-5

kernel_name = 'hybrid_natten_fnet_retention_moe_dit'


def layer_norm(x, w=None, b=None):
    xf = x.astype(jnp.float32)
    mu = jnp.mean(xf, axis=-1, keepdims=True)
    var = jnp.mean(jnp.square(xf - mu), axis=-1, keepdims=True)
    y = (xf - mu) * lax.rsqrt(var + LN_EPS)
    if w is not None:
        y = y * w.astype(jnp.float32) + b.astype(jnp.float32)
    return y.astype(x.dtype)


def modulate(x, shift, scale):
    return layer_norm(x) * (1.0 + scale) + shift


def swiglu(x, w_gate, w_up, w_down):
    return (jax.nn.silu(x @ w_gate) * (x @ w_up)) @ w_down


def split_heads(t, n_heads):
    return t.reshape(t.shape[0], t.shape[1], n_heads, -1)


def axial_rope(t, row, col):
    half = t.shape[-1] // 2
    n_pairs = half // 2
    inv_freq = ROPE_BASE ** (-jnp.arange(n_pairs, dtype=jnp.float32) / n_pairs)

    def rotate(u, pos):
        ang = pos.astype(jnp.float32)[:, None] * inv_freq[None, :]
        cos = jnp.cos(ang)[None, :, None, :].astype(u.dtype)
        sin = jnp.sin(ang)[None, :, None, :].astype(u.dtype)
        u1, u2 = u[..., :n_pairs], u[..., n_pairs:]
        return jnp.concatenate([u1 * cos - u2 * sin, u1 * sin + u2 * cos], axis=-1)

    return jnp.concatenate([rotate(t[..., :half], row), rotate(t[..., half:], col)], axis=-1)


def neighbourhood_attention(q, k, v, k_ctx, v_ctx, rpb, rows):
    B, N, H, Dh = q.shape
    kh = min(NA_WIN_ROWS, rows)
    scale = Dh ** -0.5
    q = q.reshape(B, rows, GRID_W, H, Dh)
    k = k.reshape(B, rows, GRID_W, H, Dh)
    v = v.reshape(B, rows, GRID_W, H, Dh)
    r = jnp.arange(rows)
    r0 = jnp.clip(r - kh // 2, 0, rows - kh)
    key_rows = r0[:, None] + jnp.arange(kh)[None, :]
    k_band = k[:, key_rows]
    v_band = v[:, key_rows]
    cidx = jnp.arange(GRID_W)
    c0 = jnp.clip(cidx - NA_WIN_COLS // 2, 0, GRID_W - NA_WIN_COLS)
    col_ok = (cidx[None, :] >= c0[:, None]) & (cidx[None, :] < c0[:, None] + NA_WIN_COLS)
    dr = key_rows - r[:, None] + (NA_WIN_ROWS - 1)
    dc = jnp.clip(cidx[None, :] - cidx[:, None], -(NA_WIN_COLS - 1), NA_WIN_COLS - 1) + (NA_WIN_COLS - 1)
    bias = rpb[:, dr[:, None, :, None], dc[None, :, None, :]].astype(jnp.float32)
    s_win = jnp.einsum('brqhd,brkwhd->bhrqkw', q, k_band).astype(jnp.float32) * scale + bias[None]
    s_win = jnp.where(col_ok[:, None, :], s_win, -jnp.inf)
    s_ctx = jnp.einsum('brqhd,blhd->bhrql', q, k_ctx).astype(jnp.float32) * scale
    n_win = kh * GRID_W
    s = jnp.concatenate([s_win.reshape(B, H, rows, GRID_W, n_win), s_ctx], axis=-1)
    p = jax.nn.softmax(s, axis=-1).astype(v.dtype)
    p_win = p[..., :n_win].reshape(B, H, rows, GRID_W, kh, GRID_W)
    p_ctx = p[..., n_win:]
    o = jnp.einsum('bhrqkw,brkwhd->brqhd', p_win, v_band) + jnp.einsum('bhrql,blhd->brqhd', p_ctx, v_ctx)
    return o.reshape(B, N, H * Dh)


def context_attention(q, k, v):
    B, L, H, Dh = q.shape
    s = jnp.einsum('blhd,bmhd->bhlm', q, k).astype(jnp.float32) * (Dh ** -0.5)
    p = jax.nn.softmax(s, axis=-1).astype(v.dtype)
    return jnp.einsum('bhlm,bmhd->blhd', p, v).reshape(B, L, H * Dh)


def fourier_mix(u):
    B, N, _ = u.shape
    ug = u.reshape(B, N, FNET_GROUPS, FNET_GROUP_DIM).astype(jnp.float32)
    f = jnp.fft.fft2(ug, axes=(1, 3), norm='ortho').real
    return f.reshape(B, N, FNET_WIDTH).astype(u.dtype)


def retention_chunkwise(q, k, v, log_g, s0, inclusive):
    B, H, N, Dk = q.shape
    Dv = v.shape[-1]
    C = RET_CHUNK
    nc = N // C
    qc = q.astype(jnp.float32).reshape(B, H, nc, C, Dk)
    kc = k.astype(jnp.float32).reshape(B, H, nc, C, Dk)
    vc = v.astype(jnp.float32).reshape(B, H, nc, C, Dv)
    lg = log_g.astype(jnp.float32)[:, None]
    pos = jnp.arange(C, dtype=jnp.float32)
    diff = pos[:, None] - pos[None, :]
    in_band = (diff >= 0) if inclusive else (diff > 0)
    decay_in = jnp.where(in_band, jnp.exp(lg[:, :, None] * jnp.maximum(diff, 0.0)), 0.0)
    scores = jnp.einsum('bhncd,bhnmd->bhncm', qc, kc) * decay_in[None, :, None]
    o_inner = jnp.einsum('bhncm,bhnme->bhnce', scores, vc)
    k_decay = jnp.exp(lg * (C - 1 - pos))
    q_decay = jnp.exp(lg * (pos + 1))
    chunk_decay = jnp.exp(lg[:, 0] * C)[None, :, None, None]
    kv = jnp.einsum('bhnmd,bhnme->nbhde', kc * k_decay[None, :, None, :, None], vc)

    def step(state, kv_chunk):
        return chunk_decay * state + kv_chunk, state

    s_final, s_prev = lax.scan(step, s0.astype(jnp.float32), kv)
    o_cross = jnp.einsum('bhncd,nbhde->bhnce', qc * q_decay[None, :, None, :, None], s_prev)
    return (o_inner + o_cross).reshape(B, H, N, Dv), s_final


def bidirectional_retention(q, k, v, log_g_fwd, log_g_bwd, s0_fwd, s0_bwd):
    o_f, s_f = retention_chunkwise(q, k, v, log_g_fwd, s0_fwd, True)
    o_b, s_b = retention_chunkwise(q[:, :, ::-1], k[:, :, ::-1], v[:, :, ::-1], log_g_bwd, s0_bwd, False)
    return o_f + o_b[:, :, ::-1], s_f, s_b


def retention_output(o, gate, gn_w):
    B, H, N, Dv = o.shape
    mu = jnp.mean(o, axis=-1, keepdims=True)
    var = jnp.mean(jnp.square(o - mu), axis=-1, keepdims=True)
    y = ((o - mu) * lax.rsqrt(var + GN_EPS)).transpose(0, 2, 1, 3).reshape(B, N, H * Dv)
    y = y * gn_w.astype(jnp.float32)
    return y.astype(gate.dtype) * jax.nn.silu(gate)


def merge_branches(gate_logits, y_na, y_fn, y_ret, w_o_na, w_fourier, w_o_ret, w_out):
    gates = jax.nn.sigmoid(gate_logits.reshape(gate_logits.shape[0], gate_logits.shape[1], N_BRANCHES, D_MODEL))
    y = (gates[..., 0, :] * (y_na @ w_o_na) + gates[..., 1, :] * (y_fn @ w_fourier)
         + gates[..., 2, :] * (y_ret @ w_o_ret))
    return y @ w_out


def token_mixer(h, h_ctx, w_in, na_rpb, w_o_na, w_fourier, log_g_fwd, log_g_bwd, ret_gn_w, w_o_ret, w_out, update_ctx):
    B, N, _ = h.shape
    rows = N // GRID_W
    split_at = [int(s) for s in np.cumsum(IN_SPLITS)[:-1]]
    qa, ka, va, ub, qr, kr, vr, gr, gl = jnp.split(h @ w_in, split_at, axis=-1)
    qa_c, ka_c, va_c, ub_c, qr_c, kr_c, vr_c, gr_c, gl_c = jnp.split(h_ctx @ w_in, split_at, axis=-1)

    y_na = neighbourhood_attention(split_heads(qa, NA_HEADS), split_heads(ka, NA_HEADS), split_heads(va, NA_HEADS),
                                   split_heads(ka_c, NA_HEADS), split_heads(va_c, NA_HEADS), na_rpb, rows)
    y_fn = fourier_mix(ub)
    t = jnp.arange(N)
    row, col = t // GRID_W, t % GRID_W
    q_scale = RET_QK_DIM ** -0.5

    def ret_heads(qx, kx, vx, with_rope):
        qx, kx, vx = split_heads(qx, RET_HEADS), split_heads(kx, RET_HEADS), split_heads(vx, RET_HEADS)
        if with_rope:
            qx, kx = axial_rope(qx, row, col), axial_rope(kx, row, col)
        return (qx * q_scale).transpose(0, 2, 1, 3), kx.transpose(0, 2, 1, 3), vx.transpose(0, 2, 1, 3)

    zero_state = jnp.zeros((B, RET_HEADS, RET_QK_DIM, RET_V_DIM), jnp.float32)
    qc_h, kc_h, vc_h = ret_heads(qr_c, kr_c, vr_c, False)
    o_ret_c, s_fwd, s_bwd = bidirectional_retention(qc_h, kc_h, vc_h, log_g_fwd, log_g_bwd, zero_state, zero_state)
    q_h, k_h, v_h = ret_heads(qr, kr, vr, True)
    o_ret, _, _ = bidirectional_retention(q_h, k_h, v_h, log_g_fwd, log_g_bwd, s_fwd, s_bwd)
    y_ret = retention_output(o_ret, gr, ret_gn_w)
    y = merge_branches(gl, y_na, y_fn, y_ret, w_o_na, w_fourier, w_o_ret, w_out)
    if not update_ctx:
        return y, None
    y_na_c = context_attention(split_heads(qa_c, NA_HEADS), split_heads(ka_c, NA_HEADS), split_heads(va_c, NA_HEADS))
    y_fn_c = fourier_mix(ub_c)
    y_ret_c = retention_output(o_ret_c, gr_c, ret_gn_w)
    y_c = merge_branches(gl_c, y_na_c, y_fn_c, y_ret_c, w_o_na, w_fourier, w_o_ret, w_out)
    return y, y_c


def routed_experts(h, top_e, top_w, w_gate, w_up, w_down):
    T, D = h.shape
    A = T * TOP_K
    n_slots = -(-(A + N_EXPERTS * (MOE_BLOCK - 1)) // MOE_BLOCK) * MOE_BLOCK
    n_blocks = n_slots // MOE_BLOCK
    e_flat = top_e.reshape(A)
    order = jnp.argsort(e_flat)
    e_sorted = e_flat[order]
    tok_sorted = (order // TOP_K).astype(jnp.int32)
    w_sorted = top_w.reshape(A)[order]
    counts = jnp.bincount(e_flat, length=N_EXPERTS)
    padded = (counts + MOE_BLOCK - 1) // MOE_BLOCK * MOE_BLOCK
    pad_end = jnp.cumsum(padded)
    pad_start = pad_end - padded
    seg_start = jnp.cumsum(counts) - counts
    dest = pad_start[e_sorted] + (jnp.arange(A) - seg_start[e_sorted])
    slot_tok = jnp.full((n_slots,), T, jnp.int32).at[dest].set(tok_sorted)
    slot_w = jnp.zeros((n_slots,), h.dtype).at[dest].set(w_sorted)
    block_e = jnp.minimum(jnp.searchsorted(pad_end, jnp.arange(n_blocks) * MOE_BLOCK, side='right'), N_EXPERTS - 1)
    h_pad = jnp.concatenate([h, jnp.zeros((1, D), h.dtype)], axis=0)

    def expert_block(args):
        tok, wt, e = args
        return swiglu(h_pad[tok], w_gate[e], w_up[e], w_down[e]) * wt[:, None]

    y = lax.map(expert_block, (slot_tok.reshape(n_blocks, MOE_BLOCK), slot_w.reshape(n_blocks, MOE_BLOCK), block_e))
    return jax.ops.segment_sum(y.reshape(n_slots, D), slot_tok, num_segments=T + 1)[:T]


def moe_ffn(h, router_w, router_bias, exp_w_gate, exp_w_up, exp_w_down, sh_w_gate, sh_w_up, sh_w_down):
    T = h.shape[0]
    scores = jax.nn.sigmoid((h @ router_w).astype(jnp.float32))
    sel = scores + router_bias.astype(jnp.float32)
    grp_score = lax.top_k(sel.reshape(T, N_GROUPS, EXPERTS_PER_GROUP), 2)[0].sum(-1)
    _, top_g = lax.top_k(grp_score, TOPK_GROUPS)
    g_mask = jnp.any(top_g[..., None] == jnp.arange(N_GROUPS), axis=-2)
    sel = jnp.where(jnp.repeat(g_mask, EXPERTS_PER_GROUP, axis=-1), sel, -jnp.inf)
    _, top_e = lax.top_k(sel, TOP_K)
    w = jnp.take_along_axis(scores, top_e, axis=-1)
    w = w / jnp.sum(w, axis=-1, keepdims=True) * ROUTED_SCALE
    routed = routed_experts(h, top_e, w.astype(h.dtype), exp_w_gate, exp_w_up, exp_w_down)
    return routed + swiglu(h, sh_w_gate, sh_w_up, sh_w_down)


def setup_inputs(seed: int = 0) -> dict:
    key = jax.random.key(seed)
    ks = jax.random.split(key, 27)
    f32 = jnp.float32
    D = D_MODEL

    def nrm(k, shape, std):
        return jax.random.normal(k, shape, f32) * std

    base = 1.0 - 2.0 ** (-5.0 - np.arange(RET_HEADS))
    decay_logit = jnp.asarray(np.log(base / (1.0 - base)), f32)
    return {
        'x': nrm(ks[0], (BATCH, SEQ, D), 1.0),
        'c': nrm(ks[1], (BATCH, D), 1.0),
        'ctx': nrm(ks[2], (BATCH, CTX_LEN, D), 1.0),
        'c_ctx': nrm(ks[3], (D,), 1.0),
        'ada_w': nrm(ks[4], (DEPTH, D, 6 * D), 0.5 * D ** -0.5),
        'ada_b': nrm(ks[5], (DEPTH, 6 * D), 0.02),
        'w_in': nrm(ks[6], (DEPTH, D, IN_WIDTH), D ** -0.5),
        'na_rpb': nrm(ks[7], (DEPTH, NA_HEADS, 2 * NA_WIN_ROWS - 1, 2 * NA_WIN_COLS - 1), 0.1),
        'w_o_na': nrm(ks[8], (DEPTH, NA_WIDTH, D), DN_BETA * NA_WIDTH ** -0.5),
        'w_fourier': nrm(ks[9], (DEPTH, FNET_WIDTH, D), DN_BETA * FNET_WIDTH ** -0.5),
        'ret_decay_fwd': decay_logit + nrm(ks[10], (DEPTH, RET_HEADS), 0.01),
        'ret_decay_bwd': decay_logit + nrm(ks[11], (DEPTH, RET_HEADS), 0.01),
        'ret_gn_w': 1.0 + nrm(ks[12], (DEPTH, RET_V_WIDTH), 0.02),
        'w_o_ret': nrm(ks[13], (DEPTH, RET_V_WIDTH, D), DN_BETA * RET_V_WIDTH ** -0.5),
        'w_out': nrm(ks[14], (DEPTH, D, D), DN_BETA * D ** -0.5),
        'ln_mix_w': 1.0 + nrm(ks[15], (DEPTH, D), 0.02),
        'ln_mix_b': nrm(ks[16], (DEPTH, D), 0.02),
        'router_w': nrm(ks[17], (DEPTH, D, N_EXPERTS), D ** -0.5),
        'router_bias': nrm(ks[18], (DEPTH, N_EXPERTS), 0.01),
        'exp_w_gate': nrm(ks[19], (DEPTH, N_EXPERTS, D, EXPERT_DIM), D ** -0.5),
        'exp_w_up': nrm(ks[20], (DEPTH, N_EXPERTS, D, EXPERT_DIM), D ** -0.5),
        'exp_w_down': nrm(ks[21], (DEPTH, N_EXPERTS, EXPERT_DIM, D), DN_BETA * EXPERT_DIM ** -0.5),
        'sh_w_gate': nrm(ks[22], (DEPTH, D, SHARED_DIM), D ** -0.5),
        'sh_w_up': nrm(ks[23], (DEPTH, D, SHARED_DIM), D ** -0.5),
        'sh_w_down': nrm(ks[24], (DEPTH, SHARED_DIM, D), DN_BETA * SHARED_DIM ** -0.5),
        'ln_ffn_w': 1.0 + nrm(ks[25], (DEPTH, D), 0.02),
        'ln_ffn_b': nrm(ks[26], (DEPTH, D), 0.02),
    }


def reference(x, c, ctx, c_ctx, ada_w, ada_b, w_in, na_rpb, w_o_na, w_fourier, ret_decay_fwd, ret_decay_bwd,
              ret_gn_w, w_o_ret, w_out, ln_mix_w, ln_mix_b, router_w, router_bias, exp_w_gate, exp_w_up,
              exp_w_down, sh_w_gate, sh_w_up, sh_w_down, ln_ffn_w, ln_ffn_b):
    B, N, D = x.shape
    L = ctx.shape[1]
    xc = ctx
    for l in range(DEPTH):
        update_ctx = l < DEPTH - 1
        mod = jax.nn.silu(c) @ ada_w[l] + ada_b[l]
        mod_c = jax.nn.silu(c_ctx) @ ada_w[l] + ada_b[l]
        sh1, sc1, g1, sh2, sc2, g2 = jnp.split(mod[:, None, :], 6, axis=-1)
        sh1_c, sc1_c, g1_c, sh2_c, sc2_c, g2_c = jnp.split(mod_c, 6, axis=-1)
        log_g_fwd = jax.nn.log_sigmoid(ret_decay_fwd[l].astype(jnp.float32))
        log_g_bwd = jax.nn.log_sigmoid(ret_decay_bwd[l].astype(jnp.float32))
        y, y_c = token_mixer(modulate(x, sh1, sc1), modulate(xc, sh1_c, sc1_c), w_in[l], na_rpb[l], w_o_na[l],
                             w_fourier[l], log_g_fwd, log_g_bwd, ret_gn_w[l], w_o_ret[l], w_out[l], update_ctx)
        x = layer_norm(DN_ALPHA * x + g1 * y, ln_mix_w[l], ln_mix_b[l])
        h = modulate(x, sh2, sc2).reshape(B * N, D)
        moe_args = (router_w[l], router_bias[l], exp_w_gate[l], exp_w_up[l], exp_w_down[l],
                    sh_w_gate[l], sh_w_up[l], sh_w_down[l])
        if update_ctx:
            xc = layer_norm(DN_ALPHA * xc + g1_c * y_c, ln_mix_w[l], ln_mix_b[l])
            h_c = modulate(xc, sh2_c, sc2_c).reshape(B * L, D)
            f = moe_ffn(jnp.concatenate([h, h_c], axis=0), *moe_args)
            f_lat = f[:B * N].reshape(B, N, D)
            xc = layer_norm(DN_ALPHA * xc + g2_c * f[B * N:].reshape(B, L, D), ln_ffn_w[l], ln_ffn_b[l])
        else:
            f_lat = moe_ffn(h, *moe_args).reshape(B, N, D)
        x = layer_norm(DN_ALPHA * x + g2 * f_lat, ln_ffn_w[l], ln_ffn_b[l])
    return x
```

```python
import functools
import math

import jax
import jax.numpy as jnp
from jax import lax
from jax.experimental import pallas as pl
from jax.experimental.pallas import tpu as pltpu

F32 = jnp.float32
BF16 = jnp.bfloat16

GRID_COLS = 64
NA_HEADS = 8
NA_DH = 64
NA_W = NA_HEADS * NA_DH
NA_ROWS = 8
NA_COLS = 16
FN_GROUPS = 4
FN_GD = 128
FN_W = FN_GROUPS * FN_GD
RT_HEADS = 4
RT_DK = 128
RT_DV = 256
RT_CHUNK = 128
ROPE_THETA = 10000.0
N_EXP = 256
TOPK = 8
N_GRP = 8
TOPK_GRP = 4
GRP_SZ = N_EXP // N_GRP
ROUTED_SCALE = 2.5
LN_EPS = 1e-6
GN_EPS = 1e-5

COL_QA, COL_KA, COL_VA, COL_U, COL_QR, COL_KR, COL_VR, COL_GR, COL_GL = (
    0, 512, 1024, 1536, 2048, 2560, 3072, 4096, 5120)

VMEM_LIMIT = 56 * 1024 * 1024
EXPERT_BLOCK = 256
TOKEN_TILE = 256
NEG = -1e30


def _cparams(sem):
    return pltpu.CompilerParams(dimension_semantics=sem, vmem_limit_bytes=VMEM_LIMIT)


def _ln(x):
    mu = jnp.mean(x, axis=-1, keepdims=True)
    xc = x - mu
    var = jnp.mean(xc * xc, axis=-1, keepdims=True)
    return xc * lax.rsqrt(var + LN_EPS)


def _silu(x):
    return x * jax.nn.sigmoid(x)


def _ada_kernel(c_ref, w_ref, b_ref, o_ref):
    a = _silu(c_ref[...]).astype(BF16)
    o_ref[...] = jnp.dot(a, w_ref[...].astype(BF16), preferred_element_type=F32) + b_ref[...]


def ada_mod(c_rows, w, b):
    R, D = c_rows.shape
    W = w.shape[1]
    tn = 1024
    return pl.pallas_call(
        _ada_kernel,
        out_shape=jax.ShapeDtypeStruct((R, W), F32),
        grid=(W // tn,),
        in_specs=[pl.BlockSpec((R, D), lambda j: (0, 0)),
                  pl.BlockSpec((D, tn), lambda j: (0, j)),
                  pl.BlockSpec((1, tn), lambda j: (0, j))],
        out_specs=pl.BlockSpec((R, tn), lambda j: (0, j)),
        compiler_params=_cparams(("parallel",)),
        name="ada_mod",
    )(c_rows, w, b.reshape(1, W))


def _ln_proj_kernel(x_ref, sh_ref, sc_ref, w_ref, o_ref, h_scr):
    @pl.when(pl.program_id(2) == 0)
    def _():
        h = _ln(x_ref[0]) * (1.0 + sc_ref[0]) + sh_ref[0]
        h_scr[...] = h.astype(BF16)

    o_ref[0] = jnp.dot(h_scr[...], w_ref[...], preferred_element_type=F32).astype(o_ref.dtype)


def ln_proj(x, shift, scale, w_bf16):
    B, N, D = x.shape
    W = w_bf16.shape[1]
    tm = min(N, 1024)
    tn = 1024
    return pl.pallas_call(
        _ln_proj_kernel,
        out_shape=jax.ShapeDtypeStruct((B, N, W), BF16),
        grid=(B, N // tm, W // tn),
        in_specs=[pl.BlockSpec((1, tm, D), lambda b, i, j: (b, i, 0)),
                  pl.BlockSpec((1, 1, D), lambda b, i, j: (b, 0, 0)),
                  pl.BlockSpec((1, 1, D), lambda b, i, j: (b, 0, 0)),
                  pl.BlockSpec((D, tn), lambda b, i, j: (0, j))],
        out_specs=pl.BlockSpec((1, tm, tn), lambda b, i, j: (b, i, j)),
        scratch_shapes=[pltpu.VMEM((tm, D), BF16)],
        compiler_params=_cparams(("parallel", "parallel", "arbitrary")),
        name="ln_proj",
    )(x, shift, scale, w_bf16)


def _attn_kernel(*refs, window, rows):
    if window:
        q_ref, k_ref, v_ref, kc_ref, vc_ref, bias_ref, o_ref = refs
    else:
        q_ref, kc_ref, vc_ref, o_ref = refs
    scale = NA_DH ** -0.5
    q = q_ref[0]
    kc = kc_ref[0]
    vc = vc_ref[0]
    if window:
        r = pl.program_id(1)
        r0 = jnp.clip(r - NA_ROWS // 2, 0, rows - NA_ROWS)
        start = pl.multiple_of(r0 * GRID_COLS, GRID_COLS)
        kw = k_ref[0, pl.ds(start, NA_ROWS * GRID_COLS), :]
        vw = v_ref[0, pl.ds(start, NA_ROWS * GRID_COLS), :]
    dn = (((1,), (1,)), ((), ()))
    outs = []
    for h in range(NA_HEADS):
        hs = slice(h * NA_DH, (h + 1) * NA_DH)
        qh = q[:, hs]
        s_c = lax.dot_general(qh, kc[:, hs], dn, preferred_element_type=F32) * scale
        m = jnp.max(s_c, axis=-1, keepdims=True)
        if window:
            s_w = lax.dot_general(qh, kw[:, hs], dn, preferred_element_type=F32) * scale + bias_ref[0, h]
            m = jnp.maximum(m, jnp.max(s_w, axis=-1, keepdims=True))
        p_c = jnp.exp(s_c - m)
        l = jnp.sum(p_c, axis=-1, keepdims=True)
        o = jnp.dot(p_c.astype(BF16), vc[:, hs], preferred_element_type=F32)
        if window:
            p_w = jnp.exp(s_w - m)
            l = l + jnp.sum(p_w, axis=-1, keepdims=True)
            o = o + jnp.dot(p_w.astype(BF16), vw[:, hs], preferred_element_type=F32)
        outs.append(o / l)
    o_ref[0] = jnp.concatenate(outs, axis=-1).astype(o_ref.dtype)


def window_bias_tables(rpb):
    H = rpb.shape[0]
    cidx = jnp.arange(GRID_COLS)
    c0 = jnp.clip(cidx - NA_COLS // 2, 0, GRID_COLS - NA_COLS)
    col_ok = (cidx[None, :] >= c0[:, None]) & (cidx[None, :] < c0[:, None] + NA_COLS)
    dc = jnp.clip(cidx[None, :] - cidx[:, None], -(NA_COLS - 1), NA_COLS - 1) + (NA_COLS - 1)
    d = jnp.arange(NA_ROWS)
    dr = d[:, None] + jnp.arange(NA_ROWS)[None, :]
    t = rpb.astype(F32)[:, dr[:, :, None, None], dc[None, None, :, :]]
    t = jnp.where(col_ok[None, None, None], t, NEG)
    t = t.transpose(1, 0, 3, 2, 4)
    return t.reshape(NA_ROWS, H, GRID_COLS, NA_ROWS * GRID_COLS)


def window_attention(proj, proj_ctx, bias_tab):
    B, N, _ = proj.shape
    L = proj_ctx.shape[1]
    rows = N // GRID_COLS
    cb = lambda off: off // NA_W

    def bias_idx(b, r):
        r0 = jnp.clip(r - NA_ROWS // 2, 0, rows - NA_ROWS)
        return (r0 - r + NA_ROWS - 1, 0, 0, 0)

    return pl.pallas_call(
        functools.partial(_attn_kernel, window=True, rows=rows),
        out_shape=jax.ShapeDtypeStruct((B, N, NA_W), BF16),
        grid=(B, rows),
        in_specs=[pl.BlockSpec((1, GRID_COLS, NA_W), lambda b, r: (b, r, cb(COL_QA))),
                  pl.BlockSpec((1, N, NA_W), lambda b, r: (b, 0, cb(COL_KA))),
                  pl.BlockSpec((1, N, NA_W), lambda b, r: (b, 0, cb(COL_VA))),
                  pl.BlockSpec((1, L, NA_W), lambda b, r: (b, 0, cb(COL_KA))),
                  pl.BlockSpec((1, L, NA_W), lambda b, r: (b, 0, cb(COL_VA))),
                  pl.BlockSpec((1, NA_HEADS, GRID_COLS, NA_ROWS * GRID_COLS), bias_idx)],
        out_specs=pl.BlockSpec((1, GRID_COLS, NA_W), lambda b, r: (b, r, 0)),
        compiler_params=_cparams(("parallel", "arbitrary")),
        name="window_attention",
    )(proj, proj, proj, proj_ctx, proj_ctx, bias_tab)


def context_attention(proj_ctx):
    B, L, _ = proj_ctx.shape
    cb = lambda off: off // NA_W
    tq = min(L, 128)
    return pl.pallas_call(
        functools.partial(_attn_kernel, window=False, rows=0),
        out_shape=jax.ShapeDtypeStruct((B, L, NA_W), BF16),
        grid=(B, L // tq),
        in_specs=[pl.BlockSpec((1, tq, NA_W), lambda b, r: (b, r, cb(COL_QA))),
                  pl.BlockSpec((1, L, NA_W), lambda b, r: (b, 0, cb(COL_KA))),
                  pl.BlockSpec((1, L, NA_W), lambda b, r: (b, 0, cb(COL_VA)))],
        out_specs=pl.BlockSpec((1, tq, NA_W), lambda b, r: (b, r, 0)),
        compiler_params=_cparams(("parallel", "arbitrary")),
        name="context_attention",
    )(proj_ctx, proj_ctx, proj_ctx)


def _dft_kernel(cn_ref, sn_ref, u_ref, cc_ref, sc_ref, o_ref, *, norm):
    u = u_ref[0]
    a = jnp.dot(cn_ref[...], u, preferred_element_type=F32).astype(BF16)
    b = jnp.dot(sn_ref[...], u, preferred_element_type=F32).astype(BF16)
    y = (jnp.dot(a, cc_ref[...], preferred_element_type=F32)
         - jnp.dot(b, sc_ref[...], preferred_element_type=F32))
    o_ref[0] = (y * norm).astype(o_ref.dtype)


def dft_tables(n):
    k = jnp.arange(n, dtype=jnp.int32)
    kn = (k[:, None] * k[None, :]) % n
    ang = kn.astype(F32) * (2.0 * math.pi / n)
    return jnp.cos(ang), jnp.sin(ang)


def channel_dft_tables():
    c, s = dft_tables(FN_GD)
    eye = jnp.eye(FN_GROUPS, dtype=F32)
    return jnp.kron(eye, c).astype(BF16), jnp.kron(eye, s).astype(BF16)


def fourier_mix(proj, cn, sn, cc, sc):
    B, N, _ = proj.shape
    tk = min(N, 512)
    norm = 1.0 / math.sqrt(N * FN_GD)
    return pl.pallas_call(
        functools.partial(_dft_kernel, norm=norm),
        out_shape=jax.ShapeDtypeStruct((B, N, FN_W), BF16),
        grid=(N // tk, B),
        in_specs=[pl.BlockSpec((tk, N), lambda i, b: (i, 0)),
                  pl.BlockSpec((tk, N), lambda i, b: (i, 0)),
                  pl.BlockSpec((1, N, FN_W), lambda i, b: (b, 0, COL_U // FN_W)),
                  pl.BlockSpec((FN_W, FN_W), lambda i, b: (0, 0)),
                  pl.BlockSpec((FN_W, FN_W), lambda i, b: (0, 0))],
        out_specs=pl.BlockSpec((1, tk, FN_W), lambda i, b: (b, i, 0)),
        compiler_params=_cparams(("parallel", "arbitrary")),
        name="fourier_mix",
    )(cn, sn, proj, cc, sc)


def _retention_kernel(*refs, backward, rope, finish, cpb):
    it = iter(refs)
    lg_ref = next(it)
    q_ref, k_ref, v_ref = next(it), next(it), next(it)
    if rope:
        cos_ref, sin_ref, perm_ref = next(it), next(it), next(it)
    s0_ref = next(it)
    if finish:
        ob_ref, gate_ref, gn_ref = next(it), next(it), next(it)
    o_ref, sf_ref = next(it), next(it)
    state = next(it)

    C = RT_CHUNK
    h = pl.program_id(1)
    step = pl.program_id(2)

    @pl.when(step == 0)
    def _():
        state[...] = s0_ref[0, 0]

    lg = lg_ref[h]
    pos_r = lax.broadcasted_iota(jnp.int32, (C, 1), 0).astype(F32)
    ci = lax.broadcasted_iota(jnp.int32, (C, C), 0)
    mi = lax.broadcasted_iota(jnp.int32, (C, C), 1)
    if backward:
        dist = (mi - ci).astype(F32)
        band = mi > ci
        k_dec = jnp.exp(lg * pos_r)
        q_dec = jnp.exp(lg * (C - pos_r))
    else:
        dist = (ci - mi).astype(F32)
        band = ci >= mi
        k_dec = jnp.exp(lg * (C - 1 - pos_r))
        q_dec = jnp.exp(lg * (pos_r + 1))
    decay = jnp.where(band, jnp.exp(lg * jnp.maximum(dist, 0.0)), 0.0)
    chunk_dec = jnp.exp(lg * C)
    q_scale = RT_DK ** -0.5

    order = range(cpb - 1, -1, -1) if backward else range(cpb)
    for c in order:
        sl = slice(c * C, (c + 1) * C)
        q = q_ref[0, sl, :].astype(F32)
        k = k_ref[0, sl, :].astype(F32)
        v = v_ref[0, sl, :]
        if rope:
            cos = cos_ref[sl, :]
            sin = sin_ref[sl, :]
            perm = perm_ref[...]
            q = q * cos + jnp.dot(q.astype(BF16), perm, preferred_element_type=F32) * sin
            k = k * cos + jnp.dot(k.astype(BF16), perm, preferred_element_type=F32) * sin
        q = q * q_scale
        s_prev = state[...]
        scores = lax.dot_general(q.astype(BF16), k.astype(BF16), (((1,), (1,)), ((), ())),
                                 preferred_element_type=F32) * decay
        o = jnp.dot(scores.astype(BF16), v, preferred_element_type=F32)
        o = o + jnp.dot((q * q_dec).astype(BF16), s_prev.astype(BF16), preferred_element_type=F32)
        kv = lax.dot_general((k * k_dec).astype(BF16), v, (((0,), (0,)), ((), ())),
                             preferred_element_type=F32)
        state[...] = chunk_dec * s_prev + kv
        if finish:
            o = o + ob_ref[0, sl, :]
            mu = jnp.mean(o, axis=-1, keepdims=True)
            oc = o - mu
            var = jnp.mean(oc * oc, axis=-1, keepdims=True)
            y = oc * lax.rsqrt(var + GN_EPS) * gn_ref[...]
            o = y * _silu(gate_ref[0, sl, :].astype(F32))
        o_ref[0, sl, :] = o.astype(o_ref.dtype)

    @pl.when(step == pl.num_programs(2) - 1)
    def _():
        sf_ref[0, 0] = state[...]


def rope_tables(n):
    t = jnp.arange(n)
    row, col = t // GRID_COLS, t % GRID_COLS
    half = RT_DK // 2
    n_pairs = half // 2
    inv_freq = ROPE_THETA ** (-jnp.arange(n_pairs, dtype=F32) / n_pairs)

    def cs(pos):
        ang = pos.astype(F32)[:, None] * inv_freq[None, :]
        c, s = jnp.cos(ang), jnp.sin(ang)
        return jnp.concatenate([c, c], axis=-1), jnp.concatenate([-s, s], axis=-1)

    cr, sr = cs(row)
    cc, sc = cs(col)
    cos = jnp.concatenate([cr, cc], axis=-1)
    sin = jnp.concatenate([sr, sc], axis=-1)
    idx = jnp.arange(RT_DK)
    src = (idx // half) * half + (idx % half + n_pairs) % half
    perm = (idx[:, None] == src[None, :]).astype(BF16)
    return cos, sin, perm


def retention(proj, log_g, s0, *, backward, rope_tabs=None, finish=None):
    B, N, _ = proj.shape
    nc = N // RT_CHUNK
    cpb = min(nc, 4)
    tb = cpb * RT_CHUNK
    nsteps = nc // cpb
    blk = (lambda s: nsteps - 1 - s) if backward else (lambda s: s)
    rope = rope_tabs is not None
    fin = finish is not None

    in_specs = [pl.BlockSpec(memory_space=pltpu.SMEM),
                pl.BlockSpec((1, tb, RT_DK), lambda b, h, s: (b, blk(s), COL_QR // RT_DK + h)),
                pl.BlockSpec((1, tb, RT_DK), lambda b, h, s: (b, blk(s), COL_KR // RT_DK + h)),
                pl.BlockSpec((1, tb, RT_DV), lambda b, h, s: (b, blk(s), COL_VR // RT_DV + h))]
    args = [log_g, proj, proj, proj]
    if rope:
        in_specs += [pl.BlockSpec((tb, RT_DK), lambda b, h, s: (blk(s), 0)),
                     pl.BlockSpec((tb, RT_DK), lambda b, h, s: (blk(s), 0)),
                     pl.BlockSpec((RT_DK, RT_DK), lambda b, h, s: (0, 0))]
        args += list(rope_tabs)
    in_specs.append(pl.BlockSpec((1, 1, RT_DK, RT_DV), lambda b, h, s: (b, h, 0, 0)))
    args.append(s0)
    if fin:
        o_b, gn_w = finish
        in_specs += [pl.BlockSpec((1, tb, RT_DV), lambda b, h, s: (b, blk(s), h)),
                     pl.BlockSpec((1, tb, RT_DV), lambda b, h, s: (b, blk(s), COL_GR // RT_DV + h)),
                     pl.BlockSpec((1, RT_DV), lambda b, h, s: (0, h))]
        args += [o_b, proj, gn_w.reshape(1, -1)]
    out_dtype = BF16 if fin else F32
    o, s_fin = pl.pallas_call(
        functools.partial(_retention_kernel, backward=backward, rope=rope, finish=fin, cpb=cpb),
        out_shape=(jax.ShapeDtypeStruct((B, N, RT_HEADS * RT_DV), out_dtype),
                   jax.ShapeDtypeStruct((B, RT_HEADS, RT_DK, RT_DV), F32)),
        grid=(B, RT_HEADS, nsteps),
        in_specs=in_specs,
        out_specs=(pl.BlockSpec((1, tb, RT_DV), lambda b, h, s: (b, blk(s), h)),
                   pl.BlockSpec((1, 1, RT_DK, RT_DV), lambda b, h, s: (b, h, 0, 0))),
        scratch_shapes=[pltpu.VMEM((RT_DK, RT_DV), F32)],
        compiler_params=_cparams(("parallel", "parallel", "arbitrary")),
        name="retention_bwd" if backward else "retention_fwd",
    )(*args)
    return o, s_fin


def _merge_kernel(yna_ref, yfn_ref, yrt_ref, g0_ref, g1_ref, g2_ref, wna_ref, wfn_ref, wrt_ref,
                  wout_ref, x_ref, gate_ref, lnw_ref, lnb_ref, sh_ref, sc_ref, xo_ref, h_ref, *, alpha):
    a = jnp.dot(yna_ref[0], wna_ref[...], preferred_element_type=F32)
    y = jax.nn.sigmoid(g0_ref[0].astype(F32)) * a
    b = jnp.dot(yfn_ref[0], wfn_ref[...], preferred_element_type=F32)
    y = y + jax.nn.sigmoid(g1_ref[0].astype(F32)) * b
    c = jnp.dot(yrt_ref[0], wrt_ref[...], preferred_element_type=F32)
    y = y + jax.nn.sigmoid(g2_ref[0].astype(F32)) * c
    y = jnp.dot(y.astype(BF16), wout_ref[...], preferred_element_type=F32)
    xn = _ln(alpha * x_ref[0] + gate_ref[0] * y) * lnw_ref[...] + lnb_ref[...]
    xo_ref[0] = xn
    h_ref[0] = _ln(xn) * (1.0 + sc_ref[0]) + sh_ref[0]


def merge_branches(y_na, y_fn, y_ret, proj, w_na, w_fn, w_rt, w_out, x, gate, ln_w, ln_b, shift2, scale2,
                   alpha):
    B, N, D = x.shape
    tm = min(N, 512)
    row = lambda w: pl.BlockSpec((1, tm, w), lambda b, i: (b, i, 0))
    glb = lambda j: pl.BlockSpec((1, tm, D), lambda b, i: (b, i, COL_GL // D + j))
    full = lambda a: pl.BlockSpec(a.shape, lambda b, i: (0,) * a.ndim)
    vec = pl.BlockSpec((1, 1, D), lambda b, i: (b, 0, 0))
    prm = pl.BlockSpec((1, D), lambda b, i: (0, 0))
    return pl.pallas_call(
        functools.partial(_merge_kernel, alpha=alpha),
        out_shape=(jax.ShapeDtypeStruct((B, N, D), F32), jax.ShapeDtypeStruct((B, N, D), F32)),
        grid=(B, N // tm),
        in_specs=[row(NA_W), row(FN_W), row(RT_HEADS * RT_DV), glb(0), glb(1), glb(2),
                  full(w_na), full(w_fn), full(w_rt), full(w_out),
                  row(D), vec, prm, prm, vec, vec],
        out_specs=(row(D), row(D)),
        compiler_params=_cparams(("parallel", "parallel")),
        name="merge_branches",
    )(y_na, y_fn, y_ret, proj, proj, proj, w_na, w_fn, w_rt, w_out, x, gate,
      ln_w.reshape(1, D), ln_b.reshape(1, D), shift2, scale2)


def _router_kernel(h_ref, w_ref, b_ref, e_ref, wt_ref, rk_ref, cnt_ref, cnt_scr):
    tm = h_ref.shape[0]

    @pl.when(pl.program_id(0) == 0)
    def _():
        cnt_scr[...] = jnp.zeros_like(cnt_scr)

    logits = jnp.dot(h_ref[...].astype(BF16), w_ref[...], preferred_element_type=F32)
    scores = jax.nn.sigmoid(logits)
    sel = scores + b_ref[...]
    lane = lax.broadcasted_iota(jnp.int32, (tm, N_EXP), 1)
    grp = lane // GRP_SZ

    def first_max(vals):
        m = jnp.max(vals, axis=-1, keepdims=True)
        idx = jnp.min(jnp.where(vals == m, lane, N_EXP), axis=-1, keepdims=True)
        return m, idx

    gscore = []
    for g in range(N_GRP):
        vals = jnp.where(grp == g, sel, -jnp.inf)
        m1, i1 = first_max(vals)
        m2, _ = first_max(jnp.where(lane == i1, -jnp.inf, vals))
        gscore.append(m1 + m2)
    keep = jnp.zeros((tm, N_EXP), dtype=jnp.bool_)
    for g in range(N_GRP):
        beaten = jnp.zeros((tm, 1), jnp.int32)
        for o in range(N_GRP):
            if o == g:
                continue
            ahead = (gscore[o] > gscore[g]) | ((gscore[o] == gscore[g]) & (o < g))
            beaten = beaten + ahead.astype(jnp.int32)
        keep = keep | ((grp == g) & (beaten < TOPK_GRP))
    cand = jnp.where(keep, sel, -jnp.inf)

    chosen = jnp.zeros((tm, N_EXP), dtype=jnp.bool_)
    ids, wts = [], []
    for _ in range(TOPK):
        _, idx = first_max(cand)
        hit = lane == idx
        ids.append(idx)
        wts.append(jnp.sum(jnp.where(hit, scores, 0.0), axis=-1, keepdims=True))
        chosen = chosen | hit
        cand = jnp.where(hit, -jnp.inf, cand)
    wsum = wts[0]
    for w in wts[1:]:
        wsum = wsum + w

    chosen_f = chosen.astype(F32)
    ri = lax.broadcasted_iota(jnp.int32, (tm, tm), 0)
    rj = lax.broadcasted_iota(jnp.int32, (tm, tm), 1)
    lower = (rj < ri).astype(BF16)
    rank_full = jnp.dot(lower, chosen_f.astype(BF16), preferred_element_type=F32) + cnt_scr[...]
    cnt_scr[...] = cnt_scr[...] + jnp.sum(chosen_f, axis=0, keepdims=True)

    lane_o = lax.broadcasted_iota(jnp.int32, e_ref.shape, 1)
    e_out = jnp.zeros(e_ref.shape, jnp.int32)
    w_out = jnp.zeros(e_ref.shape, F32)
    r_out = jnp.zeros(e_ref.shape, F32)
    for k in range(TOPK):
        rk = jnp.sum(jnp.where(lane == ids[k], rank_full, 0.0), axis=-1, keepdims=True)
        e_out = jnp.where(lane_o == k, ids[k], e_out)
        w_out = jnp.where(lane_o == k, wts[k] / wsum * ROUTED_SCALE, w_out)
        r_out = jnp.where(lane_o == k, rk, r_out)
    e_ref[...] = e_out
    wt_ref[...] = w_out
    rk_ref[...] = r_out.astype(jnp.int32)
    cnt_ref[...] = cnt_scr[...].astype(jnp.int32)


def router(h, w_bf16, bias):
    T, D = h.shape
    tm = TOKEN_TILE
    out = pl.BlockSpec((tm, 128), lambda i: (i, 0))
    return pl.pallas_call(
        _router_kernel,
        out_shape=(jax.ShapeDtypeStruct((T, 128), jnp.int32), jax.ShapeDtypeStruct((T, 128), F32),
                   jax.ShapeDtypeStruct((T, 128), jnp.int32), jax.ShapeDtypeStruct((1, N_EXP), jnp.int32)),
        grid=(T // tm,),
        in_specs=[pl.BlockSpec((tm, D), lambda i: (i, 0)),
                  pl.BlockSpec((D, N_EXP), lambda i: (0, 0)),
                  pl.BlockSpec((1, N_EXP), lambda i: (0, 0))],
        out_specs=(out, out, out, pl.BlockSpec((1, N_EXP), lambda i: (0, 0))),
        scratch_shapes=[pltpu.VMEM((1, N_EXP), F32)],
        compiler_params=_cparams(("arbitrary",)),
        name="router",
    )(h, w_bf16, bias.reshape(1, N_EXP))


def _row_chunks(n):
    return n // 128


def _dispatch_kernel(slot_hbm, h_ref, xs_in, xs_hbm, slot_smem, stage, sem_idx, sem):
    del xs_in
    tm = h_ref.shape[0]
    i = pl.program_id(0)
    idx_copy = pltpu.make_async_copy(slot_hbm.at[i], slot_smem, sem_idx)
    idx_copy.start()
    x = h_ref[...]
    for j in range(_row_chunks(x.shape[1])):
        stage[:, j, :] = x[:, j * 128:(j + 1) * 128]
    idx_copy.wait()

    def issue(t, carry):
        for k in range(TOPK):
            s = slot_smem[0, t * TOPK + k]
            pltpu.make_async_copy(stage.at[t], xs_hbm.at[s], sem).start()
        return carry

    lax.fori_loop(0, tm, issue, 0)
    for k in range(TOPK):
        pltpu.make_async_copy(stage, xs_hbm.at[pl.ds(0, tm)], sem).wait()


def dispatch(h, slot_tiles, n_slots):
    T, D = h.shape
    tm = TOKEN_TILE
    ch = _row_chunks(D)
    zeros = jnp.zeros((n_slots, ch, 128), F32)
    return pl.pallas_call(
        _dispatch_kernel,
        out_shape=jax.ShapeDtypeStruct((n_slots, ch, 128), F32),
        grid=(T // tm,),
        in_specs=[pl.BlockSpec(memory_space=pl.ANY),
                  pl.BlockSpec((tm, D), lambda i: (i, 0)),
                  pl.BlockSpec(memory_space=pl.ANY)],
        out_specs=pl.BlockSpec(memory_space=pl.ANY),
        scratch_shapes=[pltpu.SMEM((1, tm * TOPK), jnp.int32),
                        pltpu.VMEM((tm, ch, 128), F32),
                        pltpu.SemaphoreType.DMA(()),
                        pltpu.SemaphoreType.DMA(())],
        input_output_aliases={2: 0},
        compiler_params=_cparams(("arbitrary",)),
        name="moe_dispatch",
    )(slot_tiles, h, zeros)


def _expert_kernel(be_ref, nu_ref, xs_ref, wg_ref, wu_ref, wd_ref, ys_ref):
    del be_ref
    used = pl.program_id(0) < nu_ref[0]

    @pl.when(jnp.logical_not(used))
    def _():
        ys_ref[...] = jnp.zeros_like(ys_ref)

    @pl.when(used)
    def _():
        ch = xs_ref.shape[1]
        x = jnp.concatenate([xs_ref[:, j, :] for j in range(ch)], axis=-1).astype(BF16)
        g = jnp.dot(x, wg_ref[0].astype(BF16), preferred_element_type=F32)
        u = jnp.dot(x, wu_ref[0].astype(BF16), preferred_element_type=F32)
        a = (_silu(g) * u).astype(BF16)
        y = jnp.dot(a, wd_ref[0].astype(BF16), preferred_element_type=F32)
        for j in range(ch):
            ys_ref[:, j, :] = y[:, j * 128:(j + 1) * 128]


def routed_experts(xs, block_expert, n_used, w_gate, w_up, w_down):
    n_slots, ch, _ = xs.shape
    E, D, Fd = w_gate.shape
    nb = n_slots // EXPERT_BLOCK
    blk = lambda i, be, nu: (jnp.minimum(i, nu[0] - 1), 0, 0)
    grid_spec = pltpu.PrefetchScalarGridSpec(
        num_scalar_prefetch=2,
        grid=(nb,),
        in_specs=[pl.BlockSpec((EXPERT_BLOCK, ch, 128), blk),
                  pl.BlockSpec((1, D, Fd), lambda i, be, nu: (be[i], 0, 0)),
                  pl.BlockSpec((1, D, Fd), lambda i, be, nu: (be[i], 0, 0)),
                  pl.BlockSpec((1, Fd, D), lambda i, be, nu: (be[i], 0, 0))],
        out_specs=pl.BlockSpec((EXPERT_BLOCK, ch, 128), lambda i, be, nu: (i, 0, 0)),
    )
    return pl.pallas_call(
        _expert_kernel,
        out_shape=jax.ShapeDtypeStruct(xs.shape, F32),
        grid_spec=grid_spec,
        compiler_params=_cparams(("arbitrary",)),
        name="routed_experts",
    )(block_expert, n_used, xs, w_gate, w_up, w_down)


def _combine_kernel(slot_hbm, ys_hbm, wt_ref, h_ref, sg_ref, su_ref, sd_ref, x_ref, gate_ref,
                    lnw_ref, lnb_ref, o_ref, slot_smem, buf, sem_idx, sem, *, alpha):
    tm = h_ref.shape[0]
    ch = buf.shape[2]
    i = pl.program_id(0)
    idx_copy = pltpu.make_async_copy(slot_hbm.at[i], slot_smem, sem_idx)
    idx_copy.start()
    idx_copy.wait()

    def issue(t, carry):
        for k in range(TOPK):
            s = slot_smem[0, t * TOPK + k]
            pltpu.make_async_copy(ys_hbm.at[s], buf.at[k, t], sem).start()
        return carry

    lax.fori_loop(0, tm, issue, 0)

    hb = h_ref[...].astype(BF16)
    g = jnp.dot(hb, sg_ref[...], preferred_element_type=F32)
    u = jnp.dot(hb, su_ref[...], preferred_element_type=F32)
    f = jnp.dot((_silu(g) * u).astype(BF16), sd_ref[...], preferred_element_type=F32)

    for k in range(TOPK):
        pltpu.make_async_copy(ys_hbm.at[pl.ds(0, tm)], buf.at[k], sem).wait()

    wt = wt_ref[...]
    parts = []
    for j in range(ch):
        r = wt[:, 0:1] * buf[0, :, j, :]
        for k in range(1, TOPK):
            r = r + wt[:, k:k + 1] * buf[k, :, j, :]
        parts.append(r)
    f = f + jnp.concatenate(parts, axis=-1)
    o_ref[...] = _ln(alpha * x_ref[...] + gate_ref[0] * f) * lnw_ref[...] + lnb_ref[...]


def combine(slot_tiles, ys, wt, h, sg, su, sd, x, gate_tiles, ln_w, ln_b, alpha):
    T, D = h.shape
    tm = TOKEN_TILE
    ch = ys.shape[1]
    full = lambda a: pl.BlockSpec(a.shape, lambda i: (0,) * a.ndim)
    prm = pl.BlockSpec((1, D), lambda i: (0, 0))
    return pl.pallas_call(
        functools.partial(_combine_kernel, alpha=alpha),
        out_shape=jax.ShapeDtypeStruct((T, D), F32),
        grid=(T // tm,),
        in_specs=[pl.BlockSpec(memory_space=pl.ANY),
                  pl.BlockSpec(memory_space=pl.ANY),
                  pl.BlockSpec((tm, 128), lambda i: (i, 0)),
                  pl.BlockSpec((tm, D), lambda i: (i, 0)),
                  full(sg), full(su), full(sd),
                  pl.BlockSpec((tm, D), lambda i: (i, 0)),
                  pl.BlockSpec((1, 1, D), lambda i: (i, 0, 0)),
                  prm, prm],
        out_specs=pl.BlockSpec((tm, D), lambda i: (i, 0)),
        scratch_shapes=[pltpu.SMEM((1, tm * TOPK), jnp.int32),
                        pltpu.VMEM((TOPK, tm, ch, 128), F32),
                        pltpu.SemaphoreType.DMA(()),
                        pltpu.SemaphoreType.DMA(())],
        compiler_params=_cparams(("arbitrary",)),
        name="moe_combine",
    )(slot_tiles, ys, wt, h, sg, su, sd, x, gate_tiles, ln_w.reshape(1, D), ln_b.reshape(1, D))


def moe_sublayer(h, x, gate_tiles, router_w, router_bias, w_gate, w_up, w_down, sg, su, sd, ln_w, ln_b,
                 alpha):
    T, D = h.shape
    top_e, top_w, rank, counts = router(h, router_w.astype(BF16), router_bias)
    counts = counts[0]
    padded = (counts + EXPERT_BLOCK - 1) // EXPERT_BLOCK * EXPERT_BLOCK
    pad_end = jnp.cumsum(padded)
    pad_start = pad_end - padded
    slot = pad_start[top_e[:, :TOPK]] + rank[:, :TOPK]
    slot_tiles = slot.reshape(T // TOKEN_TILE, 1, TOKEN_TILE * TOPK).astype(jnp.int32)
    n_slots = -(-(T * TOPK + N_EXP * (EXPERT_BLOCK - 1)) // EXPERT_BLOCK) * EXPERT_BLOCK
    nb = n_slots // EXPERT_BLOCK
    block_expert = jnp.minimum(
        jnp.searchsorted(pad_end, jnp.arange(nb, dtype=jnp.int32) * EXPERT_BLOCK, side='right'),
        N_EXP - 1).astype(jnp.int32)
    n_used = (pad_end[-1:] // EXPERT_BLOCK).astype(jnp.int32)

    xs = dispatch(h, slot_tiles, n_slots)
    ys = routed_experts(xs, block_expert, n_used, w_gate, w_up, w_down)
    return combine(slot_tiles, ys, top_w, h, sg.astype(BF16), su.astype(BF16), sd.astype(BF16), x,
                   gate_tiles, ln_w, ln_b, alpha)


def kernel(x, c, ctx, c_ctx, ada_w, ada_b, w_in, na_rpb, w_o_na, w_fourier, ret_decay_fwd, ret_decay_bwd,
           ret_gn_w, w_o_ret, w_out, ln_mix_w, ln_mix_b, router_w, router_bias, exp_w_gate, exp_w_up,
           exp_w_down, sh_w_gate, sh_w_up, sh_w_down, ln_ffn_w, ln_ffn_b):
    B, N, D = x.shape
    L = ctx.shape[1]
    depth = ada_w.shape[0]
    alpha = (2.0 * depth) ** 0.25

    rope_tabs = rope_tables(N)
    cn, sn = (t.astype(BF16) for t in dft_tables(N))
    cl, sl = (t.astype(BF16) for t in dft_tables(L))
    cc, sc = channel_dft_tables()
    zero_state = jnp.zeros((B, RT_HEADS, RT_DK, RT_DV), F32)
    c_rows = jnp.concatenate([c, c_ctx[None, :], jnp.zeros((16 - B - 1, D), F32)], axis=0)

    xc = ctx
    for l in range(depth):
        update_ctx = l < depth - 1
        mod = ada_mod(c_rows, ada_w[l], ada_b[l])
        sh1, sc1, g1, sh2, sc2, g2 = (m[:, None, :] for m in jnp.split(mod[:B], 6, axis=-1))
        mod_c = jnp.broadcast_to(mod[B:B + 1], (B, 6 * D))
        sh1c, sc1c, g1c, sh2c, sc2c, g2c = (m[:, None, :] for m in jnp.split(mod_c, 6, axis=-1))
        lg_f = jax.nn.log_sigmoid(ret_decay_fwd[l].astype(F32))
        lg_b = jax.nn.log_sigmoid(ret_decay_bwd[l].astype(F32))

        w_in_b = w_in[l].astype(BF16)
        proj = ln_proj(x, sh1, sc1, w_in_b)
        proj_c = ln_proj(xc, sh1c, sc1c, w_in_b)

        y_na = window_attention(proj, proj_c, window_bias_tables(na_rpb[l]))
        y_fn = fourier_mix(proj, cn, sn, cc, sc)
        ob_c, s_b = retention(proj_c, lg_b, zero_state, backward=True)
        if update_ctx:
            yrt_c, s_f = retention(proj_c, lg_f, zero_state, backward=False, finish=(ob_c, ret_gn_w[l]))
        else:
            _, s_f = retention(proj_c, lg_f, zero_state, backward=False)
        ob, _ = retention(proj, lg_b, s_b, backward=True, rope_tabs=rope_tabs)
        y_ret, _ = retention(proj, lg_f, s_f, backward=False, rope_tabs=rope_tabs,
                             finish=(ob, ret_gn_w[l]))

        wts = (w_o_na[l].astype(BF16), w_fourier[l].astype(BF16), w_o_ret[l].astype(BF16),
               w_out[l].astype(BF16))
        x, h = merge_branches(y_na, y_fn, y_ret, proj, *wts, x, g1, ln_mix_w[l], ln_mix_b[l], sh2, sc2,
                              alpha)
        h_all = h.reshape(B * N, D)
        x_all = x.reshape(B * N, D)
        gate_tiles = jnp.repeat(g2, N // TOKEN_TILE, axis=0)
        if update_ctx:
            yna_c = context_attention(proj_c)
            yfn_c = fourier_mix(proj_c, cl, sl, cc, sc)
            xc, h_c = merge_branches(yna_c, yfn_c, yrt_c, proj_c, *wts, xc, g1c, ln_mix_w[l],
                                     ln_mix_b[l], sh2c, sc2c, alpha)
            h_all = jnp.concatenate([h_all, h_c.reshape(B * L, D)], axis=0)
            x_all = jnp.concatenate([x_all, xc.reshape(B * L, D)], axis=0)
            gate_tiles = jnp.concatenate([gate_tiles, jnp.repeat(g2c, L // TOKEN_TILE, axis=0)], axis=0)

        out = moe_sublayer(h_all, x_all, gate_tiles, router_w[l], router_bias[l], exp_w_gate[l],
                           exp_w_up[l], exp_w_down[l], sh_w_gate[l], sh_w_up[l], sh_w_down[l],
                           ln_ffn_w[l], ln_ffn_b[l], alpha)
        x = out[:B * N].reshape(B, N, D)
        if update_ctx:
            xc = out[B * N:].reshape(B, L, D)
    return x
```

```python
import functools
import math

import numpy as np
import jax
import jax.numpy as jnp
from jax import lax
from jax.experimental import pallas as pl
from jax.experimental.pallas import tpu as pltpu

F32 = jnp.float32
BF16 = jnp.bfloat16

GRID_COLS = 64
NA_HEADS = 8
NA_DH = 64
NA_W = NA_HEADS * NA_DH
NA_ROWS = 8
NA_COLS = 16
FN_GROUPS = 4
FN_GD = 128
FN_W = FN_GROUPS * FN_GD
RT_HEADS = 4
RT_DK = 128
RT_DV = 256
RT_CHUNK = 128
ROPE_THETA = 10000.0
N_EXP = 256
TOPK = 8
N_GRP = 8
TOPK_GRP = 4
GRP_SZ = N_EXP // N_GRP
ROUTED_SCALE = 2.5
LN_EPS = 1e-6
GN_EPS = 1e-5

COL_QA, COL_KA, COL_VA, COL_U, COL_QR, COL_KR, COL_VR, COL_GR, COL_GL = (
    0, 512, 1024, 1536, 2048, 2560, 3072, 4096, 5120)

VMEM_LIMIT = 56 * 1024 * 1024
LANES = 128
EXPERT_BLOCK = 256
TOKEN_TILE = 256
ATTN_ROWS = 4
NEG = -1e30


def _cparams(sem):
    return pltpu.CompilerParams(dimension_semantics=sem, vmem_limit_bytes=VMEM_LIMIT)


def _ln(x):
    mu = jnp.mean(x, axis=-1, keepdims=True)
    xc = x - mu
    var = jnp.mean(xc * xc, axis=-1, keepdims=True)
    return xc * lax.rsqrt(var + LN_EPS)


def _silu(x):
    return x * jax.nn.sigmoid(x)


def _ada_kernel(c_ref, w_ref, b_ref, o_ref):
    a = _silu(c_ref[...]).astype(BF16)
    o_ref[...] = jnp.dot(a, w_ref[0].astype(BF16), preferred_element_type=F32) + b_ref[0]


def ada_mod(c_rows, w_all, b_all, layer):
    R, D = c_rows.shape
    W = w_all.shape[2]
    tn = 1024
    return pl.pallas_call(
        _ada_kernel,
        out_shape=jax.ShapeDtypeStruct((R, W), F32),
        grid=(W // tn,),
        in_specs=[pl.BlockSpec((R, D), lambda j: (0, 0)),
                  pl.BlockSpec((1, D, tn), lambda j: (layer, 0, j)),
                  pl.BlockSpec((1, 1, tn), lambda j: (layer, 0, j))],
        out_specs=pl.BlockSpec((R, tn), lambda j: (0, j)),
        compiler_params=_cparams(("parallel",)),
        name="ada_mod",
    )(c_rows, w_all, b_all.reshape(b_all.shape[0], 1, W))


def _ln_proj_kernel(x_ref, sh_ref, sc_ref, w_ref, o_ref, h_scr):
    @pl.when(pl.program_id(2) == 0)
    def _():
        h = _ln(x_ref[0]) * (1.0 + sc_ref[0]) + sh_ref[0]
        h_scr[...] = h.astype(BF16)

    o_ref[0] = jnp.dot(h_scr[...], w_ref[...], preferred_element_type=F32).astype(o_ref.dtype)


def ln_proj(x, shift, scale, w_bf16):
    B, N, D = x.shape
    W = w_bf16.shape[1]
    tm = min(N, 1024)
    tn = 1024
    return pl.pallas_call(
        _ln_proj_kernel,
        out_shape=jax.ShapeDtypeStruct((B, N, W), BF16),
        grid=(B, N // tm, W // tn),
        in_specs=[pl.BlockSpec((1, tm, D), lambda b, i, j: (b, i, 0)),
                  pl.BlockSpec((1, 1, D), lambda b, i, j: (b, 0, 0)),
                  pl.BlockSpec((1, 1, D), lambda b, i, j: (b, 0, 0)),
                  pl.BlockSpec((D, tn), lambda b, i, j: (0, j))],
        out_specs=pl.BlockSpec((1, tm, tn), lambda b, i, j: (b, i, j)),
        scratch_shapes=[pltpu.VMEM((tm, D), BF16)],
        compiler_params=_cparams(("parallel", "parallel", "arbitrary")),
        name="ln_proj",
    )(x, shift, scale, w_bf16)


def _softmax_pv(s_parts, v):
    m = jnp.max(s_parts[0], axis=-1, keepdims=True)
    for s in s_parts[1:]:
        m = jnp.maximum(m, jnp.max(s, axis=-1, keepdims=True))
    ps = [jnp.exp(s - m) for s in s_parts]
    l = jnp.sum(ps[0], axis=-1, keepdims=True)
    for p in ps[1:]:
        l = l + jnp.sum(p, axis=-1, keepdims=True)
    p = ps[0] if len(ps) == 1 else jnp.concatenate(ps, axis=-1)
    return jnp.dot(p.astype(BF16), v, preferred_element_type=F32) / l


def _window_span(rpb):
    return NA_ROWS + rpb - 1


def _window_attn_kernel(pat_ref, q_ref, k_ref, v_ref, kc_ref, vc_ref, bias_ref, o_ref, kcat, vcat, *,
                        rows, rpb):
    del pat_ref
    L = kc_ref.shape[1]
    span = _window_span(rpb)
    win = span * GRID_COLS
    i = pl.program_id(1)

    @pl.when(i == 0)
    def _():
        kcat[0:L, :] = kc_ref[0]
        vcat[0:L, :] = vc_ref[0]

    first = jnp.clip(i * rpb - NA_ROWS // 2, 0, rows - span)
    start = pl.multiple_of(first * GRID_COLS, GRID_COLS)
    kcat[L:L + win, :] = k_ref[0, pl.ds(start, win), :]
    vcat[L:L + win, :] = v_ref[0, pl.ds(start, win), :]
    q = q_ref[0] * (NA_DH ** -0.5)
    dn = (((1,), (1,)), ((), ()))
    outs = []
    for h in range(NA_HEADS):
        hs = slice(h * NA_DH, (h + 1) * NA_DH)
        s = lax.dot_general(q[:, hs], kcat[:, hs], dn, preferred_element_type=F32)
        outs.append(_softmax_pv([s[:, :L], s[:, L:] + bias_ref[0, h]], vcat[:, hs]))
    o_ref[0] = jnp.concatenate(outs, axis=-1).astype(o_ref.dtype)


def _ctx_attn_kernel(q_ref, kc_ref, vc_ref, o_ref):
    q = q_ref[0] * (NA_DH ** -0.5)
    kc = kc_ref[0]
    vc = vc_ref[0]
    dn = (((1,), (1,)), ((), ()))
    outs = []
    for h in range(NA_HEADS):
        hs = slice(h * NA_DH, (h + 1) * NA_DH)
        s = lax.dot_general(q[:, hs], kc[:, hs], dn, preferred_element_type=F32)
        outs.append(_softmax_pv([s], vc[:, hs]))
    o_ref[0] = jnp.concatenate(outs, axis=-1).astype(o_ref.dtype)


def window_bias_tables(rpb_table, rows, rpb):
    H = rpb_table.shape[0]
    span = _window_span(rpb)
    cidx = np.arange(GRID_COLS)
    c0 = np.clip(cidx - NA_COLS // 2, 0, GRID_COLS - NA_COLS)
    col_ok = (cidx[None, :] >= c0[:, None]) & (cidx[None, :] < c0[:, None] + NA_COLS)
    dc = np.clip(cidx[None, :] - cidx[:, None], -(NA_COLS - 1), NA_COLS - 1) + (NA_COLS - 1)
    col_sel = (dc[:, :, None] == np.arange(2 * NA_COLS - 1)).astype(np.float32)

    keys, patterns, index = [], [], []
    for blk in range(rows // rpb):
        first = int(np.clip(blk * rpb - NA_ROWS // 2, 0, rows - span))
        key_row = first + np.arange(span)
        r = blk * rpb + np.arange(rpb)
        r0 = np.clip(r - NA_ROWS // 2, 0, rows - NA_ROWS)
        assert first <= r0.min() and r0.max() + NA_ROWS <= first + span
        valid = (key_row[None, :] >= r0[:, None]) & (key_row[None, :] < r0[:, None] + NA_ROWS)
        dr = np.where(valid, key_row[None, :] - r[:, None] + NA_ROWS - 1, -1)
        if dr.tobytes() not in keys:
            keys.append(dr.tobytes())
            patterns.append(dr)
        index.append(keys.index(dr.tobytes()))
    dr = np.stack(patterns)
    row_sel = (dr[..., None] == np.arange(2 * NA_ROWS - 1)).astype(np.float32)
    t = jnp.einsum('hab,prja,qkb->phrqjk', rpb_table.astype(F32), row_sel, col_sel,
                   precision=lax.Precision.HIGHEST)
    ok = (dr >= 0)[:, None, :, None, :, None] & col_ok[None, None, None, :, None, :]
    t = jnp.where(ok, t, NEG)
    tabs = t.reshape(len(patterns), H, rpb * GRID_COLS, span * GRID_COLS)
    return tabs, jnp.asarray(np.array(index, np.int32))


def window_attention(proj, proj_ctx, bias_tabs, bias_index):
    B, N, _ = proj.shape
    L = proj_ctx.shape[1]
    rows = N // GRID_COLS
    tq = bias_tabs.shape[2]
    rpb = tq // GRID_COLS
    win = _window_span(rpb) * GRID_COLS
    cb = lambda off: off // NA_W
    grid_spec = pltpu.PrefetchScalarGridSpec(
        num_scalar_prefetch=1,
        grid=(B, rows // rpb),
        in_specs=[pl.BlockSpec((1, tq, NA_W), lambda b, i, pat: (b, i, cb(COL_QA))),
                  pl.BlockSpec((1, N, NA_W), lambda b, i, pat: (b, 0, cb(COL_KA))),
                  pl.BlockSpec((1, N, NA_W), lambda b, i, pat: (b, 0, cb(COL_VA))),
                  pl.BlockSpec((1, L, NA_W), lambda b, i, pat: (b, 0, cb(COL_KA))),
                  pl.BlockSpec((1, L, NA_W), lambda b, i, pat: (b, 0, cb(COL_VA))),
                  pl.BlockSpec((1, NA_HEADS, tq, win), lambda b, i, pat: (pat[i], 0, 0, 0))],
        out_specs=pl.BlockSpec((1, tq, NA_W), lambda b, i, pat: (b, i, 0)),
        scratch_shapes=[pltpu.VMEM((L + win, NA_W), BF16), pltpu.VMEM((L + win, NA_W), BF16)],
    )
    return pl.pallas_call(
        functools.partial(_window_attn_kernel, rows=rows, rpb=rpb),
        out_shape=jax.ShapeDtypeStruct((B, N, NA_W), BF16),
        grid_spec=grid_spec,
        compiler_params=_cparams(("parallel", "arbitrary")),
        name="window_attention",
    )(bias_index, proj, proj, proj, proj_ctx, proj_ctx, bias_tabs)


def context_attention(proj_ctx):
    B, L, _ = proj_ctx.shape
    cb = lambda off: off // NA_W
    tq = min(L, 128)
    return pl.pallas_call(
        _ctx_attn_kernel,
        out_shape=jax.ShapeDtypeStruct((B, L, NA_W), BF16),
        grid=(B, L // tq),
        in_specs=[pl.BlockSpec((1, tq, NA_W), lambda b, r: (b, r, cb(COL_QA))),
                  pl.BlockSpec((1, L, NA_W), lambda b, r: (b, 0, cb(COL_KA))),
                  pl.BlockSpec((1, L, NA_W), lambda b, r: (b, 0, cb(COL_VA)))],
        out_specs=pl.BlockSpec((1, tq, NA_W), lambda b, r: (b, r, 0)),
        compiler_params=_cparams(("parallel", "arbitrary")),
        name="context_attention",
    )(proj_ctx, proj_ctx, proj_ctx)


def _dft_kernel(cn_ref, sn_ref, u_ref, cc_ref, sc_ref, o_ref, *, norm):
    u = u_ref[0]
    a = jnp.dot(cn_ref[...], u, preferred_element_type=F32).astype(BF16)
    b = jnp.dot(sn_ref[...], u, preferred_element_type=F32).astype(BF16)
    y = (jnp.dot(a, cc_ref[...], preferred_element_type=F32)
         - jnp.dot(b, sc_ref[...], preferred_element_type=F32))
    o_ref[0] = (y * norm).astype(o_ref.dtype)


def dft_tables(n):
    k = jnp.arange(n, dtype=jnp.int32)
    kn = (k[:, None] * k[None, :]) % n
    ang = kn.astype(F32) * (2.0 * math.pi / n)
    return jnp.cos(ang), jnp.sin(ang)


def channel_dft_tables():
    c, s = dft_tables(FN_GD)
    eye = jnp.eye(FN_GROUPS, dtype=F32)
    return jnp.kron(eye, c).astype(BF16), jnp.kron(eye, s).astype(BF16)


def fourier_mix(proj, cn, sn, cc, sc):
    B, N, _ = proj.shape
    tk = min(N, 512)
    norm = 1.0 / math.sqrt(N * FN_GD)
    return pl.pallas_call(
        functools.partial(_dft_kernel, norm=norm),
        out_shape=jax.ShapeDtypeStruct((B, N, FN_W), BF16),
        grid=(N // tk, B),
        in_specs=[pl.BlockSpec((tk, N), lambda i, b: (i, 0)),
                  pl.BlockSpec((tk, N), lambda i, b: (i, 0)),
                  pl.BlockSpec((1, N, FN_W), lambda i, b: (b, 0, COL_U // FN_W)),
                  pl.BlockSpec((FN_W, FN_W), lambda i, b: (0, 0)),
                  pl.BlockSpec((FN_W, FN_W), lambda i, b: (0, 0))],
        out_specs=pl.BlockSpec((1, tk, FN_W), lambda i, b: (b, i, 0)),
        compiler_params=_cparams(("parallel", "arbitrary")),
        name="fourier_mix",
    )(cn, sn, proj, cc, sc)


def _retention_kernel(*refs, backward, rope, finish, cpb):
    it = iter(refs)
    lg_ref = next(it)
    q_ref, k_ref, v_ref = next(it), next(it), next(it)
    if rope:
        cos_ref, sin_ref, perm_ref = next(it), next(it), next(it)
    s0_ref = next(it)
    if finish:
        ob_ref, gate_ref, gn_ref = next(it), next(it), next(it)
    o_ref, sf_ref = next(it), next(it)
    state = next(it)

    C = RT_CHUNK
    h = pl.program_id(1)
    step = pl.program_id(2)

    @pl.when(step == 0)
    def _():
        state[...] = s0_ref[0, 0]

    lg = lg_ref[h]
    pos_r = lax.broadcasted_iota(jnp.int32, (C, 1), 0).astype(F32)
    ci = lax.broadcasted_iota(jnp.int32, (C, C), 0)
    mi = lax.broadcasted_iota(jnp.int32, (C, C), 1)
    if backward:
        dist = (mi - ci).astype(F32)
        band = mi > ci
        k_dec = jnp.exp(lg * pos_r)
        q_dec = jnp.exp(lg * (C - pos_r))
    else:
        dist = (ci - mi).astype(F32)
        band = ci >= mi
        k_dec = jnp.exp(lg * (C - 1 - pos_r))
        q_dec = jnp.exp(lg * (pos_r + 1))
    decay = jnp.where(band, jnp.exp(lg * jnp.maximum(dist, 0.0)), 0.0)
    chunk_dec = jnp.exp(lg * C)
    q_scale = RT_DK ** -0.5

    order = range(cpb - 1, -1, -1) if backward else range(cpb)
    for c in order:
        sl = slice(c * C, (c + 1) * C)
        q = q_ref[0, sl, :].astype(F32)
        k = k_ref[0, sl, :].astype(F32)
        v = v_ref[0, sl, :]
        if rope:
            cos = cos_ref[sl, :]
            sin = sin_ref[sl, :]
            perm = perm_ref[...]
            q = q * cos + jnp.dot(q.astype(BF16), perm, preferred_element_type=F32) * sin
            k = k * cos + jnp.dot(k.astype(BF16), perm, preferred_element_type=F32) * sin
        q = q * q_scale
        s_prev = state[...]
        scores = lax.dot_general(q.astype(BF16), k.astype(BF16), (((1,), (1,)), ((), ())),
                                 preferred_element_type=F32) * decay
        o = jnp.dot(scores.astype(BF16), v, preferred_element_type=F32)
        o = o + jnp.dot((q * q_dec).astype(BF16), s_prev.astype(BF16), preferred_element_type=F32)
        kv = lax.dot_general((k * k_dec).astype(BF16), v, (((0,), (0,)), ((), ())),
                             preferred_element_type=F32)
        state[...] = chunk_dec * s_prev + kv
        if finish:
            o = o + ob_ref[0, sl, :]
            mu = jnp.mean(o, axis=-1, keepdims=True)
            oc = o - mu
            var = jnp.mean(oc * oc, axis=-1, keepdims=True)
            y = oc * lax.rsqrt(var + GN_EPS) * gn_ref[0]
            o = y * _silu(gate_ref[0, sl, :].astype(F32))
        o_ref[0, sl, :] = o.astype(o_ref.dtype)

    @pl.when(step == pl.num_programs(2) - 1)
    def _():
        sf_ref[0, 0] = state[...]


def rope_tables(n):
    t = jnp.arange(n)
    row, col = t // GRID_COLS, t % GRID_COLS
    half = RT_DK // 2
    n_pairs = half // 2
    inv_freq = ROPE_THETA ** (-jnp.arange(n_pairs, dtype=F32) / n_pairs)

    def cs(pos):
        ang = pos.astype(F32)[:, None] * inv_freq[None, :]
        c, s = jnp.cos(ang), jnp.sin(ang)
        return jnp.concatenate([c, c], axis=-1), jnp.concatenate([-s, s], axis=-1)

    cr, sr = cs(row)
    cc, sc = cs(col)
    cos = jnp.concatenate([cr, cc], axis=-1)
    sin = jnp.concatenate([sr, sc], axis=-1)
    idx = jnp.arange(RT_DK)
    src = (idx // half) * half + (idx % half + n_pairs) % half
    perm = (idx[:, None] == src[None, :]).astype(BF16)
    return cos, sin, perm


def retention(proj, log_g, s0, *, backward, rope_tabs=None, finish=None):
    B, N, _ = proj.shape
    nc = N // RT_CHUNK
    cpb = min(nc, 4)
    tb = cpb * RT_CHUNK
    nsteps = nc // cpb
    blk = (lambda s: nsteps - 1 - s) if backward else (lambda s: s)
    rope = rope_tabs is not None
    fin = finish is not None

    in_specs = [pl.BlockSpec(memory_space=pltpu.SMEM),
                pl.BlockSpec((1, tb, RT_DK), lambda b, h, s: (b, blk(s), COL_QR // RT_DK + h)),
                pl.BlockSpec((1, tb, RT_DK), lambda b, h, s: (b, blk(s), COL_KR // RT_DK + h)),
                pl.BlockSpec((1, tb, RT_DV), lambda b, h, s: (b, blk(s), COL_VR // RT_DV + h))]
    args = [log_g, proj, proj, proj]
    if rope:
        in_specs += [pl.BlockSpec((tb, RT_DK), lambda b, h, s: (blk(s), 0)),
                     pl.BlockSpec((tb, RT_DK), lambda b, h, s: (blk(s), 0)),
                     pl.BlockSpec((RT_DK, RT_DK), lambda b, h, s: (0, 0))]
        args += list(rope_tabs)
    in_specs.append(pl.BlockSpec((1, 1, RT_DK, RT_DV), lambda b, h, s: (b, h, 0, 0)))
    args.append(s0)
    if fin:
        o_b, gn_all, layer = finish
        in_specs += [pl.BlockSpec((1, tb, RT_DV), lambda b, h, s: (b, blk(s), h)),
                     pl.BlockSpec((1, tb, RT_DV), lambda b, h, s: (b, blk(s), COL_GR // RT_DV + h)),
                     pl.BlockSpec((1, 1, RT_DV), lambda b, h, s: (layer, 0, h))]
        args += [o_b, proj, gn_all.reshape(gn_all.shape[0], 1, -1)]
    out_dtype = BF16 if fin else F32
    o, s_fin = pl.pallas_call(
        functools.partial(_retention_kernel, backward=backward, rope=rope, finish=fin, cpb=cpb),
        out_shape=(jax.ShapeDtypeStruct((B, N, RT_HEADS * RT_DV), out_dtype),
                   jax.ShapeDtypeStruct((B, RT_HEADS, RT_DK, RT_DV), F32)),
        grid=(B, RT_HEADS, nsteps),
        in_specs=in_specs,
        out_specs=(pl.BlockSpec((1, tb, RT_DV), lambda b, h, s: (b, blk(s), h)),
                   pl.BlockSpec((1, 1, RT_DK, RT_DV), lambda b, h, s: (b, h, 0, 0))),
        scratch_shapes=[pltpu.VMEM((RT_DK, RT_DV), F32)],
        compiler_params=_cparams(("parallel", "parallel", "arbitrary")),
        name="retention_bwd" if backward else "retention_fwd",
    )(*args)
    return o, s_fin


def _merge_kernel(yna_ref, yfn_ref, yrt_ref, g0_ref, g1_ref, g2_ref, wna_ref, wfn_ref, wrt_ref,
                  wout_ref, x_ref, gate_ref, lnw_ref, lnb_ref, sh_ref, sc_ref, xo_ref, h_ref, *, alpha):
    a = jnp.dot(yna_ref[0], wna_ref[...], preferred_element_type=F32)
    y = jax.nn.sigmoid(g0_ref[0].astype(F32)) * a
    b = jnp.dot(yfn_ref[0], wfn_ref[...], preferred_element_type=F32)
    y = y + jax.nn.sigmoid(g1_ref[0].astype(F32)) * b
    c = jnp.dot(yrt_ref[0], wrt_ref[...], preferred_element_type=F32)
    y = y + jax.nn.sigmoid(g2_ref[0].astype(F32)) * c
    y = jnp.dot(y.astype(BF16), wout_ref[...], preferred_element_type=F32)
    xn = _ln(alpha * x_ref[0] + gate_ref[0] * y) * lnw_ref[0] + lnb_ref[0]
    xo_ref[0] = xn
    h_ref[0] = _ln(xn) * (1.0 + sc_ref[0]) + sh_ref[0]


def merge_branches(y_na, y_fn, y_ret, proj, w_na, w_fn, w_rt, w_out, x, gate, ln_w_all, ln_b_all, layer,
                   shift2, scale2, alpha):
    B, N, D = x.shape
    tm = min(N, 512)
    row = lambda w: pl.BlockSpec((1, tm, w), lambda b, i: (b, i, 0))
    glb = lambda j: pl.BlockSpec((1, tm, D), lambda b, i: (b, i, COL_GL // D + j))
    full = lambda a: pl.BlockSpec(a.shape, lambda b, i: (0,) * a.ndim)
    vec = pl.BlockSpec((1, 1, D), lambda b, i: (b, 0, 0))
    prm = pl.BlockSpec((1, 1, D), lambda b, i: (layer, 0, 0))
    depth = ln_w_all.shape[0]
    return pl.pallas_call(
        functools.partial(_merge_kernel, alpha=alpha),
        out_shape=(jax.ShapeDtypeStruct((B, N, D), F32), jax.ShapeDtypeStruct((B, N, D), F32)),
        grid=(B, N // tm),
        in_specs=[row(NA_W), row(FN_W), row(RT_HEADS * RT_DV), glb(0), glb(1), glb(2),
                  full(w_na), full(w_fn), full(w_rt), full(w_out),
                  row(D), vec, prm, prm, vec, vec],
        out_specs=(row(D), row(D)),
        compiler_params=_cparams(("parallel", "parallel")),
        name="merge_branches",
    )(y_na, y_fn, y_ret, proj, proj, proj, w_na, w_fn, w_rt, w_out, x, gate,
      ln_w_all.reshape(depth, 1, D), ln_b_all.reshape(depth, 1, D), shift2, scale2)


def _router_kernel(h_ref, w_ref, b_ref, e_ref, wcol_ref, rk_ref, cnt_ref, cnt_scr):
    tm = h_ref.shape[0]

    @pl.when(pl.program_id(0) == 0)
    def _():
        cnt_scr[...] = jnp.zeros_like(cnt_scr)

    logits = lax.dot_general(w_ref[...], h_ref[...].astype(BF16), (((1,), (1,)), ((), ())),
                             preferred_element_type=F32)
    scores = jax.nn.sigmoid(logits)
    sel = scores + b_ref[...]
    row = lax.broadcasted_iota(jnp.int32, (N_EXP, tm), 0)

    def first_max(vals, rows):
        m = jnp.max(vals, axis=0, keepdims=True)
        idx = jnp.min(jnp.where(vals == m, rows, N_EXP), axis=0, keepdims=True)
        return m, idx

    gscore = []
    grow = lax.broadcasted_iota(jnp.int32, (GRP_SZ, tm), 0)
    for g in range(N_GRP):
        gs = slice(g * GRP_SZ, (g + 1) * GRP_SZ)
        vals, rows = sel[gs], grow + g * GRP_SZ
        m1, i1 = first_max(vals, rows)
        m2, _ = first_max(jnp.where(rows == i1, -jnp.inf, vals), rows)
        gscore.append(m1 + m2)
    cand = []
    for g in range(N_GRP):
        beaten = jnp.zeros((1, tm), jnp.int32)
        for o in range(N_GRP):
            if o == g:
                continue
            ahead = (gscore[o] >= gscore[g]) if o < g else (gscore[o] > gscore[g])
            beaten = beaten + ahead.astype(jnp.int32)
        gs = slice(g * GRP_SZ, (g + 1) * GRP_SZ)
        cand.append(jnp.where(beaten < TOPK_GRP, sel[gs], -jnp.inf))
    cand = jnp.concatenate(cand, axis=0)

    chosen = jnp.zeros((N_EXP, tm), F32)
    ids, wts = [], []
    for _ in range(TOPK):
        _, idx = first_max(cand, row)
        hit = row == idx
        ids.append(idx)
        wts.append(jnp.sum(jnp.where(hit, scores, 0.0), axis=0, keepdims=True))
        chosen = jnp.where(hit, 1.0, chosen)
        cand = jnp.where(hit, -jnp.inf, cand)
    wsum = wts[0]
    for w in wts[1:]:
        wsum = wsum + w

    ti = lax.broadcasted_iota(jnp.int32, (tm, tm), 0)
    tj = lax.broadcasted_iota(jnp.int32, (tm, tm), 1)
    before = (ti < tj).astype(BF16)
    rank_full = jnp.dot(chosen.astype(BF16), before, preferred_element_type=F32) + cnt_scr[...]
    cnt_scr[...] = cnt_scr[...] + jnp.sum(chosen, axis=1, keepdims=True)

    ranks = [jnp.sum(jnp.where(row == ids[k], rank_full, 0.0), axis=0, keepdims=True) for k in range(TOPK)]
    e_ref[...] = jnp.concatenate(ids, axis=0)
    rk_ref[...] = jnp.concatenate(ranks, axis=0).astype(jnp.int32)
    wn = [w / wsum * ROUTED_SCALE for w in wts]
    wpad = jnp.concatenate(wn + [jnp.zeros((LANES - TOPK, tm), F32)], axis=0)
    wcol_ref[...] = wpad.T
    cnt_ref[...] = cnt_scr[...].astype(jnp.int32)


def router(h, w_t_bf16, bias):
    T, D = h.shape
    tm = TOKEN_TILE
    lane_out = pl.BlockSpec((TOPK, tm), lambda i: (0, i))
    return pl.pallas_call(
        _router_kernel,
        out_shape=(jax.ShapeDtypeStruct((TOPK, T), jnp.int32), jax.ShapeDtypeStruct((T, LANES), F32),
                   jax.ShapeDtypeStruct((TOPK, T), jnp.int32), jax.ShapeDtypeStruct((N_EXP, 1), jnp.int32)),
        grid=(T // tm,),
        in_specs=[pl.BlockSpec((tm, D), lambda i: (i, 0)),
                  pl.BlockSpec((N_EXP, D), lambda i: (0, 0)),
                  pl.BlockSpec((N_EXP, 1), lambda i: (0, 0))],
        out_specs=(lane_out, pl.BlockSpec((tm, LANES), lambda i: (i, 0)), lane_out,
                   pl.BlockSpec((N_EXP, 1), lambda i: (0, 0))),
        scratch_shapes=[pltpu.VMEM((N_EXP, 1), F32)],
        compiler_params=_cparams(("arbitrary",)),
        name="router",
    )(h, w_t_bf16, bias.reshape(N_EXP, 1))


def _slot_kernel(e_ref, rk_ref, ps_ref, o_ref):
    tm = e_ref.shape[1]
    row = lax.broadcasted_iota(jnp.int32, (N_EXP, tm), 0)
    ps = ps_ref[...].astype(F32)
    outs = []
    for k in range(TOPK):
        base = jnp.sum(jnp.where(row == e_ref[k:k + 1, :], ps, 0.0), axis=0, keepdims=True)
        outs.append(base.astype(jnp.int32) + rk_ref[k:k + 1, :])
    o_ref[...] = jnp.concatenate(outs, axis=0)


def assign_slots(top_e, rank, pad_start):
    K, T = top_e.shape
    tm = next(t for t in (2048, 1024, 512, 256) if T % t == 0)
    blk = pl.BlockSpec((K, tm), lambda i: (0, i))
    return pl.pallas_call(
        _slot_kernel,
        out_shape=jax.ShapeDtypeStruct((K, T), jnp.int32),
        grid=(T // tm,),
        in_specs=[blk, blk, pl.BlockSpec((N_EXP, 1), lambda i: (0, 0))],
        out_specs=blk,
        compiler_params=_cparams(("parallel",)),
        name="assign_slots",
    )(top_e, rank, pad_start.reshape(N_EXP, 1).astype(jnp.int32))


def _dispatch_kernel(slot_hbm, h_ref, xs_in, xs_hbm, slot_smem, stage, sem_idx, sem):
    del xs_in
    tm = h_ref.shape[0]
    i = pl.program_id(0)
    idx_copy = pltpu.make_async_copy(slot_hbm.at[:, pl.ds(i * tm, tm)], slot_smem, sem_idx)
    idx_copy.start()
    x = h_ref[...]
    for j in range(x.shape[1] // LANES):
        stage[:, j, :] = x[:, j * LANES:(j + 1) * LANES]
    idx_copy.wait()

    def issue(t, carry):
        for k in range(TOPK):
            pltpu.make_async_copy(stage.at[t], xs_hbm.at[slot_smem[k, t]], sem).start(priority=k % 2)
        return carry

    lax.fori_loop(0, tm, issue, 0)
    for k in range(TOPK):
        pltpu.make_async_copy(stage, xs_hbm.at[pl.ds(0, tm)], sem).wait()


def dispatch(h, slots, n_slots):
    T, D = h.shape
    tm = TOKEN_TILE
    ch = D // LANES
    zeros = jnp.zeros((n_slots, ch, LANES), F32)
    return pl.pallas_call(
        _dispatch_kernel,
        out_shape=jax.ShapeDtypeStruct((n_slots, ch, LANES), F32),
        grid=(T // tm,),
        in_specs=[pl.BlockSpec(memory_space=pl.ANY),
                  pl.BlockSpec((tm, D), lambda i: (i, 0)),
                  pl.BlockSpec(memory_space=pl.ANY)],
        out_specs=pl.BlockSpec(memory_space=pl.ANY),
        scratch_shapes=[pltpu.SMEM((TOPK, tm), jnp.int32),
                        pltpu.VMEM((tm, ch, LANES), F32),
                        pltpu.SemaphoreType.DMA(()),
                        pltpu.SemaphoreType.DMA(())],
        input_output_aliases={2: 0},
        compiler_params=_cparams(("arbitrary",)),
        name="moe_dispatch",
    )(slots, h, zeros)


def _expert_kernel(be_ref, nu_ref, xs_ref, wg_ref, wu_ref, wd_ref, ys_ref, wg_s, wu_s, wd_s):
    i = pl.program_id(0)
    used = i < nu_ref[0]
    new_expert = jnp.logical_or(i == 0, be_ref[i] != be_ref[jnp.maximum(i - 1, 0)])

    @pl.when(jnp.logical_not(used))
    def _():
        ys_ref[...] = jnp.zeros_like(ys_ref)

    @pl.when(jnp.logical_and(used, new_expert))
    def _():
        wg_s[...] = wg_ref[0, 0].astype(BF16)
        wu_s[...] = wu_ref[0, 0].astype(BF16)
        wd_s[...] = wd_ref[0, 0].astype(BF16)

    @pl.when(used)
    def _():
        ch = wg_s.shape[0] // LANES
        nrow = xs_ref.shape[0] // ch
        x = jnp.concatenate([xs_ref[pl.ds(j, nrow, stride=ch), :] for j in range(ch)], axis=-1).astype(BF16)
        g = jnp.dot(x, wg_s[...], preferred_element_type=F32)
        u = jnp.dot(x, wu_s[...], preferred_element_type=F32)
        a = (_silu(g) * u).astype(BF16)
        y = jnp.dot(a, wd_s[...], preferred_element_type=F32)
        for j in range(ch):
            ys_ref[pl.ds(j, nrow, stride=ch), :] = y[:, j * LANES:(j + 1) * LANES]


def routed_experts(xs, block_expert, n_used, w_gate_all, w_up_all, w_down_all, layer):
    n_slots, ch, _ = xs.shape
    _, E, D, Fd = w_gate_all.shape
    nb = n_slots // EXPERT_BLOCK
    rows_blk = EXPERT_BLOCK * ch
    wsel = lambda i, be, nu: (layer, be[i], 0, 0)
    grid_spec = pltpu.PrefetchScalarGridSpec(
        num_scalar_prefetch=2,
        grid=(nb,),
        in_specs=[pl.BlockSpec((rows_blk, LANES), lambda i, be, nu: (jnp.minimum(i, nu[0] - 1), 0)),
                  pl.BlockSpec((1, 1, D, Fd), wsel),
                  pl.BlockSpec((1, 1, D, Fd), wsel),
                  pl.BlockSpec((1, 1, Fd, D), wsel)],
        out_specs=pl.BlockSpec((rows_blk, LANES), lambda i, be, nu: (i, 0)),
        scratch_shapes=[pltpu.VMEM((D, Fd), BF16), pltpu.VMEM((D, Fd), BF16), pltpu.VMEM((Fd, D), BF16)],
    )
    ys = pl.pallas_call(
        _expert_kernel,
        out_shape=jax.ShapeDtypeStruct((n_slots * ch, LANES), F32),
        grid_spec=grid_spec,
        compiler_params=_cparams(("arbitrary",)),
        name="routed_experts",
    )(block_expert, n_used, xs.reshape(n_slots * ch, LANES), w_gate_all, w_up_all, w_down_all)
    return ys.reshape(n_slots, ch, LANES)


def _combine_kernel(slot_hbm, ys_hbm, wt_ref, h_ref, sg_ref, su_ref, sd_ref, x_ref, gate_ref,
                    lnw_ref, lnb_ref, o_ref, slot_smem, buf, sem_idx, sem, *, alpha):
    tm = h_ref.shape[0]
    ch = buf.shape[2]
    i = pl.program_id(0)
    cur = i % 2

    def fetch(tile, b):
        idx_copy = pltpu.make_async_copy(slot_hbm.at[:, pl.ds(tile * tm, tm)], slot_smem.at[b], sem_idx)
        idx_copy.start()
        idx_copy.wait()

        def issue(t, carry):
            for k in range(TOPK):
                pltpu.make_async_copy(ys_hbm.at[slot_smem[b, k, t]], buf.at[b, k, :, t, :],
                                      sem.at[b]).start(priority=k % 2)
            return carry

        lax.fori_loop(0, tm, issue, 0)

    @pl.when(i == 0)
    def _():
        fetch(0, 0)

    @pl.when(i + 1 < pl.num_programs(0))
    def _():
        fetch(i + 1, 1 - cur)

    hb = h_ref[...].astype(BF16)
    g = jnp.dot(hb, sg_ref[0], preferred_element_type=F32)
    u = jnp.dot(hb, su_ref[0], preferred_element_type=F32)
    f = jnp.dot((_silu(g) * u).astype(BF16), sd_ref[0], preferred_element_type=F32)

    for k in range(TOPK):
        pltpu.make_async_copy(buf.at[cur, k], buf.at[cur, k], sem.at[cur]).wait()

    wt = wt_ref[...]
    parts = []
    for j in range(ch):
        r = wt[:, 0:1] * buf[cur, 0, j]
        for k in range(1, TOPK):
            r = r + wt[:, k:k + 1] * buf[cur, k, j]
        parts.append(r)
    f = f + jnp.concatenate(parts, axis=-1)
    o_ref[...] = _ln(alpha * x_ref[...] + gate_ref[0] * f) * lnw_ref[0] + lnb_ref[0]


def combine(slots, ys, wt, h, sg, su, sd, x, gate_tiles, ln_w_all, ln_b_all, layer, alpha):
    T, D = h.shape
    tm = TOKEN_TILE
    ch = ys.shape[1]
    depth = ln_w_all.shape[0]
    shw = lambda a: pl.BlockSpec((1,) + a.shape[1:], lambda i: (layer, 0, 0))
    prm = pl.BlockSpec((1, 1, D), lambda i: (layer, 0, 0))
    return pl.pallas_call(
        functools.partial(_combine_kernel, alpha=alpha),
        out_shape=jax.ShapeDtypeStruct((T, D), F32),
        grid=(T // tm,),
        in_specs=[pl.BlockSpec(memory_space=pl.ANY),
                  pl.BlockSpec(memory_space=pl.ANY),
                  pl.BlockSpec((tm, LANES), lambda i: (i, 0)),
                  pl.BlockSpec((tm, D), lambda i: (i, 0)),
                  shw(sg), shw(su), shw(sd),
                  pl.BlockSpec((tm, D), lambda i: (i, 0)),
                  pl.BlockSpec((1, 1, D), lambda i: (i, 0, 0)),
                  prm, prm],
        out_specs=pl.BlockSpec((tm, D), lambda i: (i, 0)),
        scratch_shapes=[pltpu.SMEM((2, TOPK, tm), jnp.int32),
                        pltpu.VMEM((2, TOPK, ch, tm, LANES), F32),
                        pltpu.SemaphoreType.DMA(()),
                        pltpu.SemaphoreType.DMA((2,))],
        compiler_params=_cparams(("arbitrary",)),
        name="moe_combine",
    )(slots, ys, wt, h, sg, su, sd, x, gate_tiles, ln_w_all.reshape(depth, 1, D),
      ln_b_all.reshape(depth, 1, D))


def moe_sublayer(h, x, gate_tiles, router_w_t, router_bias, w_gate_all, w_up_all, w_down_all, sg, su, sd,
                 ln_w_all, ln_b_all, layer, alpha):
    T, D = h.shape
    top_e, w_col, rank, counts = router(h, router_w_t, router_bias)
    counts = counts[:, 0]
    padded = (counts + EXPERT_BLOCK - 1) // EXPERT_BLOCK * EXPERT_BLOCK
    pad_end = jnp.cumsum(padded)
    pad_start = pad_end - padded
    slots = assign_slots(top_e, rank, pad_start)
    n_slots = -(-(T * TOPK + N_EXP * (EXPERT_BLOCK - 1)) // EXPERT_BLOCK) * EXPERT_BLOCK
    nb = n_slots // EXPERT_BLOCK
    block_start = jnp.arange(nb, dtype=jnp.int32) * EXPERT_BLOCK
    block_expert = jnp.minimum(jnp.sum(block_start[:, None] >= pad_end[None, :], axis=1),
                               N_EXP - 1).astype(jnp.int32)
    n_used = (pad_end[-1:] // EXPERT_BLOCK).astype(jnp.int32)

    xs = dispatch(h, slots, n_slots)
    ys = routed_experts(xs, block_expert, n_used, w_gate_all, w_up_all, w_down_all, layer)
    return combine(slots, ys, w_col, h, sg, su, sd, x, gate_tiles, ln_w_all, ln_b_all, layer, alpha)


def kernel(x, c, ctx, c_ctx, ada_w, ada_b, w_in, na_rpb, w_o_na, w_fourier, ret_decay_fwd, ret_decay_bwd,
           ret_gn_w, w_o_ret, w_out, ln_mix_w, ln_mix_b, router_w, router_bias, exp_w_gate, exp_w_up,
           exp_w_down, sh_w_gate, sh_w_up, sh_w_down, ln_ffn_w, ln_ffn_b):
    B, N, D = x.shape
    L = ctx.shape[1]
    depth = ada_w.shape[0]
    alpha = (2.0 * depth) ** 0.25
    rows = N // GRID_COLS
    attn_rows = ATTN_ROWS if (rows % ATTN_ROWS == 0 and rows >= _window_span(ATTN_ROWS)) else 1

    rope_tabs = rope_tables(N)
    cn, sn = (t.astype(BF16) for t in dft_tables(N))
    cl, sl = (t.astype(BF16) for t in dft_tables(L))
    cc, sc = channel_dft_tables()
    zero_state = jnp.zeros((B, RT_HEADS, RT_DK, RT_DV), F32)
    c_rows = jnp.concatenate([c, c_ctx[None, :], jnp.zeros((16 - B - 1, D), F32)], axis=0)
    sg_all, su_all, sd_all = sh_w_gate.astype(BF16), sh_w_up.astype(BF16), sh_w_down.astype(BF16)

    xc = ctx
    for l in range(depth):
        update_ctx = l < depth - 1
        mod = ada_mod(c_rows, ada_w, ada_b, l)
        sh1, sc1, g1, sh2, sc2, g2 = (m[:, None, :] for m in jnp.split(mod[:B], 6, axis=-1))
        mod_c = jnp.broadcast_to(mod[B:B + 1], (B, 6 * D))
        sh1c, sc1c, g1c, sh2c, sc2c, g2c = (m[:, None, :] for m in jnp.split(mod_c, 6, axis=-1))
        lg_f = jax.nn.log_sigmoid(ret_decay_fwd[l].astype(F32))
        lg_b = jax.nn.log_sigmoid(ret_decay_bwd[l].astype(F32))

        w_in_b = w_in[l].astype(BF16)
        proj = ln_proj(x, sh1, sc1, w_in_b)
        proj_c = ln_proj(xc, sh1c, sc1c, w_in_b)

        y_na = window_attention(proj, proj_c, *window_bias_tables(na_rpb[l], rows, attn_rows))
        y_fn = fourier_mix(proj, cn, sn, cc, sc)
        ob_c, s_b = retention(proj_c, lg_b, zero_state, backward=True)
        if update_ctx:
            yrt_c, s_f = retention(proj_c, lg_f, zero_state, backward=False, finish=(ob_c, ret_gn_w, l))
        else:
            _, s_f = retention(proj_c, lg_f, zero_state, backward=False)
        ob, _ = retention(proj, lg_b, s_b, backward=True, rope_tabs=rope_tabs)
        y_ret, _ = retention(proj, lg_f, s_f, backward=False, rope_tabs=rope_tabs,
                             finish=(ob, ret_gn_w, l))

        wts = (w_o_na[l].astype(BF16), w_fourier[l].astype(BF16), w_o_ret[l].astype(BF16),
               w_out[l].astype(BF16))
        x, h = merge_branches(y_na, y_fn, y_ret, proj, *wts, x, g1, ln_mix_w, ln_mix_b, l, sh2, sc2, alpha)
        h_all = h.reshape(B * N, D)
        x_all = x.reshape(B * N, D)
        gate_tiles = jnp.repeat(g2, N // TOKEN_TILE, axis=0)
        if update_ctx:
            yna_c = context_attention(proj_c)
            yfn_c = fourier_mix(proj_c, cl, sl, cc, sc)
            xc, h_c = merge_branches(yna_c, yfn_c, yrt_c, proj_c, *wts, xc, g1c, ln_mix_w, ln_mix_b, l,
                                     sh2c, sc2c, alpha)
            h_all = jnp.concatenate([h_all, h_c.reshape(B * L, D)], axis=0)
            x_all = jnp.concatenate([x_all, xc.reshape(B * L, D)], axis=0)
            gate_tiles = jnp.concatenate([gate_tiles, jnp.repeat(g2c, L // TOKEN_TILE, axis=0)], axis=0)

        out = moe_sublayer(h_all, x_all, gate_tiles, router_w[l].T.astype(BF16), router_bias[l],
                           exp_w_gate, exp_w_up, exp_w_down, sg_all, su_all, sd_all,
                           ln_ffn_w, ln_ffn_b, l, alpha)
        x = out[:B * N].reshape(B, N, D)
        if update_ctx:
            xc = out[B * N:].reshape(B, L, D)
    return x
```

```python
import functools
import math

import numpy as np
import jax
import jax.numpy as jnp
from jax import lax
from jax.experimental import pallas as pl
from jax.experimental.pallas import tpu as pltpu

F32 = jnp.float32
BF16 = jnp.bfloat16
U32 = jnp.uint32

GRID_COLS = 64
NA_HEADS = 8
NA_DH = 64
NA_W = NA_HEADS * NA_DH
NA_ROWS = 8
NA_COLS = 16
FN_GROUPS = 4
FN_GD = 128
FN_W = FN_GROUPS * FN_GD
RT_HEADS = 4
RT_DK = 128
RT_DV = 256
RT_CHUNK = 128
ROPE_THETA = 10000.0
N_EXP = 256
TOPK = 8
N_GRP = 8
TOPK_GRP = 4
GRP_SZ = N_EXP // N_GRP
ROUTED_SCALE = 2.5
LN_EPS = 1e-6
GN_EPS = 1e-5

COL_QA, COL_KA, COL_VA, COL_U, COL_QR, COL_KR, COL_VR, COL_GR, COL_GL = (
    0, 512, 1024, 1536, 2048, 2560, 3072, 4096, 5120)

VMEM_LIMIT = 56 * 1024 * 1024
LANES = 128
EXPERT_BLOCK = 256
TOKEN_TILE = 256
ATTN_ROWS = 4
NEG = -1e30


def _cparams(sem):
    return pltpu.CompilerParams(dimension_semantics=sem, vmem_limit_bytes=VMEM_LIMIT)


def _ln(x):
    mu = jnp.mean(x, axis=-1, keepdims=True)
    xc = x - mu
    var = jnp.mean(xc * xc, axis=-1, keepdims=True)
    return xc * lax.rsqrt(var + LN_EPS)


def _silu(x):
    return x * jax.nn.sigmoid(x)


def _ada_kernel(c_ref, w_ref, b_ref, o_ref):
    a = _silu(c_ref[...]).astype(BF16)
    o_ref[...] = jnp.dot(a, w_ref[0].astype(BF16), preferred_element_type=F32) + b_ref[0]


def ada_mod(c_rows, w_all, b_all, layer):
    R, D = c_rows.shape
    W = w_all.shape[2]
    tn = 1024
    return pl.pallas_call(
        _ada_kernel,
        out_shape=jax.ShapeDtypeStruct((R, W), F32),
        grid=(W // tn,),
        in_specs=[pl.BlockSpec((R, D), lambda j: (0, 0)),
                  pl.BlockSpec((1, D, tn), lambda j: (layer, 0, j)),
                  pl.BlockSpec((1, 1, tn), lambda j: (layer, 0, j))],
        out_specs=pl.BlockSpec((R, tn), lambda j: (0, j)),
        compiler_params=_cparams(("parallel",)),
        name="ada_mod",
    )(c_rows, w_all, b_all.reshape(b_all.shape[0], 1, W))


def _ln_proj_kernel(x_ref, sh_ref, sc_ref, w_ref, o_ref, h_scr):
    @pl.when(pl.program_id(2) == 0)
    def _():
        h = _ln(x_ref[0]) * (1.0 + sc_ref[0]) + sh_ref[0]
        h_scr[...] = h.astype(BF16)

    o_ref[0] = jnp.dot(h_scr[...], w_ref[...], preferred_element_type=F32).astype(o_ref.dtype)


def ln_proj(x, shift, scale, w_bf16):
    B, N, D = x.shape
    W = w_bf16.shape[1]
    tm = min(N, 1024)
    tn = 1024
    return pl.pallas_call(
        _ln_proj_kernel,
        out_shape=jax.ShapeDtypeStruct((B, N, W), BF16),
        grid=(B, N // tm, W // tn),
        in_specs=[pl.BlockSpec((1, tm, D), lambda b, i, j: (b, i, 0)),
                  pl.BlockSpec((1, 1, D), lambda b, i, j: (b, 0, 0)),
                  pl.BlockSpec((1, 1, D), lambda b, i, j: (b, 0, 0)),
                  pl.BlockSpec((D, tn), lambda b, i, j: (0, j))],
        out_specs=pl.BlockSpec((1, tm, tn), lambda b, i, j: (b, i, j)),
        scratch_shapes=[pltpu.VMEM((tm, D), BF16)],
        compiler_params=_cparams(("parallel", "parallel", "arbitrary")),
        name="ln_proj",
    )(x, shift, scale, w_bf16)


def _softmax_pv(s_parts, v):
    m = jnp.max(s_parts[0], axis=-1, keepdims=True)
    for s in s_parts[1:]:
        m = jnp.maximum(m, jnp.max(s, axis=-1, keepdims=True))
    ps = [jnp.exp(s - m) for s in s_parts]
    l = jnp.sum(ps[0], axis=-1, keepdims=True)
    for p in ps[1:]:
        l = l + jnp.sum(p, axis=-1, keepdims=True)
    p = ps[0] if len(ps) == 1 else jnp.concatenate(ps, axis=-1)
    return jnp.dot(p.astype(BF16), v, preferred_element_type=F32) / l


def _window_span(rpb):
    return NA_ROWS + rpb - 1


def _window_attn_kernel(pat_ref, q_ref, k_ref, v_ref, kc_ref, vc_ref, bias_ref, o_ref, kcat, vcat, *,
                        rows, rpb):
    del pat_ref
    L = kc_ref.shape[1]
    span = _window_span(rpb)
    win = span * GRID_COLS
    i = pl.program_id(1)

    @pl.when(i == 0)
    def _():
        kcat[0:L, :] = kc_ref[0]
        vcat[0:L, :] = vc_ref[0]

    first = jnp.clip(i * rpb - NA_ROWS // 2, 0, rows - span)
    start = pl.multiple_of(first * GRID_COLS, GRID_COLS)
    kcat[L:L + win, :] = k_ref[0, pl.ds(start, win), :]
    vcat[L:L + win, :] = v_ref[0, pl.ds(start, win), :]
    q = q_ref[0] * (NA_DH ** -0.5)
    dn = (((1,), (1,)), ((), ()))
    outs = []
    for h in range(NA_HEADS):
        hs = slice(h * NA_DH, (h + 1) * NA_DH)
        s = lax.dot_general(q[:, hs], kcat[:, hs], dn, preferred_element_type=F32)
        outs.append(_softmax_pv([s[:, :L], s[:, L:] + bias_ref[0, h]], vcat[:, hs]))
    o_ref[0] = jnp.concatenate(outs, axis=-1).astype(o_ref.dtype)


def _ctx_attn_kernel(q_ref, kc_ref, vc_ref, o_ref):
    q = q_ref[0] * (NA_DH ** -0.5)
    kc = kc_ref[0]
    vc = vc_ref[0]
    dn = (((1,), (1,)), ((), ()))
    outs = []
    for h in range(NA_HEADS):
        hs = slice(h * NA_DH, (h + 1) * NA_DH)
        s = lax.dot_general(q[:, hs], kc[:, hs], dn, preferred_element_type=F32)
        outs.append(_softmax_pv([s], vc[:, hs]))
    o_ref[0] = jnp.concatenate(outs, axis=-1).astype(o_ref.dtype)


def window_bias_tables(rpb_table, rows, rpb):
    H = rpb_table.shape[0]
    span = _window_span(rpb)
    cidx = np.arange(GRID_COLS)
    c0 = np.clip(cidx - NA_COLS // 2, 0, GRID_COLS - NA_COLS)
    col_ok = (cidx[None, :] >= c0[:, None]) & (cidx[None, :] < c0[:, None] + NA_COLS)
    dc = np.clip(cidx[None, :] - cidx[:, None], -(NA_COLS - 1), NA_COLS - 1) + (NA_COLS - 1)
    col_sel = (dc[:, :, None] == np.arange(2 * NA_COLS - 1)).astype(np.float32)

    keys, patterns, index = [], [], []
    for blk in range(rows // rpb):
        first = int(np.clip(blk * rpb - NA_ROWS // 2, 0, rows - span))
        key_row = first + np.arange(span)
        r = blk * rpb + np.arange(rpb)
        r0 = np.clip(r - NA_ROWS // 2, 0, rows - NA_ROWS)
        assert first <= r0.min() and r0.max() + NA_ROWS <= first + span
        valid = (key_row[None, :] >= r0[:, None]) & (key_row[None, :] < r0[:, None] + NA_ROWS)
        dr = np.where(valid, key_row[None, :] - r[:, None] + NA_ROWS - 1, -1)
        if dr.tobytes() not in keys:
            keys.append(dr.tobytes())
            patterns.append(dr)
        index.append(keys.index(dr.tobytes()))
    dr = np.stack(patterns)
    row_sel = (dr[..., None] == np.arange(2 * NA_ROWS - 1)).astype(np.float32)
    t = jnp.einsum('hab,prja,qkb->phrqjk', rpb_table.astype(F32), row_sel, col_sel,
                   precision=lax.Precision.HIGHEST)
    ok = (dr >= 0)[:, None, :, None, :, None] & col_ok[None, None, None, :, None, :]
    t = jnp.where(ok, t, NEG)
    tabs = t.reshape(len(patterns), H, rpb * GRID_COLS, span * GRID_COLS)
    return tabs, jnp.asarray(np.array(index, np.int32))


def window_attention(proj, proj_ctx, bias_tabs, bias_index):
    B, N, _ = proj.shape
    L = proj_ctx.shape[1]
    rows = N // GRID_COLS
    tq = bias_tabs.shape[2]
    rpb = tq // GRID_COLS
    win = _window_span(rpb) * GRID_COLS
    cb = lambda off: off // NA_W
    grid_spec = pltpu.PrefetchScalarGridSpec(
        num_scalar_prefetch=1,
        grid=(B, rows // rpb),
        in_specs=[pl.BlockSpec((1, tq, NA_W), lambda b, i, pat: (b, i, cb(COL_QA))),
                  pl.BlockSpec((1, N, NA_W), lambda b, i, pat: (b, 0, cb(COL_KA))),
                  pl.BlockSpec((1, N, NA_W), lambda b, i, pat: (b, 0, cb(COL_VA))),
                  pl.BlockSpec((1, L, NA_W), lambda b, i, pat: (b, 0, cb(COL_KA))),
                  pl.BlockSpec((1, L, NA_W), lambda b, i, pat: (b, 0, cb(COL_VA))),
                  pl.BlockSpec((1, NA_HEADS, tq, win), lambda b, i, pat: (pat[i], 0, 0, 0))],
        out_specs=pl.BlockSpec((1, tq, NA_W), lambda b, i, pat: (b, i, 0)),
        scratch_shapes=[pltpu.VMEM((L + win, NA_W), BF16), pltpu.VMEM((L + win, NA_W), BF16)],
    )
    return pl.pallas_call(
        functools.partial(_window_attn_kernel, rows=rows, rpb=rpb),
        out_shape=jax.ShapeDtypeStruct((B, N, NA_W), BF16),
        grid_spec=grid_spec,
        compiler_params=_cparams(("parallel", "arbitrary")),
        name="window_attention",
    )(bias_index, proj, proj, proj, proj_ctx, proj_ctx, bias_tabs)


def context_attention(proj_ctx):
    B, L, _ = proj_ctx.shape
    cb = lambda off: off // NA_W
    tq = min(L, 128)
    return pl.pallas_call(
        _ctx_attn_kernel,
        out_shape=jax.ShapeDtypeStruct((B, L, NA_W), BF16),
        grid=(B, L // tq),
        in_specs=[pl.BlockSpec((1, tq, NA_W), lambda b, r: (b, r, cb(COL_QA))),
                  pl.BlockSpec((1, L, NA_W), lambda b, r: (b, 0, cb(COL_KA))),
                  pl.BlockSpec((1, L, NA_W), lambda b, r: (b, 0, cb(COL_VA)))],
        out_specs=pl.BlockSpec((1, tq, NA_W), lambda b, r: (b, r, 0)),
        compiler_params=_cparams(("parallel", "arbitrary")),
        name="context_attention",
    )(proj_ctx, proj_ctx, proj_ctx)


def _dft_kernel(cn_ref, sn_ref, u_ref, cc_ref, sc_ref, o_ref, *, norm):
    u = u_ref[0]
    a = jnp.dot(cn_ref[...], u, preferred_element_type=F32).astype(BF16)
    b = jnp.dot(sn_ref[...], u, preferred_element_type=F32).astype(BF16)
    y = (jnp.dot(a, cc_ref[...], preferred_element_type=F32)
         - jnp.dot(b, sc_ref[...], preferred_element_type=F32))
    o_ref[0] = (y * norm).astype(o_ref.dtype)


def dft_tables(n):
    k = jnp.arange(n, dtype=jnp.int32)
    kn = (k[:, None] * k[None, :]) % n
    ang = kn.astype(F32) * (2.0 * math.pi / n)
    return jnp.cos(ang), jnp.sin(ang)


def channel_dft_tables():
    c, s = dft_tables(FN_GD)
    eye = jnp.eye(FN_GROUPS, dtype=F32)
    return jnp.kron(eye, c).astype(BF16), jnp.kron(eye, s).astype(BF16)


def fourier_mix(proj, cn, sn, cc, sc):
    B, N, _ = proj.shape
    tk = min(N, 512)
    norm = 1.0 / math.sqrt(N * FN_GD)
    return pl.pallas_call(
        functools.partial(_dft_kernel, norm=norm),
        out_shape=jax.ShapeDtypeStruct((B, N, FN_W), BF16),
        grid=(N // tk, B),
        in_specs=[pl.BlockSpec((tk, N), lambda i, b: (i, 0)),
                  pl.BlockSpec((tk, N), lambda i, b: (i, 0)),
                  pl.BlockSpec((1, N, FN_W), lambda i, b: (b, 0, COL_U // FN_W)),
                  pl.BlockSpec((FN_W, FN_W), lambda i, b: (0, 0)),
                  pl.BlockSpec((FN_W, FN_W), lambda i, b: (0, 0))],
        out_specs=pl.BlockSpec((1, tk, FN_W), lambda i, b: (b, i, 0)),
        compiler_params=_cparams(("parallel", "arbitrary")),
        name="fourier_mix",
    )(cn, sn, proj, cc, sc)


def _retention_kernel(*refs, backward, rope, finish, cpb):
    it = iter(refs)
    lg_ref = next(it)
    q_ref, k_ref, v_ref = next(it), next(it), next(it)
    if rope:
        cos_ref, sin_ref, perm_ref = next(it), next(it), next(it)
    s0_ref = next(it)
    if finish:
        ob_ref, gate_ref, gn_ref = next(it), next(it), next(it)
    o_ref, sf_ref = next(it), next(it)
    state = next(it)

    C = RT_CHUNK
    step = pl.program_id(1)

    @pl.when(step == 0)
    def _():
        state[...] = s0_ref[0]

    pos_r = lax.broadcasted_iota(jnp.int32, (C, 1), 0).astype(F32)
    ci = lax.broadcasted_iota(jnp.int32, (C, C), 0)
    mi = lax.broadcasted_iota(jnp.int32, (C, C), 1)
    if backward:
        dist = jnp.maximum(mi - ci, 0).astype(F32)
        band = mi > ci
    else:
        dist = jnp.maximum(ci - mi, 0).astype(F32)
        band = ci >= mi

    q = q_ref[0]
    k = k_ref[0]
    if rope:
        cos = jnp.concatenate([cos_ref[...]] * RT_HEADS, axis=-1)
        sin = jnp.concatenate([sin_ref[...]] * RT_HEADS, axis=-1)
        perm = perm_ref[...]
        qf = q.astype(F32) * cos + jnp.dot(q, perm, preferred_element_type=F32) * sin
        kf = k.astype(F32) * cos + jnp.dot(k, perm, preferred_element_type=F32) * sin
    else:
        qf = q.astype(F32)
        kf = k.astype(F32)
    qf = qf * (RT_DK ** -0.5)

    order = range(cpb - 1, -1, -1) if backward else range(cpb)
    for h in range(RT_HEADS):
        lg = lg_ref[h]
        if backward:
            k_dec = jnp.exp(lg * pos_r)
            q_dec = jnp.exp(lg * (C - pos_r))
        else:
            k_dec = jnp.exp(lg * (C - 1 - pos_r))
            q_dec = jnp.exp(lg * (pos_r + 1))
        decay = jnp.where(band, jnp.exp(lg * dist), 0.0)
        chunk_dec = jnp.exp(lg * C)
        kcol = slice(h * RT_DK, (h + 1) * RT_DK)
        vcol = slice(h * RT_DV, (h + 1) * RT_DV)
        s_prev = state[h]
        for c in order:
            sl = slice(c * C, (c + 1) * C)
            qc = qf[sl, kcol]
            kc = kf[sl, kcol]
            vc = v_ref[0, sl, vcol]
            scores = lax.dot_general(qc.astype(BF16), kc.astype(BF16), (((1,), (1,)), ((), ())),
                                     preferred_element_type=F32) * decay
            lhs = jnp.concatenate([scores, qc * q_dec], axis=-1).astype(BF16)
            rhs = jnp.concatenate([vc, s_prev.astype(BF16)], axis=0)
            o = jnp.dot(lhs, rhs, preferred_element_type=F32)
            kv = lax.dot_general((kc * k_dec).astype(BF16), vc, (((0,), (0,)), ((), ())),
                                 preferred_element_type=F32)
            s_prev = chunk_dec * s_prev + kv
            if finish:
                o = o + ob_ref[0, sl, vcol]
                mu = jnp.mean(o, axis=-1, keepdims=True)
                oc = o - mu
                var = jnp.mean(oc * oc, axis=-1, keepdims=True)
                y = oc * lax.rsqrt(var + GN_EPS) * gn_ref[0, :, vcol]
                o = y * _silu(gate_ref[0, sl, vcol].astype(F32))
            o_ref[0, sl, vcol] = o.astype(o_ref.dtype)
        state[h] = s_prev

    @pl.when(step == pl.num_programs(1) - 1)
    def _():
        sf_ref[0] = state[...]


def rope_tables(n):
    t = jnp.arange(n)
    row, col = t // GRID_COLS, t % GRID_COLS
    half = RT_DK // 2
    n_pairs = half // 2
    inv_freq = ROPE_THETA ** (-jnp.arange(n_pairs, dtype=F32) / n_pairs)

    def cs(pos):
        ang = pos.astype(F32)[:, None] * inv_freq[None, :]
        c, s = jnp.cos(ang), jnp.sin(ang)
        return jnp.concatenate([c, c], axis=-1), jnp.concatenate([-s, s], axis=-1)

    cr, sr = cs(row)
    cc, sc = cs(col)
    cos = jnp.concatenate([cr, cc], axis=-1)
    sin = jnp.concatenate([sr, sc], axis=-1)
    idx = jnp.arange(RT_DK)
    src = (idx // half) * half + (idx % half + n_pairs) % half
    perm = (idx[:, None] == src[None, :]).astype(F32)
    return cos, sin, jnp.kron(jnp.eye(RT_HEADS, dtype=F32), perm).astype(BF16)


def retention(proj, log_g, s0, *, backward, rope_tabs=None, finish=None):
    B, N, _ = proj.shape
    nc = N // RT_CHUNK
    cpb = min(nc, 4)
    tb = cpb * RT_CHUNK
    nsteps = nc // cpb
    blk = (lambda s: nsteps - 1 - s) if backward else (lambda s: s)
    rope = rope_tabs is not None
    fin = finish is not None

    qk_w = RT_HEADS * RT_DK
    v_w = RT_HEADS * RT_DV
    state_spec = pl.BlockSpec((1, RT_HEADS, RT_DK, RT_DV), lambda b, s: (b, 0, 0, 0))
    in_specs = [pl.BlockSpec(memory_space=pltpu.SMEM),
                pl.BlockSpec((1, tb, qk_w), lambda b, s: (b, blk(s), COL_QR // qk_w)),
                pl.BlockSpec((1, tb, qk_w), lambda b, s: (b, blk(s), COL_KR // qk_w)),
                pl.BlockSpec((1, tb, v_w), lambda b, s: (b, blk(s), COL_VR // v_w))]
    args = [log_g, proj, proj, proj]
    if rope:
        in_specs += [pl.BlockSpec((tb, RT_DK), lambda b, s: (blk(s), 0)),
                     pl.BlockSpec((tb, RT_DK), lambda b, s: (blk(s), 0)),
                     pl.BlockSpec((qk_w, qk_w), lambda b, s: (0, 0))]
        args += list(rope_tabs)
    in_specs.append(state_spec)
    args.append(s0)
    if fin:
        o_b, gn_all, layer = finish
        in_specs += [pl.BlockSpec((1, tb, v_w), lambda b, s: (b, blk(s), 0)),
                     pl.BlockSpec((1, tb, v_w), lambda b, s: (b, blk(s), COL_GR // v_w)),
                     pl.BlockSpec((1, 1, v_w), lambda b, s: (layer, 0, 0))]
        args += [o_b, proj, gn_all.reshape(gn_all.shape[0], 1, -1)]
    out_dtype = BF16 if fin else F32
    o, s_fin = pl.pallas_call(
        functools.partial(_retention_kernel, backward=backward, rope=rope, finish=fin, cpb=cpb),
        out_shape=(jax.ShapeDtypeStruct((B, N, v_w), out_dtype),
                   jax.ShapeDtypeStruct((B, RT_HEADS, RT_DK, RT_DV), F32)),
        grid=(B, nsteps),
        in_specs=in_specs,
        out_specs=(pl.BlockSpec((1, tb, v_w), lambda b, s: (b, blk(s), 0)), state_spec),
        scratch_shapes=[pltpu.VMEM((RT_HEADS, RT_DK, RT_DV), F32)],
        compiler_params=_cparams(("parallel", "arbitrary")),
        name="retention_bwd" if backward else "retention_fwd",
    )(*args)
    return o, s_fin


def _merge_kernel(yna_ref, yfn_ref, yrt_ref, g0_ref, g1_ref, g2_ref, wna_ref, wfn_ref, wrt_ref,
                  wout_ref, x_ref, gate_ref, lnw_ref, lnb_ref, sh_ref, sc_ref, xo_ref, h_ref, *, alpha):
    a = jnp.dot(yna_ref[0], wna_ref[...], preferred_element_type=F32)
    y = jax.nn.sigmoid(g0_ref[0].astype(F32)) * a
    b = jnp.dot(yfn_ref[0], wfn_ref[...], preferred_element_type=F32)
    y = y + jax.nn.sigmoid(g1_ref[0].astype(F32)) * b
    c = jnp.dot(yrt_ref[0], wrt_ref[...], preferred_element_type=F32)
    y = y + jax.nn.sigmoid(g2_ref[0].astype(F32)) * c
    y = jnp.dot(y.astype(BF16), wout_ref[...], preferred_element_type=F32)
    xn = _ln(alpha * x_ref[0] + gate_ref[0] * y) * lnw_ref[0] + lnb_ref[0]
    xo_ref[0] = xn
    h_ref[0] = _ln(xn) * (1.0 + sc_ref[0]) + sh_ref[0]


def merge_branches(y_na, y_fn, y_ret, proj, w_na, w_fn, w_rt, w_out, x, gate, ln_w_all, ln_b_all, layer,
                   shift2, scale2, alpha):
    B, N, D = x.shape
    tm = min(N, 512)
    row = lambda w: pl.BlockSpec((1, tm, w), lambda b, i: (b, i, 0))
    glb = lambda j: pl.BlockSpec((1, tm, D), lambda b, i: (b, i, COL_GL // D + j))
    full = lambda a: pl.BlockSpec(a.shape, lambda b, i: (0,) * a.ndim)
    vec = pl.BlockSpec((1, 1, D), lambda b, i: (b, 0, 0))
    prm = pl.BlockSpec((1, 1, D), lambda b, i: (layer, 0, 0))
    depth = ln_w_all.shape[0]
    return pl.pallas_call(
        functools.partial(_merge_kernel, alpha=alpha),
        out_shape=(jax.ShapeDtypeStruct((B, N, D), F32), jax.ShapeDtypeStruct((B, N, D), F32)),
        grid=(B, N // tm),
        in_specs=[row(NA_W), row(FN_W), row(RT_HEADS * RT_DV), glb(0), glb(1), glb(2),
                  full(w_na), full(w_fn), full(w_rt), full(w_out),
                  row(D), vec, prm, prm, vec, vec],
        out_specs=(row(D), row(D)),
        compiler_params=_cparams(("parallel", "parallel")),
        name="merge_branches",
    )(y_na, y_fn, y_ret, proj, proj, proj, w_na, w_fn, w_rt, w_out, x, gate,
      ln_w_all.reshape(depth, 1, D), ln_b_all.reshape(depth, 1, D), shift2, scale2)


def _router_kernel(h_ref, w_ref, b_ref, e_ref, wcol_ref, rk_ref, cnt_ref, cnt_scr):
    tm = h_ref.shape[0]

    @pl.when(pl.program_id(0) == 0)
    def _():
        cnt_scr[...] = jnp.zeros_like(cnt_scr)

    logits = lax.dot_general(w_ref[...], h_ref[...].astype(BF16), (((1,), (1,)), ((), ())),
                             preferred_element_type=F32)
    scores = jax.nn.sigmoid(logits)
    sel = scores + b_ref[...]
    row = lax.broadcasted_iota(jnp.int32, (N_EXP, tm), 0)

    def first_max(vals, rows):
        m = jnp.max(vals, axis=0, keepdims=True)
        idx = jnp.min(jnp.where(vals == m, rows, N_EXP), axis=0, keepdims=True)
        return m, idx

    gscore = []
    grow = lax.broadcasted_iota(jnp.int32, (GRP_SZ, tm), 0)
    for g in range(N_GRP):
        gs = slice(g * GRP_SZ, (g + 1) * GRP_SZ)
        vals, rows = sel[gs], grow + g * GRP_SZ
        m1, i1 = first_max(vals, rows)
        m2, _ = first_max(jnp.where(rows == i1, -jnp.inf, vals), rows)
        gscore.append(m1 + m2)
    cand = []
    for g in range(N_GRP):
        beaten = jnp.zeros((1, tm), jnp.int32)
        for o in range(N_GRP):
            if o == g:
                continue
            ahead = (gscore[o] >= gscore[g]) if o < g else (gscore[o] > gscore[g])
            beaten = beaten + ahead.astype(jnp.int32)
        gs = slice(g * GRP_SZ, (g + 1) * GRP_SZ)
        cand.append(jnp.where(beaten < TOPK_GRP, sel[gs], -jnp.inf))
    cand = jnp.concatenate(cand, axis=0)

    chosen = jnp.zeros((N_EXP, tm), F32)
    ids, wts = [], []
    for _ in range(TOPK):
        _, idx = first_max(cand, row)
        hit = row == idx
        ids.append(idx)
        wts.append(jnp.sum(jnp.where(hit, scores, 0.0), axis=0, keepdims=True))
        chosen = jnp.where(hit, 1.0, chosen)
        cand = jnp.where(hit, -jnp.inf, cand)
    wsum = wts[0]
    for w in wts[1:]:
        wsum = wsum + w

    ti = lax.broadcasted_iota(jnp.int32, (tm, tm), 0)
    tj = lax.broadcasted_iota(jnp.int32, (tm, tm), 1)
    before = (ti < tj).astype(BF16)
    rank_full = jnp.dot(chosen.astype(BF16), before, preferred_element_type=F32) + cnt_scr[...]
    cnt_scr[...] = cnt_scr[...] + jnp.sum(chosen, axis=1, keepdims=True)

    ranks = [jnp.sum(jnp.where(row == ids[k], rank_full, 0.0), axis=0, keepdims=True) for k in range(TOPK)]
    e_ref[...] = jnp.concatenate(ids, axis=0)
    rk_ref[...] = jnp.concatenate(ranks, axis=0).astype(jnp.int32)
    wn = [w / wsum * ROUTED_SCALE for w in wts]
    wpad = jnp.concatenate(wn + [jnp.zeros((LANES - TOPK, tm), F32)], axis=0)
    wcol_ref[...] = wpad.T
    cnt_ref[...] = cnt_scr[...].astype(jnp.int32)


def router(h, w_t_bf16, bias):
    T, D = h.shape
    tm = TOKEN_TILE
    lane_out = pl.BlockSpec((TOPK, tm), lambda i: (0, i))
    return pl.pallas_call(
        _router_kernel,
        out_shape=(jax.ShapeDtypeStruct((TOPK, T), jnp.int32), jax.ShapeDtypeStruct((T, LANES), F32),
                   jax.ShapeDtypeStruct((TOPK, T), jnp.int32), jax.ShapeDtypeStruct((N_EXP, 1), jnp.int32)),
        grid=(T // tm,),
        in_specs=[pl.BlockSpec((tm, D), lambda i: (i, 0)),
                  pl.BlockSpec((N_EXP, D), lambda i: (0, 0)),
                  pl.BlockSpec((N_EXP, 1), lambda i: (0, 0))],
        out_specs=(lane_out, pl.BlockSpec((tm, LANES), lambda i: (i, 0)), lane_out,
                   pl.BlockSpec((N_EXP, 1), lambda i: (0, 0))),
        scratch_shapes=[pltpu.VMEM((N_EXP, 1), F32)],
        compiler_params=_cparams(("arbitrary",)),
        name="router",
    )(h, w_t_bf16, bias.reshape(N_EXP, 1))


def _slot_kernel(e_ref, rk_ref, ps_ref, o_ref):
    tm = e_ref.shape[1]
    row = lax.broadcasted_iota(jnp.int32, (N_EXP, tm), 0)
    ps = ps_ref[...].astype(F32)
    outs = []
    for k in range(TOPK):
        base = jnp.sum(jnp.where(row == e_ref[k:k + 1, :], ps, 0.0), axis=0, keepdims=True)
        outs.append(base.astype(jnp.int32) + rk_ref[k:k + 1, :])
    o_ref[...] = jnp.concatenate(outs, axis=0)


def assign_slots(top_e, rank, pad_start):
    K, T = top_e.shape
    tm = next(t for t in (2048, 1024, 512, 256) if T % t == 0)
    blk = pl.BlockSpec((K, tm), lambda i: (0, i))
    return pl.pallas_call(
        _slot_kernel,
        out_shape=jax.ShapeDtypeStruct((K, T), jnp.int32),
        grid=(T // tm,),
        in_specs=[blk, blk, pl.BlockSpec((N_EXP, 1), lambda i: (0, 0))],
        out_specs=blk,
        compiler_params=_cparams(("parallel",)),
        name="assign_slots",
    )(top_e, rank, pad_start.reshape(N_EXP, 1).astype(jnp.int32))


def _pack_pair(lo, hi):
    a = lax.bitcast_convert_type(lo.astype(BF16).astype(F32), U32) >> 16
    b = lax.bitcast_convert_type(hi.astype(BF16).astype(F32), U32) & jnp.uint32(0xFFFF0000)
    return a | b


def _unpack_pair(w):
    lo = lax.bitcast_convert_type(w << 16, F32)
    hi = lax.bitcast_convert_type(w & jnp.uint32(0xFFFF0000), F32)
    return lo, hi


def _dispatch_kernel(pe_ref, cnt_ref, nu_ref, slot_hbm, h_ref, xs_hbm, slot_smem, stage, zbuf, sem_idx, sem):
    tm, D = h_ref.shape
    ch = stage.shape[1]
    i = pl.program_id(0)
    nb = xs_hbm.shape[0] // EXPERT_BLOCK

    @pl.when(i == 0)
    def _():
        zbuf[...] = jnp.zeros_like(zbuf)

        def block_copy(b):
            start = pl.multiple_of(b * EXPERT_BLOCK, EXPERT_BLOCK)
            return pltpu.make_async_copy(zbuf, xs_hbm.at[pl.ds(start, EXPERT_BLOCK)], sem)

        def for_partial_blocks(fn):
            def body(e, carry):
                @pl.when(cnt_ref[e] % EXPERT_BLOCK != 0)
                def _():
                    fn(block_copy(pe_ref[e] // EXPERT_BLOCK - 1))
                return carry
            lax.fori_loop(0, N_EXP, body, 0)

        def for_unused_blocks(fn):
            def body(b, carry):
                fn(block_copy(b))
                return carry
            lax.fori_loop(nu_ref[0], nb, body, 0)

        for_partial_blocks(lambda cp: cp.start())
        for_unused_blocks(lambda cp: cp.start())
        for_partial_blocks(lambda cp: cp.wait())
        for_unused_blocks(lambda cp: cp.wait())

    idx_copy = pltpu.make_async_copy(slot_hbm.at[:, pl.ds(i * tm, tm)], slot_smem, sem_idx)
    idx_copy.start()
    x = h_ref[...]
    half = D // 2
    for j in range(ch):
        stage[:, j, :] = _pack_pair(x[:, j * LANES:(j + 1) * LANES],
                                    x[:, half + j * LANES:half + (j + 1) * LANES])
    idx_copy.wait()

    def issue(t, carry):
        for k in range(TOPK):
            pltpu.make_async_copy(stage.at[t], xs_hbm.at[slot_smem[k, t]], sem).start(priority=k % 2)
        return carry

    lax.fori_loop(0, tm, issue, 0)
    for k in range(TOPK):
        pltpu.make_async_copy(stage, xs_hbm.at[pl.ds(0, tm)], sem).wait()


def dispatch(h, slots, n_slots, pad_end, counts, n_used):
    T, D = h.shape
    tm = TOKEN_TILE
    ch = D // 2 // LANES
    grid_spec = pltpu.PrefetchScalarGridSpec(
        num_scalar_prefetch=3,
        grid=(T // tm,),
        in_specs=[pl.BlockSpec(memory_space=pl.ANY),
                  pl.BlockSpec((tm, D), lambda i, pe, cnt, nu: (i, 0))],
        out_specs=pl.BlockSpec(memory_space=pl.ANY),
        scratch_shapes=[pltpu.SMEM((TOPK, tm), jnp.int32),
                        pltpu.VMEM((tm, ch, LANES), U32),
                        pltpu.VMEM((EXPERT_BLOCK, ch, LANES), U32),
                        pltpu.SemaphoreType.DMA(()),
                        pltpu.SemaphoreType.DMA(())],
    )
    return pl.pallas_call(
        _dispatch_kernel,
        out_shape=jax.ShapeDtypeStruct((n_slots, ch, LANES), U32),
        grid_spec=grid_spec,
        compiler_params=_cparams(("arbitrary",)),
        name="moe_dispatch",
    )(pad_end.astype(jnp.int32), counts.astype(jnp.int32), n_used, slots, h)


def _expert_kernel(be_ref, nu_ref, xs_ref, wg_ref, wu_ref, wd_ref, ys_ref, wg_s, wu_s, wd_s):
    i = pl.program_id(0)
    used = i < nu_ref[0]
    new_expert = jnp.logical_or(i == 0, be_ref[i] != be_ref[jnp.maximum(i - 1, 0)])

    @pl.when(jnp.logical_not(used))
    def _():
        ys_ref[...] = jnp.zeros_like(ys_ref)

    @pl.when(jnp.logical_and(used, new_expert))
    def _():
        wg_s[...] = wg_ref[0, 0].astype(BF16)
        wu_s[...] = wu_ref[0, 0].astype(BF16)
        wd_s[...] = wd_ref[0, 0].astype(BF16)

    @pl.when(used)
    def _():
        D = wg_s.shape[0]
        ch = D // 2 // LANES
        nrow = xs_ref.shape[0] // ch
        pairs = [_unpack_pair(xs_ref[pl.ds(j, nrow, stride=ch), :]) for j in range(ch)]
        x = jnp.concatenate([lo.astype(BF16) for lo, _ in pairs] + [hi.astype(BF16) for _, hi in pairs],
                            axis=-1)
        g = jnp.dot(x, wg_s[...], preferred_element_type=F32)
        u = jnp.dot(x, wu_s[...], preferred_element_type=F32)
        a = (_silu(g) * u).astype(BF16)
        y = jnp.dot(a, wd_s[...], preferred_element_type=F32)
        half = D // 2
        for j in range(ch):
            ys_ref[pl.ds(j, nrow, stride=ch), :] = _pack_pair(
                y[:, j * LANES:(j + 1) * LANES], y[:, half + j * LANES:half + (j + 1) * LANES])


def routed_experts(xs, block_expert, n_used, w_gate_all, w_up_all, w_down_all, layer):
    n_slots, ch, _ = xs.shape
    _, E, D, Fd = w_gate_all.shape
    nb = n_slots // EXPERT_BLOCK
    rows_blk = EXPERT_BLOCK * ch
    wsel = lambda i, be, nu: (layer, be[i], 0, 0)
    grid_spec = pltpu.PrefetchScalarGridSpec(
        num_scalar_prefetch=2,
        grid=(nb,),
        in_specs=[pl.BlockSpec((rows_blk, LANES), lambda i, be, nu: (jnp.minimum(i, nu[0] - 1), 0)),
                  pl.BlockSpec((1, 1, D, Fd), wsel),
                  pl.BlockSpec((1, 1, D, Fd), wsel),
                  pl.BlockSpec((1, 1, Fd, D), wsel)],
        out_specs=pl.BlockSpec((rows_blk, LANES), lambda i, be, nu: (i, 0)),
        scratch_shapes=[pltpu.VMEM((D, Fd), BF16), pltpu.VMEM((D, Fd), BF16), pltpu.VMEM((Fd, D), BF16)],
    )
    ys = pl.pallas_call(
        _expert_kernel,
        out_shape=jax.ShapeDtypeStruct((n_slots * ch, LANES), U32),
        grid_spec=grid_spec,
        compiler_params=_cparams(("arbitrary",)),
        name="routed_experts",
    )(block_expert, n_used, xs.reshape(n_slots * ch, LANES), w_gate_all, w_up_all, w_down_all)
    return ys.reshape(n_slots, ch, LANES)


def _combine_kernel(slot_hbm, ys_hbm, wt_ref, h_ref, sg_ref, su_ref, sd_ref, x_ref, gate_ref,
                    lnw_ref, lnb_ref, o_ref, slot_smem, buf, sem_idx, sem, *, alpha):
    tm = h_ref.shape[0]
    ch = buf.shape[2]
    i = pl.program_id(0)
    cur = i % 2

    def fetch(tile, b):
        idx_copy = pltpu.make_async_copy(slot_hbm.at[:, pl.ds(tile * tm, tm)], slot_smem.at[b], sem_idx)
        idx_copy.start()
        idx_copy.wait()

        def issue(t, carry):
            for k in range(TOPK):
                pltpu.make_async_copy(ys_hbm.at[slot_smem[b, k, t]], buf.at[b, k, :, t, :],
                                      sem.at[b]).start(priority=k % 2)
            return carry

        lax.fori_loop(0, tm, issue, 0)

    @pl.when(i == 0)
    def _():
        fetch(0, 0)

    @pl.when(i + 1 < pl.num_programs(0))
    def _():
        fetch(i + 1, 1 - cur)

    hb = h_ref[...].astype(BF16)
    g = jnp.dot(hb, sg_ref[0], preferred_element_type=F32)
    u = jnp.dot(hb, su_ref[0], preferred_element_type=F32)
    f = jnp.dot((_silu(g) * u).astype(BF16), sd_ref[0], preferred_element_type=F32)

    for k in range(TOPK):
        pltpu.make_async_copy(buf.at[cur, k], buf.at[cur, k], sem.at[cur]).wait()

    wt = wt_ref[...]
    los, his = [], []
    for j in range(ch):
        lo, hi = _unpack_pair(buf[cur, 0, j])
        r_lo, r_hi = wt[:, 0:1] * lo, wt[:, 0:1] * hi
        for k in range(1, TOPK):
            lo, hi = _unpack_pair(buf[cur, k, j])
            r_lo, r_hi = r_lo + wt[:, k:k + 1] * lo, r_hi + wt[:, k:k + 1] * hi
        los.append(r_lo)
        his.append(r_hi)
    f = f + jnp.concatenate(los + his, axis=-1)
    o_ref[...] = _ln(alpha * x_ref[...] + gate_ref[0] * f) * lnw_ref[0] + lnb_ref[0]


def combine(slots, ys, wt, h, sg, su, sd, x, gate_tiles, ln_w_all, ln_b_all, layer, alpha):
    T, D = h.shape
    tm = TOKEN_TILE
    ch = ys.shape[1]
    depth = ln_w_all.shape[0]
    shw = lambda a: pl.BlockSpec((1,) + a.shape[1:], lambda i: (layer, 0, 0))
    prm = pl.BlockSpec((1, 1, D), lambda i: (layer, 0, 0))
    return pl.pallas_call(
        functools.partial(_combine_kernel, alpha=alpha),
        out_shape=jax.ShapeDtypeStruct((T, D), F32),
        grid=(T // tm,),
        in_specs=[pl.BlockSpec(memory_space=pl.ANY),
                  pl.BlockSpec(memory_space=pl.ANY),
                  pl.BlockSpec((tm, LANES), lambda i: (i, 0)),
                  pl.BlockSpec((tm, D), lambda i: (i, 0)),
                  shw(sg), shw(su), shw(sd),
                  pl.BlockSpec((tm, D), lambda i: (i, 0)),
                  pl.BlockSpec((1, 1, D), lambda i: (i, 0, 0)),
                  prm, prm],
        out_specs=pl.BlockSpec((tm, D), lambda i: (i, 0)),
        scratch_shapes=[pltpu.SMEM((2, TOPK, tm), jnp.int32),
                        pltpu.VMEM((2, TOPK, ch, tm, LANES), U32),
                        pltpu.SemaphoreType.DMA(()),
                        pltpu.SemaphoreType.DMA((2,))],
        compiler_params=_cparams(("arbitrary",)),
        name="moe_combine",
    )(slots, ys, wt, h, sg, su, sd, x, gate_tiles, ln_w_all.reshape(depth, 1, D),
      ln_b_all.reshape(depth, 1, D))


def moe_sublayer(h, x, gate_tiles, router_w_t, router_bias, w_gate_all, w_up_all, w_down_all, sg, su, sd,
                 ln_w_all, ln_b_all, layer, alpha):
    T, D = h.shape
    top_e, w_col, rank, counts = router(h, router_w_t, router_bias)
    counts = counts[:, 0]
    padded = (counts + EXPERT_BLOCK - 1) // EXPERT_BLOCK * EXPERT_BLOCK
    pad_end = jnp.cumsum(padded)
    pad_start = pad_end - padded
    slots = assign_slots(top_e, rank, pad_start)
    n_slots = -(-(T * TOPK + N_EXP * (EXPERT_BLOCK - 1)) // EXPERT_BLOCK) * EXPERT_BLOCK
    nb = n_slots // EXPERT_BLOCK
    block_start = jnp.arange(nb, dtype=jnp.int32) * EXPERT_BLOCK
    block_expert = jnp.minimum(jnp.sum(block_start[:, None] >= pad_end[None, :], axis=1),
                               N_EXP - 1).astype(jnp.int32)
    n_used = (pad_end[-1:] // EXPERT_BLOCK).astype(jnp.int32)

    xs = dispatch(h, slots, n_slots, pad_end, counts, n_used)
    ys = routed_experts(xs, block_expert, n_used, w_gate_all, w_up_all, w_down_all, layer)
    return combine(slots, ys, w_col, h, sg, su, sd, x, gate_tiles, ln_w_all, ln_b_all, layer, alpha)


def kernel(x, c, ctx, c_ctx, ada_w, ada_b, w_in, na_rpb, w_o_na, w_fourier, ret_decay_fwd, ret_decay_bwd,
           ret_gn_w, w_o_ret, w_out, ln_mix_w, ln_mix_b, router_w, router_bias, exp_w_gate, exp_w_up,
           exp_w_down, sh_w_gate, sh_w_up, sh_w_down, ln_ffn_w, ln_ffn_b):
    B, N, D = x.shape
    L = ctx.shape[1]
    depth = ada_w.shape[0]
    alpha = (2.0 * depth) ** 0.25
    rows = N // GRID_COLS
    attn_rows = ATTN_ROWS if (rows % ATTN_ROWS == 0 and rows >= _window_span(ATTN_ROWS)) else 1

    rope_tabs = rope_tables(N)
    cn, sn = (t.astype(BF16) for t in dft_tables(N))
    cl, sl = (t.astype(BF16) for t in dft_tables(L))
    cc, sc = channel_dft_tables()
    zero_state = jnp.zeros((B, RT_HEADS, RT_DK, RT_DV), F32)
    c_rows = jnp.concatenate([c, c_ctx[None, :], jnp.zeros((16 - B - 1, D), F32)], axis=0)
    sg_all, su_all, sd_all = sh_w_gate.astype(BF16), sh_w_up.astype(BF16), sh_w_down.astype(BF16)

    xc = ctx
    for l in range(depth):
        update_ctx = l < depth - 1
        mod = ada_mod(c_rows, ada_w, ada_b, l)
        sh1, sc1, g1, sh2, sc2, g2 = (m[:, None, :] for m in jnp.split(mod[:B], 6, axis=-1))
        mod_c = jnp.broadcast_to(mod[B:B + 1], (B, 6 * D))
        sh1c, sc1c, g1c, sh2c, sc2c, g2c = (m[:, None, :] for m in jnp.split(mod_c, 6, axis=-1))
        lg_f = jax.nn.log_sigmoid(ret_decay_fwd[l].astype(F32))
        lg_b = jax.nn.log_sigmoid(ret_decay_bwd[l].astype(F32))

        w_in_b = w_in[l].astype(BF16)
        proj = ln_proj(x, sh1, sc1, w_in_b)
        proj_c = ln_proj(xc, sh1c, sc1c, w_in_b)

        y_na = window_attention(proj, proj_c, *window_bias_tables(na_rpb[l], rows, attn_rows))
        y_fn = fourier_mix(proj, cn, sn, cc, sc)
        ob_c, s_b = retention(proj_c, lg_b, zero_state, backward=True)
        if update_ctx:
            yrt_c, s_f = retention(proj_c, lg_f, zero_state, backward=False, finish=(ob_c, ret_gn_w, l))
        else:
            _, s_f = retention(proj_c, lg_f, zero_state, backward=False)
        ob, _ = retention(proj, lg_b, s_b, backward=True, rope_tabs=rope_tabs)
        y_ret, _ = retention(proj, lg_f, s_f, backward=False, rope_tabs=rope_tabs,
                             finish=(ob, ret_gn_w, l))

        wts = (w_o_na[l].astype(BF16), w_fourier[l].astype(BF16), w_o_ret[l].astype(BF16),
               w_out[l].astype(BF16))
        x, h = merge_branches(y_na, y_fn, y_ret, proj, *wts, x, g1, ln_mix_w, ln_mix_b, l, sh2, sc2, alpha)
        h_all = h.reshape(B * N, D)
        x_all = x.reshape(B * N, D)
        gate_tiles = jnp.repeat(g2, N // TOKEN_TILE, axis=0)
        if update_ctx:
            yna_c = context_attention(proj_c)
            yfn_c = fourier_mix(proj_c, cl, sl, cc, sc)
            xc, h_c = merge_branches(yna_c, yfn_c, yrt_c, proj_c, *wts, xc, g1c, ln_mix_w, ln_mix_b, l,
                                     sh2c, sc2c, alpha)
            h_all = jnp.concatenate([h_all, h_c.reshape(B * L, D)], axis=0)
            x_all = jnp.concatenate([x_all, xc.reshape(B * L, D)], axis=0)
            gate_tiles = jnp.concatenate([gate_tiles, jnp.repeat(g2c, L // TOKEN_TILE, axis=0)], axis=0)

        out = moe_sublayer(h_all, x_all, gate_tiles, router_w[l].T.astype(BF16), router_bias[l],
                           exp_w_gate, exp_w_up, exp_w_down, sg_all, su_all, sd_all,
                           ln_ffn_w, ln_ffn_b, l, alpha)
        x = out[:B * N].reshape(B, N, D)
        if update_ctx:
            xc = out[B * N:].reshape(B, L, D)
    return x
```

```python
import functools
import math

import numpy as np
import jax
import jax.numpy as jnp
from jax import lax
from jax.experimental import pallas as pl
from jax.experimental.pallas import tpu as pltpu

F32 = jnp.float32
BF16 = jnp.bfloat16
U32 = jnp.uint32

GRID_COLS = 64
NA_HEADS = 8
NA_DH = 64
NA_W = NA_HEADS * NA_DH
NA_ROWS = 8
NA_COLS = 16
FN_GROUPS = 4
FN_GD = 128
FN_W = FN_GROUPS * FN_GD
RT_HEADS = 4
RT_DK = 128
RT_DV = 256
RT_CHUNK = 128
ROPE_THETA = 10000.0
N_EXP = 256
TOPK = 8
N_GRP = 8
TOPK_GRP = 4
GRP_SZ = N_EXP // N_GRP
ROUTED_SCALE = 2.5
LN_EPS = 1e-6
GN_EPS = 1e-5

COL_QA, COL_KA, COL_VA, COL_U, COL_QR, COL_KR, COL_VR, COL_GR, COL_GL = (
    0, 512, 1024, 1536, 2048, 2560, 3072, 4096, 5120)

VMEM_LIMIT = 56 * 1024 * 1024
LANES = 128
MXU_ROWS = 256
EXPERT_BLOCK = 2 * MXU_ROWS
TOKEN_TILE = 256
ATTN_ROWS = 4
NEG = -1e30


def _cparams(sem):
    return pltpu.CompilerParams(dimension_semantics=sem, vmem_limit_bytes=VMEM_LIMIT)


def _ln(x):
    mu = jnp.mean(x, axis=-1, keepdims=True)
    xc = x - mu
    var = jnp.mean(xc * xc, axis=-1, keepdims=True)
    return xc * lax.rsqrt(var + LN_EPS)


def _silu(x):
    return x * jax.nn.sigmoid(x)


def _ada_kernel(c_ref, w_ref, b_ref, o_ref):
    a = _silu(c_ref[...]).astype(BF16)
    o_ref[...] = jnp.dot(a, w_ref[0].astype(BF16), preferred_element_type=F32) + b_ref[0]


def ada_mod(c_rows, w_all, b_all, layer):
    R, D = c_rows.shape
    W = w_all.shape[2]
    tn = 1024
    return pl.pallas_call(
        _ada_kernel,
        out_shape=jax.ShapeDtypeStruct((R, W), F32),
        grid=(W // tn,),
        in_specs=[pl.BlockSpec((R, D), lambda j: (0, 0)),
                  pl.BlockSpec((1, D, tn), lambda j: (layer, 0, j)),
                  pl.BlockSpec((1, 1, tn), lambda j: (layer, 0, j))],
        out_specs=pl.BlockSpec((R, tn), lambda j: (0, j)),
        compiler_params=_cparams(("parallel",)),
        name="ada_mod",
    )(c_rows, w_all, b_all.reshape(b_all.shape[0], 1, W))


def _ln_proj_kernel(x_ref, sh_ref, sc_ref, w_ref, o_ref, h_scr):
    @pl.when(pl.program_id(2) == 0)
    def _():
        h = _ln(x_ref[0]) * (1.0 + sc_ref[0]) + sh_ref[0]
        h_scr[...] = h.astype(BF16)

    o_ref[0] = jnp.dot(h_scr[...], w_ref[...], preferred_element_type=F32).astype(o_ref.dtype)


def ln_proj(x, shift, scale, w_bf16):
    B, N, D = x.shape
    W = w_bf16.shape[1]
    tm = min(N, 1024)
    tn = 1024
    return pl.pallas_call(
        _ln_proj_kernel,
        out_shape=jax.ShapeDtypeStruct((B, N, W), BF16),
        grid=(B, N // tm, W // tn),
        in_specs=[pl.BlockSpec((1, tm, D), lambda b, i, j: (b, i, 0)),
                  pl.BlockSpec((1, 1, D), lambda b, i, j: (b, 0, 0)),
                  pl.BlockSpec((1, 1, D), lambda b, i, j: (b, 0, 0)),
                  pl.BlockSpec((D, tn), lambda b, i, j: (0, j))],
        out_specs=pl.BlockSpec((1, tm, tn), lambda b, i, j: (b, i, j)),
        scratch_shapes=[pltpu.VMEM((tm, D), BF16)],
        compiler_params=_cparams(("parallel", "parallel", "arbitrary")),
        name="ln_proj",
    )(x, shift, scale, w_bf16)


def _softmax_pv(s_parts, v):
    m = jnp.max(s_parts[0], axis=-1, keepdims=True)
    for s in s_parts[1:]:
        m = jnp.maximum(m, jnp.max(s, axis=-1, keepdims=True))
    ps = [jnp.exp(s - m) for s in s_parts]
    l = jnp.sum(ps[0], axis=-1, keepdims=True)
    for p in ps[1:]:
        l = l + jnp.sum(p, axis=-1, keepdims=True)
    p = ps[0] if len(ps) == 1 else jnp.concatenate(ps, axis=-1)
    return jnp.dot(p.astype(BF16), v, preferred_element_type=F32) / l


def _window_span(rpb):
    return NA_ROWS + rpb - 1


def _window_attn_kernel(pat_ref, q_ref, k_ref, v_ref, kc_ref, vc_ref, bias_ref, o_ref, kcat, vcat, *,
                        rows, rpb):
    del pat_ref
    L = kc_ref.shape[1]
    span = _window_span(rpb)
    win = span * GRID_COLS
    i = pl.program_id(1)

    @pl.when(i == 0)
    def _():
        kcat[0:L, :] = kc_ref[0]
        vcat[0:L, :] = vc_ref[0]

    first = jnp.clip(i * rpb - NA_ROWS // 2, 0, rows - span)
    start = pl.multiple_of(first * GRID_COLS, GRID_COLS)
    kcat[L:L + win, :] = k_ref[0, pl.ds(start, win), :]
    vcat[L:L + win, :] = v_ref[0, pl.ds(start, win), :]
    q = q_ref[0] * (NA_DH ** -0.5)
    dn = (((1,), (1,)), ((), ()))
    outs = []
    for h in range(NA_HEADS):
        hs = slice(h * NA_DH, (h + 1) * NA_DH)
        s = lax.dot_general(q[:, hs], kcat[:, hs], dn, preferred_element_type=F32)
        outs.append(_softmax_pv([s[:, :L], s[:, L:] + bias_ref[0, h]], vcat[:, hs]))
    o_ref[0] = jnp.concatenate(outs, axis=-1).astype(o_ref.dtype)


def _ctx_attn_kernel(q_ref, kc_ref, vc_ref, o_ref):
    q = q_ref[0] * (NA_DH ** -0.5)
    kc = kc_ref[0]
    vc = vc_ref[0]
    dn = (((1,), (1,)), ((), ()))
    outs = []
    for h in range(NA_HEADS):
        hs = slice(h * NA_DH, (h + 1) * NA_DH)
        s = lax.dot_general(q[:, hs], kc[:, hs], dn, preferred_element_type=F32)
        outs.append(_softmax_pv([s], vc[:, hs]))
    o_ref[0] = jnp.concatenate(outs, axis=-1).astype(o_ref.dtype)


def window_bias_tables(rpb_table, rows, rpb):
    H = rpb_table.shape[0]
    span = _window_span(rpb)
    cidx = np.arange(GRID_COLS)
    c0 = np.clip(cidx - NA_COLS // 2, 0, GRID_COLS - NA_COLS)
    col_ok = (cidx[None, :] >= c0[:, None]) & (cidx[None, :] < c0[:, None] + NA_COLS)
    dc = np.clip(cidx[None, :] - cidx[:, None], -(NA_COLS - 1), NA_COLS - 1) + (NA_COLS - 1)
    col_sel = (dc[:, :, None] == np.arange(2 * NA_COLS - 1)).astype(np.float32)

    keys, patterns, index = [], [], []
    for blk in range(rows // rpb):
        first = int(np.clip(blk * rpb - NA_ROWS // 2, 0, rows - span))
        key_row = first + np.arange(span)
        r = blk * rpb + np.arange(rpb)
        r0 = np.clip(r - NA_ROWS // 2, 0, rows - NA_ROWS)
        assert first <= r0.min() and r0.max() + NA_ROWS <= first + span
        valid = (key_row[None, :] >= r0[:, None]) & (key_row[None, :] < r0[:, None] + NA_ROWS)
        dr = np.where(valid, key_row[None, :] - r[:, None] + NA_ROWS - 1, -1)
        if dr.tobytes() not in keys:
            keys.append(dr.tobytes())
            patterns.append(dr)
        index.append(keys.index(dr.tobytes()))
    dr = np.stack(patterns)
    row_sel = (dr[..., None] == np.arange(2 * NA_ROWS - 1)).astype(np.float32)
    t = jnp.einsum('hab,prja,qkb->phrqjk', rpb_table.astype(F32), row_sel, col_sel,
                   precision=lax.Precision.HIGHEST)
    ok = (dr >= 0)[:, None, :, None, :, None] & col_ok[None, None, None, :, None, :]
    t = jnp.where(ok, t, NEG)
    tabs = t.reshape(len(patterns), H, rpb * GRID_COLS, span * GRID_COLS)
    return tabs, jnp.asarray(np.array(index, np.int32))


def window_attention(proj, proj_ctx, bias_tabs, bias_index):
    B, N, _ = proj.shape
    L = proj_ctx.shape[1]
    rows = N // GRID_COLS
    tq = bias_tabs.shape[2]
    rpb = tq // GRID_COLS
    win = _window_span(rpb) * GRID_COLS
    cb = lambda off: off // NA_W
    grid_spec = pltpu.PrefetchScalarGridSpec(
        num_scalar_prefetch=1,
        grid=(B, rows // rpb),
        in_specs=[pl.BlockSpec((1, tq, NA_W), lambda b, i, pat: (b, i, cb(COL_QA))),
                  pl.BlockSpec((1, N, NA_W), lambda b, i, pat: (b, 0, cb(COL_KA))),
                  pl.BlockSpec((1, N, NA_W), lambda b, i, pat: (b, 0, cb(COL_VA))),
                  pl.BlockSpec((1, L, NA_W), lambda b, i, pat: (b, 0, cb(COL_KA))),
                  pl.BlockSpec((1, L, NA_W), lambda b, i, pat: (b, 0, cb(COL_VA))),
                  pl.BlockSpec((1, NA_HEADS, tq, win), lambda b, i, pat: (pat[i], 0, 0, 0))],
        out_specs=pl.BlockSpec((1, tq, NA_W), lambda b, i, pat: (b, i, 0)),
        scratch_shapes=[pltpu.VMEM((L + win, NA_W), BF16), pltpu.VMEM((L + win, NA_W), BF16)],
    )
    return pl.pallas_call(
        functools.partial(_window_attn_kernel, rows=rows, rpb=rpb),
        out_shape=jax.ShapeDtypeStruct((B, N, NA_W), BF16),
        grid_spec=grid_spec,
        compiler_params=_cparams(("parallel", "arbitrary")),
        name="window_attention",
    )(bias_index, proj, proj, proj, proj_ctx, proj_ctx, bias_tabs)


def context_attention(proj_ctx):
    B, L, _ = proj_ctx.shape
    cb = lambda off: off // NA_W
    tq = min(L, 128)
    return pl.pallas_call(
        _ctx_attn_kernel,
        out_shape=jax.ShapeDtypeStruct((B, L, NA_W), BF16),
        grid=(B, L // tq),
        in_specs=[pl.BlockSpec((1, tq, NA_W), lambda b, r: (b, r, cb(COL_QA))),
                  pl.BlockSpec((1, L, NA_W), lambda b, r: (b, 0, cb(COL_KA))),
                  pl.BlockSpec((1, L, NA_W), lambda b, r: (b, 0, cb(COL_VA)))],
        out_specs=pl.BlockSpec((1, tq, NA_W), lambda b, r: (b, r, 0)),
        compiler_params=_cparams(("parallel", "arbitrary")),
        name="context_attention",
    )(proj_ctx, proj_ctx, proj_ctx)


def _dft_kernel(cn_ref, sn_ref, u_ref, cc_ref, sc_ref, o_ref, *, norm):
    u = u_ref[0]
    a = jnp.dot(cn_ref[...], u, preferred_element_type=F32).astype(BF16)
    b = jnp.dot(sn_ref[...], u, preferred_element_type=F32).astype(BF16)
    y = (jnp.dot(a, cc_ref[...], preferred_element_type=F32)
         - jnp.dot(b, sc_ref[...], preferred_element_type=F32))
    o_ref[0] = (y * norm).astype(o_ref.dtype)


def _unit_circle(rows, cols, period):
    ang = ((rows[:, None] * cols[None, :]) % period).astype(F32) * (2.0 * math.pi / period)
    return jnp.cos(ang), jnp.sin(ang)


def dft_tables(n):
    m = jnp.arange(n, dtype=jnp.int32)
    split = GRID_COLS
    if n <= split * split // 16 or n % split:
        return _unit_circle(m, m, n)
    hi_c, hi_s = _unit_circle(jnp.arange(n // split, dtype=jnp.int32), m, n // split)
    lo_c, lo_s = _unit_circle(jnp.arange(split, dtype=jnp.int32), m, n)
    cos = hi_c[:, None, :] * lo_c[None, :, :] - hi_s[:, None, :] * lo_s[None, :, :]
    sin = hi_s[:, None, :] * lo_c[None, :, :] + hi_c[:, None, :] * lo_s[None, :, :]
    return cos.reshape(n, n), sin.reshape(n, n)


def channel_dft_tables():
    c, s = dft_tables(FN_GD)
    eye = jnp.eye(FN_GROUPS, dtype=F32)
    return jnp.kron(eye, c).astype(BF16), jnp.kron(eye, s).astype(BF16)


def fourier_mix(proj, cn, sn, cc, sc):
    B, N, _ = proj.shape
    tk = min(N, 512)
    norm = 1.0 / math.sqrt(N * FN_GD)
    return pl.pallas_call(
        functools.partial(_dft_kernel, norm=norm),
        out_shape=jax.ShapeDtypeStruct((B, N, FN_W), BF16),
        grid=(N // tk, B),
        in_specs=[pl.BlockSpec((tk, N), lambda i, b: (i, 0)),
                  pl.BlockSpec((tk, N), lambda i, b: (i, 0)),
                  pl.BlockSpec((1, N, FN_W), lambda i, b: (b, 0, COL_U // FN_W)),
                  pl.BlockSpec((FN_W, FN_W), lambda i, b: (0, 0)),
                  pl.BlockSpec((FN_W, FN_W), lambda i, b: (0, 0))],
        out_specs=pl.BlockSpec((1, tk, FN_W), lambda i, b: (b, i, 0)),
        compiler_params=_cparams(("parallel", "arbitrary")),
        name="fourier_mix",
    )(cn, sn, proj, cc, sc)


def _retention_kernel(*refs, backward, rope, finish, cpb):
    it = iter(refs)
    lg_ref = next(it)
    q_ref, k_ref, v_ref = next(it), next(it), next(it)
    if rope:
        cos_ref, sin_ref, perm_ref = next(it), next(it), next(it)
    s0_ref = next(it)
    if finish:
        ob_ref, gate_ref, gn_ref = next(it), next(it), next(it)
    o_ref, sf_ref = next(it), next(it)
    state = next(it)

    C = RT_CHUNK
    step = pl.program_id(1)

    @pl.when(step == 0)
    def _():
        state[...] = s0_ref[0]

    pos_r = lax.broadcasted_iota(jnp.int32, (C, 1), 0).astype(F32)
    ci = lax.broadcasted_iota(jnp.int32, (C, C), 0)
    mi = lax.broadcasted_iota(jnp.int32, (C, C), 1)
    if backward:
        dist = jnp.maximum(mi - ci, 0).astype(F32)
        band = mi > ci
    else:
        dist = jnp.maximum(ci - mi, 0).astype(F32)
        band = ci >= mi

    q = q_ref[0]
    k = k_ref[0]
    if rope:
        cos = jnp.concatenate([cos_ref[...]] * RT_HEADS, axis=-1)
        sin = jnp.concatenate([sin_ref[...]] * RT_HEADS, axis=-1)
        perm = perm_ref[...]
        qf = q.astype(F32) * cos + jnp.dot(q, perm, preferred_element_type=F32) * sin
        kf = k.astype(F32) * cos + jnp.dot(k, perm, preferred_element_type=F32) * sin
    else:
        qf = q.astype(F32)
        kf = k.astype(F32)
    qf = qf * (RT_DK ** -0.5)

    order = range(cpb - 1, -1, -1) if backward else range(cpb)
    for h in range(RT_HEADS):
        lg = lg_ref[h]
        if backward:
            k_dec = jnp.exp(lg * pos_r)
            q_dec = jnp.exp(lg * (C - pos_r))
        else:
            k_dec = jnp.exp(lg * (C - 1 - pos_r))
            q_dec = jnp.exp(lg * (pos_r + 1))
        decay = jnp.where(band, jnp.exp(lg * dist), 0.0)
        chunk_dec = jnp.exp(lg * C)
        kcol = slice(h * RT_DK, (h + 1) * RT_DK)
        vcol = slice(h * RT_DV, (h + 1) * RT_DV)
        s_prev = state[h]
        for c in order:
            sl = slice(c * C, (c + 1) * C)
            qc = qf[sl, kcol]
            kc = kf[sl, kcol]
            vc = v_ref[0, sl, vcol]
            scores = lax.dot_general(qc.astype(BF16), kc.astype(BF16), (((1,), (1,)), ((), ())),
                                     preferred_element_type=F32) * decay
            lhs = jnp.concatenate([scores, qc * q_dec], axis=-1).astype(BF16)
            rhs = jnp.concatenate([vc, s_prev.astype(BF16)], axis=0)
            o = jnp.dot(lhs, rhs, preferred_element_type=F32)
            kv = lax.dot_general((kc * k_dec).astype(BF16), vc, (((0,), (0,)), ((), ())),
                                 preferred_element_type=F32)
            s_prev = chunk_dec * s_prev + kv
            if finish:
                o = o + ob_ref[0, sl, vcol]
                mu = jnp.mean(o, axis=-1, keepdims=True)
                oc = o - mu
                var = jnp.mean(oc * oc, axis=-1, keepdims=True)
                y = oc * lax.rsqrt(var + GN_EPS) * gn_ref[0, :, vcol]
                o = y * _silu(gate_ref[0, sl, vcol].astype(F32))
            o_ref[0, sl, vcol] = o.astype(o_ref.dtype)
        state[h] = s_prev

    @pl.when(step == pl.num_programs(1) - 1)
    def _():
        sf_ref[0] = state[...]


def rope_tables(n):
    t = jnp.arange(n)
    row, col = t // GRID_COLS, t % GRID_COLS
    half = RT_DK // 2
    n_pairs = half // 2
    inv_freq = ROPE_THETA ** (-jnp.arange(n_pairs, dtype=F32) / n_pairs)

    def cs(pos):
        ang = pos.astype(F32)[:, None] * inv_freq[None, :]
        c, s = jnp.cos(ang), jnp.sin(ang)
        return jnp.concatenate([c, c], axis=-1), jnp.concatenate([-s, s], axis=-1)

    cr, sr = cs(row)
    cc, sc = cs(col)
    cos = jnp.concatenate([cr, cc], axis=-1)
    sin = jnp.concatenate([sr, sc], axis=-1)
    idx = jnp.arange(RT_DK)
    src = (idx // half) * half + (idx % half + n_pairs) % half
    perm = (idx[:, None] == src[None, :]).astype(F32)
    return cos, sin, jnp.kron(jnp.eye(RT_HEADS, dtype=F32), perm).astype(BF16)


def retention(proj, log_g, s0, *, backward, rope_tabs=None, finish=None):
    B, N, _ = proj.shape
    nc = N // RT_CHUNK
    cpb = min(nc, 4)
    tb = cpb * RT_CHUNK
    nsteps = nc // cpb
    blk = (lambda s: nsteps - 1 - s) if backward else (lambda s: s)
    rope = rope_tabs is not None
    fin = finish is not None

    qk_w = RT_HEADS * RT_DK
    v_w = RT_HEADS * RT_DV
    state_spec = pl.BlockSpec((1, RT_HEADS, RT_DK, RT_DV), lambda b, s: (b, 0, 0, 0))
    in_specs = [pl.BlockSpec(memory_space=pltpu.SMEM),
                pl.BlockSpec((1, tb, qk_w), lambda b, s: (b, blk(s), COL_QR // qk_w)),
                pl.BlockSpec((1, tb, qk_w), lambda b, s: (b, blk(s), COL_KR // qk_w)),
                pl.BlockSpec((1, tb, v_w), lambda b, s: (b, blk(s), COL_VR // v_w))]
    args = [log_g, proj, proj, proj]
    if rope:
        in_specs += [pl.BlockSpec((tb, RT_DK), lambda b, s: (blk(s), 0)),
                     pl.BlockSpec((tb, RT_DK), lambda b, s: (blk(s), 0)),
                     pl.BlockSpec((qk_w, qk_w), lambda b, s: (0, 0))]
        args += list(rope_tabs)
    in_specs.append(state_spec)
    args.append(s0)
    if fin:
        o_b, gn_all, layer = finish
        in_specs += [pl.BlockSpec((1, tb, v_w), lambda b, s: (b, blk(s), 0)),
                     pl.BlockSpec((1, tb, v_w), lambda b, s: (b, blk(s), COL_GR // v_w)),
                     pl.BlockSpec((1, 1, v_w), lambda b, s: (layer, 0, 0))]
        args += [o_b, proj, gn_all.reshape(gn_all.shape[0], 1, -1)]
    out_dtype = BF16 if fin else F32
    o, s_fin = pl.pallas_call(
        functools.partial(_retention_kernel, backward=backward, rope=rope, finish=fin, cpb=cpb),
        out_shape=(jax.ShapeDtypeStruct((B, N, v_w), out_dtype),
                   jax.ShapeDtypeStruct((B, RT_HEADS, RT_DK, RT_DV), F32)),
        grid=(B, nsteps),
        in_specs=in_specs,
        out_specs=(pl.BlockSpec((1, tb, v_w), lambda b, s: (b, blk(s), 0)), state_spec),
        scratch_shapes=[pltpu.VMEM((RT_HEADS, RT_DK, RT_DV), F32)],
        compiler_params=_cparams(("parallel", "arbitrary")),
        name="retention_bwd" if backward else "retention_fwd",
    )(*args)
    return o, s_fin


def _merge_kernel(yna_ref, yfn_ref, yrt_ref, g0_ref, g1_ref, g2_ref, wna_ref, wfn_ref, wrt_ref,
                  wout_ref, x_ref, gate_ref, lnw_ref, lnb_ref, sh_ref, sc_ref, xo_ref, h_ref, *, alpha):
    a = jnp.dot(yna_ref[0], wna_ref[...], preferred_element_type=F32)
    y = jax.nn.sigmoid(g0_ref[0].astype(F32)) * a
    b = jnp.dot(yfn_ref[0], wfn_ref[...], preferred_element_type=F32)
    y = y + jax.nn.sigmoid(g1_ref[0].astype(F32)) * b
    c = jnp.dot(yrt_ref[0], wrt_ref[...], preferred_element_type=F32)
    y = y + jax.nn.sigmoid(g2_ref[0].astype(F32)) * c
    y = jnp.dot(y.astype(BF16), wout_ref[...], preferred_element_type=F32)
    xn = _ln(alpha * x_ref[0] + gate_ref[0] * y) * lnw_ref[0] + lnb_ref[0]
    xo_ref[0] = xn
    h_ref[0] = _ln(xn) * (1.0 + sc_ref[0]) + sh_ref[0]


def merge_branches(y_na, y_fn, y_ret, proj, w_na, w_fn, w_rt, w_out, x, gate, ln_w_all, ln_b_all, layer,
                   shift2, scale2, alpha):
    B, N, D = x.shape
    tm = min(N, 512)
    row = lambda w: pl.BlockSpec((1, tm, w), lambda b, i: (b, i, 0))
    glb = lambda j: pl.BlockSpec((1, tm, D), lambda b, i: (b, i, COL_GL // D + j))
    full = lambda a: pl.BlockSpec(a.shape, lambda b, i: (0,) * a.ndim)
    vec = pl.BlockSpec((1, 1, D), lambda b, i: (b, 0, 0))
    prm = pl.BlockSpec((1, 1, D), lambda b, i: (layer, 0, 0))
    depth = ln_w_all.shape[0]
    return pl.pallas_call(
        functools.partial(_merge_kernel, alpha=alpha),
        out_shape=(jax.ShapeDtypeStruct((B, N, D), F32), jax.ShapeDtypeStruct((B, N, D), F32)),
        grid=(B, N // tm),
        in_specs=[row(NA_W), row(FN_W), row(RT_HEADS * RT_DV), glb(0), glb(1), glb(2),
                  full(w_na), full(w_fn), full(w_rt), full(w_out),
                  row(D), vec, prm, prm, vec, vec],
        out_specs=(row(D), row(D)),
        compiler_params=_cparams(("parallel", "parallel")),
        name="merge_branches",
    )(y_na, y_fn, y_ret, proj, proj, proj, w_na, w_fn, w_rt, w_out, x, gate,
      ln_w_all.reshape(depth, 1, D), ln_b_all.reshape(depth, 1, D), shift2, scale2)


def _router_kernel(h_ref, w_ref, b_ref, e_ref, wcol_ref, rk_ref, cnt_ref, cnt_scr):
    tm = h_ref.shape[0]

    @pl.when(pl.program_id(0) == 0)
    def _():
        cnt_scr[...] = jnp.zeros_like(cnt_scr)

    logits = lax.dot_general(w_ref[...], h_ref[...].astype(BF16), (((1,), (1,)), ((), ())),
                             preferred_element_type=F32)
    scores = jax.nn.sigmoid(logits)
    sel = scores + b_ref[...]
    row = lax.broadcasted_iota(jnp.int32, (N_EXP, tm), 0)

    def first_max(vals, rows):
        m = jnp.max(vals, axis=0, keepdims=True)
        idx = jnp.min(jnp.where(vals == m, rows, N_EXP), axis=0, keepdims=True)
        return m, idx

    gscore = []
    grow = lax.broadcasted_iota(jnp.int32, (GRP_SZ, tm), 0)
    for g in range(N_GRP):
        gs = slice(g * GRP_SZ, (g + 1) * GRP_SZ)
        vals, rows = sel[gs], grow + g * GRP_SZ
        m1, i1 = first_max(vals, rows)
        m2, _ = first_max(jnp.where(rows == i1, -jnp.inf, vals), rows)
        gscore.append(m1 + m2)
    cand = []
    for g in range(N_GRP):
        beaten = jnp.zeros((1, tm), jnp.int32)
        for o in range(N_GRP):
            if o == g:
                continue
            ahead = (gscore[o] >= gscore[g]) if o < g else (gscore[o] > gscore[g])
            beaten = beaten + ahead.astype(jnp.int32)
        gs = slice(g * GRP_SZ, (g + 1) * GRP_SZ)
        cand.append(jnp.where(beaten < TOPK_GRP, sel[gs], -jnp.inf))
    cand = jnp.concatenate(cand, axis=0)

    chosen = jnp.zeros((N_EXP, tm), F32)
    ids, wts = [], []
    for _ in range(TOPK):
        _, idx = first_max(cand, row)
        hit = row == idx
        ids.append(idx)
        wts.append(jnp.sum(jnp.where(hit, scores, 0.0), axis=0, keepdims=True))
        chosen = jnp.where(hit, 1.0, chosen)
        cand = jnp.where(hit, -jnp.inf, cand)
    wsum = wts[0]
    for w in wts[1:]:
        wsum = wsum + w

    ti = lax.broadcasted_iota(jnp.int32, (tm, tm), 0)
    tj = lax.broadcasted_iota(jnp.int32, (tm, tm), 1)
    before = (ti < tj).astype(BF16)
    rank_full = jnp.dot(chosen.astype(BF16), before, preferred_element_type=F32) + cnt_scr[...]
    cnt_scr[...] = cnt_scr[...] + jnp.sum(chosen, axis=1, keepdims=True)

    ranks = [jnp.sum(jnp.where(row == ids[k], rank_full, 0.0), axis=0, keepdims=True) for k in range(TOPK)]
    e_ref[...] = jnp.concatenate(ids, axis=0)
    rk_ref[...] = jnp.concatenate(ranks, axis=0).astype(jnp.int32)
    wn = [w / wsum * ROUTED_SCALE for w in wts]
    wpad = jnp.concatenate(wn + [jnp.zeros((LANES - TOPK, tm), F32)], axis=0)
    wcol_ref[...] = wpad.T
    cnt_ref[...] = cnt_scr[...].astype(jnp.int32)


def router(h, w_t_bf16, bias):
    T, D = h.shape
    tm = TOKEN_TILE
    lane_out = pl.BlockSpec((TOPK, tm), lambda i: (0, i))
    return pl.pallas_call(
        _router_kernel,
        out_shape=(jax.ShapeDtypeStruct((TOPK, T), jnp.int32), jax.ShapeDtypeStruct((T, LANES), F32),
                   jax.ShapeDtypeStruct((TOPK, T), jnp.int32), jax.ShapeDtypeStruct((N_EXP, 1), jnp.int32)),
        grid=(T // tm,),
        in_specs=[pl.BlockSpec((tm, D), lambda i: (i, 0)),
                  pl.BlockSpec((N_EXP, D), lambda i: (0, 0)),
                  pl.BlockSpec((N_EXP, 1), lambda i: (0, 0))],
        out_specs=(lane_out, pl.BlockSpec((tm, LANES), lambda i: (i, 0)), lane_out,
                   pl.BlockSpec((N_EXP, 1), lambda i: (0, 0))),
        scratch_shapes=[pltpu.VMEM((N_EXP, 1), F32)],
        compiler_params=_cparams(("arbitrary",)),
        name="router",
    )(h, w_t_bf16, bias.reshape(N_EXP, 1))


def _slot_kernel(e_ref, rk_ref, ps_ref, o_ref):
    tm = e_ref.shape[1]
    row = lax.broadcasted_iota(jnp.int32, (N_EXP, tm), 0)
    ps = ps_ref[...].astype(F32)
    outs = []
    for k in range(TOPK):
        base = jnp.sum(jnp.where(row == e_ref[k:k + 1, :], ps, 0.0), axis=0, keepdims=True)
        outs.append(base.astype(jnp.int32) + rk_ref[k:k + 1, :])
    o_ref[...] = jnp.concatenate(outs, axis=0)


def assign_slots(top_e, rank, pad_start):
    K, T = top_e.shape
    tm = next(t for t in (2048, 1024, 512, 256) if T % t == 0)
    blk = pl.BlockSpec((K, tm), lambda i: (0, i))
    return pl.pallas_call(
        _slot_kernel,
        out_shape=jax.ShapeDtypeStruct((K, T), jnp.int32),
        grid=(T // tm,),
        in_specs=[blk, blk, pl.BlockSpec((N_EXP, 1), lambda i: (0, 0))],
        out_specs=blk,
        compiler_params=_cparams(("parallel",)),
        name="assign_slots",
    )(top_e, rank, pad_start.reshape(N_EXP, 1).astype(jnp.int32))


def _pack_pair(lo, hi):
    a = lax.bitcast_convert_type(lo.astype(BF16).astype(F32), U32) >> 16
    b = lax.bitcast_convert_type(hi.astype(BF16).astype(F32), U32) & jnp.uint32(0xFFFF0000)
    return a | b


def _unpack_pair(w):
    lo = lax.bitcast_convert_type(w << 16, F32)
    hi = lax.bitcast_convert_type(w & jnp.uint32(0xFFFF0000), F32)
    return lo, hi


def _dispatch_kernel(pe_ref, cnt_ref, nu_ref, slot_hbm, h_ref, xs_hbm, slot_smem, stage, zbuf, sem_idx, sem):
    tm, D = h_ref.shape
    ch = stage.shape[1]
    i = pl.program_id(0)
    nb = xs_hbm.shape[0] // EXPERT_BLOCK

    @pl.when(i == 0)
    def _():
        zbuf[...] = jnp.zeros_like(zbuf)

        def block_copy(b):
            start = pl.multiple_of(b * EXPERT_BLOCK, EXPERT_BLOCK)
            return pltpu.make_async_copy(zbuf, xs_hbm.at[pl.ds(start, EXPERT_BLOCK)], sem)

        def for_partial_blocks(fn):
            def body(e, carry):
                @pl.when(cnt_ref[e] % EXPERT_BLOCK != 0)
                def _():
                    fn(block_copy(pe_ref[e] // EXPERT_BLOCK - 1))
                return carry
            lax.fori_loop(0, N_EXP, body, 0)

        def for_unused_blocks(fn):
            def body(b, carry):
                fn(block_copy(b))
                return carry
            lax.fori_loop(nu_ref[0], nb, body, 0)

        for_partial_blocks(lambda cp: cp.start())
        for_unused_blocks(lambda cp: cp.start())
        for_partial_blocks(lambda cp: cp.wait())
        for_unused_blocks(lambda cp: cp.wait())

    idx_copy = pltpu.make_async_copy(slot_hbm.at[:, pl.ds(i * tm, tm)], slot_smem, sem_idx)
    idx_copy.start()
    x = h_ref[...]
    half = D // 2
    for j in range(ch):
        stage[:, j, :] = _pack_pair(x[:, j * LANES:(j + 1) * LANES],
                                    x[:, half + j * LANES:half + (j + 1) * LANES])
    idx_copy.wait()

    def issue(t, carry):
        for k in range(TOPK):
            pltpu.make_async_copy(stage.at[t], xs_hbm.at[slot_smem[k, t]], sem).start(priority=k % 2)
        return carry

    lax.fori_loop(0, tm, issue, 0)
    for k in range(TOPK):
        pltpu.make_async_copy(stage, xs_hbm.at[pl.ds(0, tm)], sem).wait()


def dispatch(h, slots, n_slots, pad_end, counts, n_used):
    T, D = h.shape
    tm = TOKEN_TILE
    ch = D // 2 // LANES
    grid_spec = pltpu.PrefetchScalarGridSpec(
        num_scalar_prefetch=3,
        grid=(T // tm,),
        in_specs=[pl.BlockSpec(memory_space=pl.ANY),
                  pl.BlockSpec((tm, D), lambda i, pe, cnt, nu: (i, 0))],
        out_specs=pl.BlockSpec(memory_space=pl.ANY),
        scratch_shapes=[pltpu.SMEM((TOPK, tm), jnp.int32),
                        pltpu.VMEM((tm, ch, LANES), U32),
                        pltpu.VMEM((EXPERT_BLOCK, ch, LANES), U32),
                        pltpu.SemaphoreType.DMA(()),
                        pltpu.SemaphoreType.DMA(())],
    )
    return pl.pallas_call(
        _dispatch_kernel,
        out_shape=jax.ShapeDtypeStruct((n_slots, ch, LANES), U32),
        grid_spec=grid_spec,
        compiler_params=_cparams(("arbitrary",)),
        name="moe_dispatch",
    )(pad_end.astype(jnp.int32), counts.astype(jnp.int32), n_used, slots, h)


def _expert_kernel(be_ref, nu_ref, xs_ref, wg_ref, wu_ref, wd_ref, ys_ref, wg_s, wu_s, wd_s):
    i = pl.program_id(0)
    used = i < nu_ref[0]
    new_expert = jnp.logical_or(i == 0, be_ref[i] != be_ref[jnp.maximum(i - 1, 0)])

    @pl.when(jnp.logical_not(used))
    def _():
        ys_ref[...] = jnp.zeros_like(ys_ref)

    @pl.when(jnp.logical_and(used, new_expert))
    def _():
        wg_s[...] = wg_ref[0, 0].astype(BF16)
        wu_s[...] = wu_ref[0, 0].astype(BF16)
        wd_s[...] = wd_ref[0, 0].astype(BF16)

    @pl.when(used)
    def _():
        D = wg_s.shape[0]
        ch = D // 2 // LANES
        half = D // 2
        for part in range(xs_ref.shape[0] // (ch * MXU_ROWS)):
            base = part * MXU_ROWS * ch
            pairs = [_unpack_pair(xs_ref[pl.ds(base + j, MXU_ROWS, stride=ch), :]) for j in range(ch)]
            x = jnp.concatenate([lo.astype(BF16) for lo, _ in pairs] + [hi.astype(BF16) for _, hi in pairs],
                                axis=-1)
            g = jnp.dot(x, wg_s[...], preferred_element_type=F32)
            u = jnp.dot(x, wu_s[...], preferred_element_type=F32)
            a = (_silu(g) * u).astype(BF16)
            y = jnp.dot(a, wd_s[...], preferred_element_type=F32)
            for j in range(ch):
                ys_ref[pl.ds(base + j, MXU_ROWS, stride=ch), :] = _pack_pair(
                    y[:, j * LANES:(j + 1) * LANES], y[:, half + j * LANES:half + (j + 1) * LANES])


def routed_experts(xs, block_expert, n_used, w_gate_all, w_up_all, w_down_all, layer):
    n_slots, ch, _ = xs.shape
    _, E, D, Fd = w_gate_all.shape
    nb = n_slots // EXPERT_BLOCK
    rows_blk = EXPERT_BLOCK * ch
    wsel = lambda i, be, nu: (layer, be[i], 0, 0)
    grid_spec = pltpu.PrefetchScalarGridSpec(
        num_scalar_prefetch=2,
        grid=(nb,),
        in_specs=[pl.BlockSpec((rows_blk, LANES), lambda i, be, nu: (jnp.minimum(i, nu[0] - 1), 0)),
                  pl.BlockSpec((1, 1, D, Fd), wsel),
                  pl.BlockSpec((1, 1, D, Fd), wsel),
                  pl.BlockSpec((1, 1, Fd, D), wsel)],
        out_specs=pl.BlockSpec((rows_blk, LANES), lambda i, be, nu: (i, 0)),
        scratch_shapes=[pltpu.VMEM((D, Fd), BF16), pltpu.VMEM((D, Fd), BF16), pltpu.VMEM((Fd, D), BF16)],
    )
    ys = pl.pallas_call(
        _expert_kernel,
        out_shape=jax.ShapeDtypeStruct((n_slots * ch, LANES), U32),
        grid_spec=grid_spec,
        compiler_params=_cparams(("arbitrary",)),
        name="routed_experts",
    )(block_expert, n_used, xs.reshape(n_slots * ch, LANES), w_gate_all, w_up_all, w_down_all)
    return ys.reshape(n_slots, ch, LANES)


def _combine_kernel(slot_hbm, ys_hbm, wt_ref, h_ref, sg_ref, su_ref, sd_ref, x_ref, gate_ref,
                    lnw_ref, lnb_ref, o_ref, slot_smem, buf, acc, sem_idx, sem, *, alpha):
    tm = h_ref.shape[0]
    ch = buf.shape[2]
    i = pl.program_id(0)
    cur = i % 2

    last = pl.num_programs(0) - 1
    half = ch * LANES

    def load_slots(tile, b):
        idx_copy = pltpu.make_async_copy(slot_hbm.at[:, pl.ds(tile * tm, tm)], slot_smem.at[b], sem_idx)
        idx_copy.start()
        idx_copy.wait()

    def issue_rows(t, b):
        for k in range(TOPK):
            pltpu.make_async_copy(ys_hbm.at[slot_smem[b, k, t]], buf.at[b, k, :, t, :],
                                  sem.at[b]).start(priority=k % 2)

    def wait_rows(b):
        for k in range(TOPK):
            pltpu.make_async_copy(buf.at[b, k], buf.at[b, k], sem.at[b]).wait()

    @pl.when(i == 0)
    def _():
        load_slots(0, 0)

        def first(t, carry):
            issue_rows(t, 0)
            return carry

        lax.fori_loop(0, tm, first, 0)

    nxt = 1 - cur
    load_slots(jnp.minimum(i + 1, last), nxt)
    wait_rows(cur)

    def reduce_and_issue(sb, carry):
        r0 = pl.multiple_of(sb * 8, 8)
        for tt in range(8):
            issue_rows(r0 + tt, nxt)
        wt = wt_ref[pl.ds(r0, 8), :]
        for j in range(ch):
            lo, hi = _unpack_pair(buf[cur, 0, j, pl.ds(r0, 8), :])
            r_lo, r_hi = wt[:, 0:1] * lo, wt[:, 0:1] * hi
            for k in range(1, TOPK):
                lo, hi = _unpack_pair(buf[cur, k, j, pl.ds(r0, 8), :])
                r_lo, r_hi = r_lo + wt[:, k:k + 1] * lo, r_hi + wt[:, k:k + 1] * hi
            acc[pl.ds(r0, 8), j * LANES:(j + 1) * LANES] = r_lo
            acc[pl.ds(r0, 8), half + j * LANES:half + (j + 1) * LANES] = r_hi
        return carry

    lax.fori_loop(0, tm // 8, reduce_and_issue, 0)

    hb = h_ref[...].astype(BF16)
    g = jnp.dot(hb, sg_ref[0], preferred_element_type=F32)
    u = jnp.dot(hb, su_ref[0], preferred_element_type=F32)
    f = jnp.dot((_silu(g) * u).astype(BF16), sd_ref[0], preferred_element_type=F32) + acc[...]
    o_ref[...] = _ln(alpha * x_ref[...] + gate_ref[0] * f) * lnw_ref[0] + lnb_ref[0]

    @pl.when(i == last)
    def _():
        wait_rows(nxt)


def combine(slots, ys, wt, h, sg, su, sd, x, gate_tiles, ln_w_all, ln_b_all, layer, alpha):
    T, D = h.shape
    tm = TOKEN_TILE
    ch = ys.shape[1]
    depth = ln_w_all.shape[0]
    shw = lambda a: pl.BlockSpec((1,) + a.shape[1:], lambda i: (layer, 0, 0))
    prm = pl.BlockSpec((1, 1, D), lambda i: (layer, 0, 0))
    return pl.pallas_call(
        functools.partial(_combine_kernel, alpha=alpha),
        out_shape=jax.ShapeDtypeStruct((T, D), F32),
        grid=(T // tm,),
        in_specs=[pl.BlockSpec(memory_space=pl.ANY),
                  pl.BlockSpec(memory_space=pl.ANY),
                  pl.BlockSpec((tm, LANES), lambda i: (i, 0)),
                  pl.BlockSpec((tm, D), lambda i: (i, 0)),
                  shw(sg), shw(su), shw(sd),
                  pl.BlockSpec((tm, D), lambda i: (i, 0)),
                  pl.BlockSpec((1, 1, D), lambda i: (i, 0, 0)),
                  prm, prm],
        out_specs=pl.BlockSpec((tm, D), lambda i: (i, 0)),
        scratch_shapes=[pltpu.SMEM((2, TOPK, tm), jnp.int32),
                        pltpu.VMEM((2, TOPK, ch, tm, LANES), U32),
                        pltpu.VMEM((tm, D), F32),
                        pltpu.SemaphoreType.DMA(()),
                        pltpu.SemaphoreType.DMA((2,))],
        compiler_params=_cparams(("arbitrary",)),
        name="moe_combine",
    )(slots, ys, wt, h, sg, su, sd, x, gate_tiles, ln_w_all.reshape(depth, 1, D),
      ln_b_all.reshape(depth, 1, D))


def moe_sublayer(h, x, gate_tiles, router_w_t, router_bias, w_gate_all, w_up_all, w_down_all, sg, su, sd,
                 ln_w_all, ln_b_all, layer, alpha):
    T, D = h.shape
    top_e, w_col, rank, counts = router(h, router_w_t, router_bias)
    counts = counts[:, 0]
    padded = (counts + EXPERT_BLOCK - 1) // EXPERT_BLOCK * EXPERT_BLOCK
    pad_end = jnp.cumsum(padded)
    pad_start = pad_end - padded
    slots = assign_slots(top_e, rank, pad_start)
    n_slots = -(-(T * TOPK + N_EXP * (EXPERT_BLOCK - 1)) // EXPERT_BLOCK) * EXPERT_BLOCK
    nb = n_slots // EXPERT_BLOCK
    block_start = jnp.arange(nb, dtype=jnp.int32) * EXPERT_BLOCK
    block_expert = jnp.minimum(jnp.sum(block_start[:, None] >= pad_end[None, :], axis=1),
                               N_EXP - 1).astype(jnp.int32)
    n_used = (pad_end[-1:] // EXPERT_BLOCK).astype(jnp.int32)

    xs = dispatch(h, slots, n_slots, pad_end, counts, n_used)
    ys = routed_experts(xs, block_expert, n_used, w_gate_all, w_up_all, w_down_all, layer)
    return combine(slots, ys, w_col, h, sg, su, sd, x, gate_tiles, ln_w_all, ln_b_all, layer, alpha)


def kernel(x, c, ctx, c_ctx, ada_w, ada_b, w_in, na_rpb, w_o_na, w_fourier, ret_decay_fwd, ret_decay_bwd,
           ret_gn_w, w_o_ret, w_out, ln_mix_w, ln_mix_b, router_w, router_bias, exp_w_gate, exp_w_up,
           exp_w_down, sh_w_gate, sh_w_up, sh_w_down, ln_ffn_w, ln_ffn_b):
    B, N, D = x.shape
    L = ctx.shape[1]
    depth = ada_w.shape[0]
    alpha = (2.0 * depth) ** 0.25
    rows = N // GRID_COLS
    attn_rows = ATTN_ROWS if (rows % ATTN_ROWS == 0 and rows >= _window_span(ATTN_ROWS)) else 1

    rope_tabs = rope_tables(N)
    cn, sn = (t.astype(BF16) for t in dft_tables(N))
    cl, sl = (t.astype(BF16) for t in dft_tables(L))
    cc, sc = channel_dft_tables()
    zero_state = jnp.zeros((B, RT_HEADS, RT_DK, RT_DV), F32)
    c_rows = jnp.concatenate([c, c_ctx[None, :], jnp.zeros((16 - B - 1, D), F32)], axis=0)
    sg_all, su_all, sd_all = sh_w_gate.astype(BF16), sh_w_up.astype(BF16), sh_w_down.astype(BF16)

    xc = ctx
    for l in range(depth):
        update_ctx = l < depth - 1
        mod = ada_mod(c_rows, ada_w, ada_b, l)
        sh1, sc1, g1, sh2, sc2, g2 = (m[:, None, :] for m in jnp.split(mod[:B], 6, axis=-1))
        mod_c = jnp.broadcast_to(mod[B:B + 1], (B, 6 * D))
        sh1c, sc1c, g1c, sh2c, sc2c, g2c = (m[:, None, :] for m in jnp.split(mod_c, 6, axis=-1))
        lg_f = jax.nn.log_sigmoid(ret_decay_fwd[l].astype(F32))
        lg_b = jax.nn.log_sigmoid(ret_decay_bwd[l].astype(F32))

        w_in_b = w_in[l].astype(BF16)
        proj = ln_proj(x, sh1, sc1, w_in_b)
        proj_c = ln_proj(xc, sh1c, sc1c, w_in_b)

        y_na = window_attention(proj, proj_c, *window_bias_tables(na_rpb[l], rows, attn_rows))
        y_fn = fourier_mix(proj, cn, sn, cc, sc)
        ob_c, s_b = retention(proj_c, lg_b, zero_state, backward=True)
        if update_ctx:
            yrt_c, s_f = retention(proj_c, lg_f, zero_state, backward=False, finish=(ob_c, ret_gn_w, l))
        else:
            _, s_f = retention(proj_c, lg_f, zero_state, backward=False)
        ob, _ = retention(proj, lg_b, s_b, backward=True, rope_tabs=rope_tabs)
        y_ret, _ = retention(proj, lg_f, s_f, backward=False, rope_tabs=rope_tabs,
                             finish=(ob, ret_gn_w, l))

        wts = (w_o_na[l].astype(BF16), w_fourier[l].astype(BF16), w_o_ret[l].astype(BF16),
               w_out[l].astype(BF16))
        x, h = merge_branches(y_na, y_fn, y_ret, proj, *wts, x, g1, ln_mix_w, ln_mix_b, l, sh2, sc2, alpha)
        h_all = h.reshape(B * N, D)
        x_all = x.reshape(B * N, D)
        gate_tiles = jnp.repeat(g2, N // TOKEN_TILE, axis=0)
        if update_ctx:
            yna_c = context_attention(proj_c)
            yfn_c = fourier_mix(proj_c, cl, sl, cc, sc)
            xc, h_c = merge_branches(yna_c, yfn_c, yrt_c, proj_c, *wts, xc, g1c, ln_mix_w, ln_mix_b, l,
                                     sh2c, sc2c, alpha)
            h_all = jnp.concatenate([h_all, h_c.reshape(B * L, D)], axis=0)
            x_all = jnp.concatenate([x_all, xc.reshape(B * L, D)], axis=0)
            gate_tiles = jnp.concatenate([gate_tiles, jnp.repeat(g2c, L // TOKEN_TILE, axis=0)], axis=0)

        out = moe_sublayer(h_all, x_all, gate_tiles, router_w[l].T.astype(BF16), router_bias[l],
                           exp_w_gate, exp_w_up, exp_w_down, sg_all, su_all, sd_all,
                           ln_ffn_w, ln_ffn_b, l, alpha)
        x = out[:B * N].reshape(B, N, D)
        if update_ctx:
            xc = out[B * N:].reshape(B, L, D)
    return x
```

```python
import functools
import math

import numpy as np
import jax
import jax.numpy as jnp
from jax import lax
from jax.experimental import pallas as pl
from jax.experimental.pallas import tpu as pltpu

F32 = jnp.float32
BF16 = jnp.bfloat16
U32 = jnp.uint32

GRID_COLS = 64
NA_HEADS = 8
NA_DH = 64
NA_W = NA_HEADS * NA_DH
NA_ROWS = 8
NA_COLS = 16
FN_GROUPS = 4
FN_GD = 128
FN_W = FN_GROUPS * FN_GD
RT_HEADS = 4
RT_DK = 128
RT_DV = 256
RT_CHUNK = 128
ROPE_THETA = 10000.0
N_EXP = 256
TOPK = 8
N_GRP = 8
TOPK_GRP = 4
GRP_SZ = N_EXP // N_GRP
ROUTED_SCALE = 2.5
LN_EPS = 1e-6
GN_EPS = 1e-5

COL_QA, COL_KA, COL_VA, COL_U, COL_QR, COL_KR, COL_VR, COL_GR, COL_GL = (
    0, 512, 1024, 1536, 2048, 2560, 3072, 4096, 5120)

VMEM_LIMIT = 56 * 1024 * 1024
LANES = 128
SUBLANES = 8
BF16_ROWS = 16
MXU_ROWS = 256
EXPERT_BLOCK = 2 * MXU_ROWS
TOKEN_TILE = 256
ATTN_ROWS = 4
NEG = -1e30


def _cparams(sem):
    return pltpu.CompilerParams(dimension_semantics=sem, vmem_limit_bytes=VMEM_LIMIT)


def _ln(x):
    mu = jnp.mean(x, axis=-1, keepdims=True)
    xc = x - mu
    var = jnp.mean(xc * xc, axis=-1, keepdims=True)
    return xc * lax.rsqrt(var + LN_EPS)


def _silu(x):
    return x * jax.nn.sigmoid(x)


def _ada_kernel(c_ref, w_ref, b_ref, o_ref):
    a = _silu(c_ref[...]).astype(BF16)
    o_ref[...] = jnp.dot(a, w_ref[0].astype(BF16), preferred_element_type=F32) + b_ref[0]


def ada_mod(c_rows, w_all, b_all, layer):
    R, D = c_rows.shape
    W = w_all.shape[2]
    tn = 1024
    return pl.pallas_call(
        _ada_kernel,
        out_shape=jax.ShapeDtypeStruct((R, W), F32),
        grid=(W // tn,),
        in_specs=[pl.BlockSpec((R, D), lambda j: (0, 0)),
                  pl.BlockSpec((1, D, tn), lambda j: (layer, 0, j)),
                  pl.BlockSpec((1, 1, tn), lambda j: (layer, 0, j))],
        out_specs=pl.BlockSpec((R, tn), lambda j: (0, j)),
        compiler_params=_cparams(("parallel",)),
        name="ada_mod",
    )(c_rows, w_all, b_all.reshape(b_all.shape[0], 1, W))


def _ln_proj_kernel(x_ref, sh_ref, sc_ref, w_ref, o_ref, h_scr):
    @pl.when(pl.program_id(2) == 0)
    def _():
        h = _ln(x_ref[0]) * (1.0 + sc_ref[0]) + sh_ref[0]
        h_scr[...] = h.astype(BF16)

    o_ref[0] = jnp.dot(h_scr[...], w_ref[...], preferred_element_type=F32).astype(o_ref.dtype)


def ln_proj(x, shift, scale, w_bf16):
    B, N, D = x.shape
    W = w_bf16.shape[1]
    tm = min(N, 1024)
    tn = 1024
    return pl.pallas_call(
        _ln_proj_kernel,
        out_shape=jax.ShapeDtypeStruct((B, N, W), BF16),
        grid=(B, N // tm, W // tn),
        in_specs=[pl.BlockSpec((1, tm, D), lambda b, i, j: (b, i, 0)),
                  pl.BlockSpec((1, 1, D), lambda b, i, j: (b, 0, 0)),
                  pl.BlockSpec((1, 1, D), lambda b, i, j: (b, 0, 0)),
                  pl.BlockSpec((D, tn), lambda b, i, j: (0, j))],
        out_specs=pl.BlockSpec((1, tm, tn), lambda b, i, j: (b, i, j)),
        scratch_shapes=[pltpu.VMEM((tm, D), BF16)],
        compiler_params=_cparams(("parallel", "parallel", "arbitrary")),
        name="ln_proj",
    )(x, shift, scale, w_bf16)


def _softmax_pv(s_parts, v):
    m = jnp.max(s_parts[0], axis=-1, keepdims=True)
    for s in s_parts[1:]:
        m = jnp.maximum(m, jnp.max(s, axis=-1, keepdims=True))
    ps = [jnp.exp(s - m) for s in s_parts]
    l = jnp.sum(ps[0], axis=-1, keepdims=True)
    for p in ps[1:]:
        l = l + jnp.sum(p, axis=-1, keepdims=True)
    p = ps[0] if len(ps) == 1 else jnp.concatenate(ps, axis=-1)
    return jnp.dot(p.astype(BF16), v, preferred_element_type=F32) / l


def _window_span(rpb):
    return NA_ROWS + rpb - 1


def _window_attn_kernel(pat_ref, q_ref, k_ref, v_ref, kc_ref, vc_ref, bias_ref, o_ref, kcat, vcat, *,
                        rows, rpb):
    del pat_ref
    L = kc_ref.shape[1]
    span = _window_span(rpb)
    win = span * GRID_COLS
    i = pl.program_id(1)

    @pl.when(i == 0)
    def _():
        kcat[0:L, :] = kc_ref[0]
        vcat[0:L, :] = vc_ref[0]

    first = jnp.clip(i * rpb - NA_ROWS // 2, 0, rows - span)
    start = pl.multiple_of(first * GRID_COLS, GRID_COLS)
    kcat[L:L + win, :] = k_ref[0, pl.ds(start, win), :]
    vcat[L:L + win, :] = v_ref[0, pl.ds(start, win), :]
    q = q_ref[0] * (NA_DH ** -0.5)
    dn = (((1,), (1,)), ((), ()))
    outs = []
    for h in range(NA_HEADS):
        hs = slice(h * NA_DH, (h + 1) * NA_DH)
        s = lax.dot_general(q[:, hs], kcat[:, hs], dn, preferred_element_type=F32)
        outs.append(_softmax_pv([s[:, :L], s[:, L:] + bias_ref[0, h]], vcat[:, hs]))
    o_ref[0] = jnp.concatenate(outs, axis=-1).astype(o_ref.dtype)


def _ctx_attn_kernel(q_ref, kc_ref, vc_ref, o_ref):
    q = q_ref[0] * (NA_DH ** -0.5)
    kc = kc_ref[0]
    vc = vc_ref[0]
    dn = (((1,), (1,)), ((), ()))
    outs = []
    for h in range(NA_HEADS):
        hs = slice(h * NA_DH, (h + 1) * NA_DH)
        s = lax.dot_general(q[:, hs], kc[:, hs], dn, preferred_element_type=F32)
        outs.append(_softmax_pv([s], vc[:, hs]))
    o_ref[0] = jnp.concatenate(outs, axis=-1).astype(o_ref.dtype)


def window_bias_tables(rpb_table, rows, rpb):
    H = rpb_table.shape[0]
    span = _window_span(rpb)
    cidx = np.arange(GRID_COLS)
    c0 = np.clip(cidx - NA_COLS // 2, 0, GRID_COLS - NA_COLS)
    col_ok = (cidx[None, :] >= c0[:, None]) & (cidx[None, :] < c0[:, None] + NA_COLS)
    dc = np.clip(cidx[None, :] - cidx[:, None], -(NA_COLS - 1), NA_COLS - 1) + (NA_COLS - 1)
    col_sel = (dc[:, :, None] == np.arange(2 * NA_COLS - 1)).astype(np.float32)

    keys, patterns, index = [], [], []
    for blk in range(rows // rpb):
        first = int(np.clip(blk * rpb - NA_ROWS // 2, 0, rows - span))
        key_row = first + np.arange(span)
        r = blk * rpb + np.arange(rpb)
        r0 = np.clip(r - NA_ROWS // 2, 0, rows - NA_ROWS)
        assert first <= r0.min() and r0.max() + NA_ROWS <= first + span
        valid = (key_row[None, :] >= r0[:, None]) & (key_row[None, :] < r0[:, None] + NA_ROWS)
        dr = np.where(valid, key_row[None, :] - r[:, None] + NA_ROWS - 1, -1)
        if dr.tobytes() not in keys:
            keys.append(dr.tobytes())
            patterns.append(dr)
        index.append(keys.index(dr.tobytes()))
    dr = np.stack(patterns)
    row_sel = (dr[..., None] == np.arange(2 * NA_ROWS - 1)).astype(np.float32)
    t = jnp.einsum('hab,prja,qkb->phrqjk', rpb_table.astype(F32), row_sel, col_sel,
                   precision=lax.Precision.HIGHEST)
    ok = (dr >= 0)[:, None, :, None, :, None] & col_ok[None, None, None, :, None, :]
    t = jnp.where(ok, t, NEG)
    tabs = t.reshape(len(patterns), H, rpb * GRID_COLS, span * GRID_COLS)
    return tabs, jnp.asarray(np.array(index, np.int32))


def window_attention(proj, proj_ctx, bias_tabs, bias_index):
    B, N, _ = proj.shape
    L = proj_ctx.shape[1]
    rows = N // GRID_COLS
    tq = bias_tabs.shape[2]
    rpb = tq // GRID_COLS
    win = _window_span(rpb) * GRID_COLS
    cb = lambda off: off // NA_W
    grid_spec = pltpu.PrefetchScalarGridSpec(
        num_scalar_prefetch=1,
        grid=(B, rows // rpb),
        in_specs=[pl.BlockSpec((1, tq, NA_W), lambda b, i, pat: (b, i, cb(COL_QA))),
                  pl.BlockSpec((1, N, NA_W), lambda b, i, pat: (b, 0, cb(COL_KA))),
                  pl.BlockSpec((1, N, NA_W), lambda b, i, pat: (b, 0, cb(COL_VA))),
                  pl.BlockSpec((1, L, NA_W), lambda b, i, pat: (b, 0, cb(COL_KA))),
                  pl.BlockSpec((1, L, NA_W), lambda b, i, pat: (b, 0, cb(COL_VA))),
                  pl.BlockSpec((1, NA_HEADS, tq, win), lambda b, i, pat: (pat[i], 0, 0, 0))],
        out_specs=pl.BlockSpec((1, tq, NA_W), lambda b, i, pat: (b, i, 0)),
        scratch_shapes=[pltpu.VMEM((L + win, NA_W), BF16), pltpu.VMEM((L + win, NA_W), BF16)],
    )
    return pl.pallas_call(
        functools.partial(_window_attn_kernel, rows=rows, rpb=rpb),
        out_shape=jax.ShapeDtypeStruct((B, N, NA_W), BF16),
        grid_spec=grid_spec,
        compiler_params=_cparams(("parallel", "arbitrary")),
        name="window_attention",
    )(bias_index, proj, proj, proj, proj_ctx, proj_ctx, bias_tabs)


def context_attention(proj_ctx):
    B, L, _ = proj_ctx.shape
    cb = lambda off: off // NA_W
    tq = min(L, 128)
    return pl.pallas_call(
        _ctx_attn_kernel,
        out_shape=jax.ShapeDtypeStruct((B, L, NA_W), BF16),
        grid=(B, L // tq),
        in_specs=[pl.BlockSpec((1, tq, NA_W), lambda b, r: (b, r, cb(COL_QA))),
                  pl.BlockSpec((1, L, NA_W), lambda b, r: (b, 0, cb(COL_KA))),
                  pl.BlockSpec((1, L, NA_W), lambda b, r: (b, 0, cb(COL_VA)))],
        out_specs=pl.BlockSpec((1, tq, NA_W), lambda b, r: (b, r, 0)),
        compiler_params=_cparams(("parallel", "arbitrary")),
        name="context_attention",
    )(proj_ctx, proj_ctx, proj_ctx)


def _dft_kernel(cn_ref, sn_ref, u_ref, cc_ref, sc_ref, o_ref, *, norm):
    u = u_ref[0]
    a = jnp.dot(cn_ref[...], u, preferred_element_type=F32).astype(BF16)
    b = jnp.dot(sn_ref[...], u, preferred_element_type=F32).astype(BF16)
    y = (jnp.dot(a, cc_ref[...], preferred_element_type=F32)
         - jnp.dot(b, sc_ref[...], preferred_element_type=F32))
    o_ref[0] = (y * norm).astype(o_ref.dtype)


def _unit_circle(rows, cols, period):
    ang = ((rows[:, None] * cols[None, :]) % period).astype(F32) * (2.0 * math.pi / period)
    return jnp.cos(ang), jnp.sin(ang)


def dft_tables(n):
    m = jnp.arange(n, dtype=jnp.int32)
    split = GRID_COLS
    if n <= split * split // 16 or n % split:
        return _unit_circle(m, m, n)
    hi_c, hi_s = _unit_circle(jnp.arange(n // split, dtype=jnp.int32), m, n // split)
    lo_c, lo_s = _unit_circle(jnp.arange(split, dtype=jnp.int32), m, n)
    cos = hi_c[:, None, :] * lo_c[None, :, :] - hi_s[:, None, :] * lo_s[None, :, :]
    sin = hi_s[:, None, :] * lo_c[None, :, :] + hi_c[:, None, :] * lo_s[None, :, :]
    return cos.reshape(n, n), sin.reshape(n, n)


def channel_dft_tables():
    c, s = dft_tables(FN_GD)
    eye = jnp.eye(FN_GROUPS, dtype=F32)
    return jnp.kron(eye, c).astype(BF16), jnp.kron(eye, s).astype(BF16)


def fourier_mix(proj, cn, sn, cc, sc):
    B, N, _ = proj.shape
    tk = min(N, 512)
    norm = 1.0 / math.sqrt(N * FN_GD)
    return pl.pallas_call(
        functools.partial(_dft_kernel, norm=norm),
        out_shape=jax.ShapeDtypeStruct((B, N, FN_W), BF16),
        grid=(N // tk, B),
        in_specs=[pl.BlockSpec((tk, N), lambda i, b: (i, 0)),
                  pl.BlockSpec((tk, N), lambda i, b: (i, 0)),
                  pl.BlockSpec((1, N, FN_W), lambda i, b: (b, 0, COL_U // FN_W)),
                  pl.BlockSpec((FN_W, FN_W), lambda i, b: (0, 0)),
                  pl.BlockSpec((FN_W, FN_W), lambda i, b: (0, 0))],
        out_specs=pl.BlockSpec((1, tk, FN_W), lambda i, b: (b, i, 0)),
        compiler_params=_cparams(("parallel", "arbitrary")),
        name="fourier_mix",
    )(cn, sn, proj, cc, sc)


def _retention_kernel(*refs, backward, rope, finish, cpb):
    it = iter(refs)
    lg_ref = next(it)
    q_ref, k_ref, v_ref = next(it), next(it), next(it)
    if rope:
        cos_ref, sin_ref, perm_ref = next(it), next(it), next(it)
    s0_ref = next(it)
    if finish:
        ob_ref, gate_ref, gn_ref = next(it), next(it), next(it)
    o_ref, sf_ref = next(it), next(it)
    state = next(it)

    C = RT_CHUNK
    step = pl.program_id(1)

    @pl.when(step == 0)
    def _():
        state[...] = s0_ref[0]

    pos_r = lax.broadcasted_iota(jnp.int32, (C, 1), 0).astype(F32)
    ci = lax.broadcasted_iota(jnp.int32, (C, C), 0)
    mi = lax.broadcasted_iota(jnp.int32, (C, C), 1)
    if backward:
        dist = jnp.maximum(mi - ci, 0).astype(F32)
        band = mi > ci
    else:
        dist = jnp.maximum(ci - mi, 0).astype(F32)
        band = ci >= mi

    q = q_ref[0]
    k = k_ref[0]
    if rope:
        cos = jnp.concatenate([cos_ref[...]] * RT_HEADS, axis=-1)
        sin = jnp.concatenate([sin_ref[...]] * RT_HEADS, axis=-1)
        perm = perm_ref[...]
        qf = q.astype(F32) * cos + jnp.dot(q, perm, preferred_element_type=F32) * sin
        kf = k.astype(F32) * cos + jnp.dot(k, perm, preferred_element_type=F32) * sin
    else:
        qf = q.astype(F32)
        kf = k.astype(F32)
    qf = qf * (RT_DK ** -0.5)

    order = range(cpb - 1, -1, -1) if backward else range(cpb)
    for h in range(RT_HEADS):
        lg = lg_ref[h]
        if backward:
            k_dec = jnp.exp(lg * pos_r)
            q_dec = jnp.exp(lg * (C - pos_r))
        else:
            k_dec = jnp.exp(lg * (C - 1 - pos_r))
            q_dec = jnp.exp(lg * (pos_r + 1))
        decay = jnp.where(band, jnp.exp(lg * dist), 0.0)
        chunk_dec = jnp.exp(lg * C)
        kcol = slice(h * RT_DK, (h + 1) * RT_DK)
        vcol = slice(h * RT_DV, (h + 1) * RT_DV)
        s_prev = state[h]
        for c in order:
            sl = slice(c * C, (c + 1) * C)
            qc = qf[sl, kcol]
            kc = kf[sl, kcol]
            vc = v_ref[0, sl, vcol]
            scores = lax.dot_general(qc.astype(BF16), kc.astype(BF16), (((1,), (1,)), ((), ())),
                                     preferred_element_type=F32) * decay
            lhs = jnp.concatenate([scores, qc * q_dec], axis=-1).astype(BF16)
            rhs = jnp.concatenate([vc, s_prev.astype(BF16)], axis=0)
            o = jnp.dot(lhs, rhs, preferred_element_type=F32)
            kv = lax.dot_general((kc * k_dec).astype(BF16), vc, (((0,), (0,)), ((), ())),
                                 preferred_element_type=F32)
            s_prev = chunk_dec * s_prev + kv
            if finish:
                o = o + ob_ref[0, sl, vcol]
                mu = jnp.mean(o, axis=-1, keepdims=True)
                oc = o - mu
                var = jnp.mean(oc * oc, axis=-1, keepdims=True)
                y = oc * lax.rsqrt(var + GN_EPS) * gn_ref[0, :, vcol]
                o = y * _silu(gate_ref[0, sl, vcol].astype(F32))
            o_ref[0, sl, vcol] = o.astype(o_ref.dtype)
        state[h] = s_prev

    @pl.when(step == pl.num_programs(1) - 1)
    def _():
        sf_ref[0] = state[...]


def rope_tables(n):
    t = jnp.arange(n)
    row, col = t // GRID_COLS, t % GRID_COLS
    half = RT_DK // 2
    n_pairs = half // 2
    inv_freq = ROPE_THETA ** (-jnp.arange(n_pairs, dtype=F32) / n_pairs)

    def cs(pos):
        ang = pos.astype(F32)[:, None] * inv_freq[None, :]
        c, s = jnp.cos(ang), jnp.sin(ang)
        return jnp.concatenate([c, c], axis=-1), jnp.concatenate([-s, s], axis=-1)

    cr, sr = cs(row)
    cc, sc = cs(col)
    cos = jnp.concatenate([cr, cc], axis=-1)
    sin = jnp.concatenate([sr, sc], axis=-1)
    idx = jnp.arange(RT_DK)
    src = (idx // half) * half + (idx % half + n_pairs) % half
    perm = (idx[:, None] == src[None, :]).astype(F32)
    return cos, sin, jnp.kron(jnp.eye(RT_HEADS, dtype=F32), perm).astype(BF16)


def retention(proj, log_g, s0, *, backward, rope_tabs=None, finish=None):
    B, N, _ = proj.shape
    nc = N // RT_CHUNK
    cpb = min(nc, 4)
    tb = cpb * RT_CHUNK
    nsteps = nc // cpb
    blk = (lambda s: nsteps - 1 - s) if backward else (lambda s: s)
    rope = rope_tabs is not None
    fin = finish is not None

    qk_w = RT_HEADS * RT_DK
    v_w = RT_HEADS * RT_DV
    state_spec = pl.BlockSpec((1, RT_HEADS, RT_DK, RT_DV), lambda b, s: (b, 0, 0, 0))
    in_specs = [pl.BlockSpec(memory_space=pltpu.SMEM),
                pl.BlockSpec((1, tb, qk_w), lambda b, s: (b, blk(s), COL_QR // qk_w)),
                pl.BlockSpec((1, tb, qk_w), lambda b, s: (b, blk(s), COL_KR // qk_w)),
                pl.BlockSpec((1, tb, v_w), lambda b, s: (b, blk(s), COL_VR // v_w))]
    args = [log_g, proj, proj, proj]
    if rope:
        in_specs += [pl.BlockSpec((tb, RT_DK), lambda b, s: (blk(s), 0)),
                     pl.BlockSpec((tb, RT_DK), lambda b, s: (blk(s), 0)),
                     pl.BlockSpec((qk_w, qk_w), lambda b, s: (0, 0))]
        args += list(rope_tabs)
    in_specs.append(state_spec)
    args.append(s0)
    if fin:
        o_b, gn_all, layer = finish
        in_specs += [pl.BlockSpec((1, tb, v_w), lambda b, s: (b, blk(s), 0)),
                     pl.BlockSpec((1, tb, v_w), lambda b, s: (b, blk(s), COL_GR // v_w)),
                     pl.BlockSpec((1, 1, v_w), lambda b, s: (layer, 0, 0))]
        args += [o_b, proj, gn_all.reshape(gn_all.shape[0], 1, -1)]
    out_dtype = BF16 if fin else F32
    o, s_fin = pl.pallas_call(
        functools.partial(_retention_kernel, backward=backward, rope=rope, finish=fin, cpb=cpb),
        out_shape=(jax.ShapeDtypeStruct((B, N, v_w), out_dtype),
                   jax.ShapeDtypeStruct((B, RT_HEADS, RT_DK, RT_DV), F32)),
        grid=(B, nsteps),
        in_specs=in_specs,
        out_specs=(pl.BlockSpec((1, tb, v_w), lambda b, s: (b, blk(s), 0)), state_spec),
        scratch_shapes=[pltpu.VMEM((RT_HEADS, RT_DK, RT_DV), F32)],
        compiler_params=_cparams(("parallel", "arbitrary")),
        name="retention_bwd" if backward else "retention_fwd",
    )(*args)
    return o, s_fin


def _merge_kernel(yna_ref, yfn_ref, yrt_ref, g0_ref, g1_ref, g2_ref, wna_ref, wfn_ref, wrt_ref,
                  wout_ref, x_ref, gate_ref, lnw_ref, lnb_ref, sh_ref, sc_ref, xo_ref, h_ref, *, alpha):
    a = jnp.dot(yna_ref[0], wna_ref[...], preferred_element_type=F32)
    y = jax.nn.sigmoid(g0_ref[0].astype(F32)) * a
    b = jnp.dot(yfn_ref[0], wfn_ref[...], preferred_element_type=F32)
    y = y + jax.nn.sigmoid(g1_ref[0].astype(F32)) * b
    c = jnp.dot(yrt_ref[0], wrt_ref[...], preferred_element_type=F32)
    y = y + jax.nn.sigmoid(g2_ref[0].astype(F32)) * c
    y = jnp.dot(y.astype(BF16), wout_ref[...], preferred_element_type=F32)
    xn = _ln(alpha * x_ref[0] + gate_ref[0] * y) * lnw_ref[0] + lnb_ref[0]
    xo_ref[0] = xn
    h_ref[0] = _ln(xn) * (1.0 + sc_ref[0]) + sh_ref[0]


def merge_branches(y_na, y_fn, y_ret, proj, w_na, w_fn, w_rt, w_out, x, gate, ln_w_all, ln_b_all, layer,
                   shift2, scale2, alpha):
    B, N, D = x.shape
    tm = min(N, 512)
    row = lambda w: pl.BlockSpec((1, tm, w), lambda b, i: (b, i, 0))
    glb = lambda j: pl.BlockSpec((1, tm, D), lambda b, i: (b, i, COL_GL // D + j))
    full = lambda a: pl.BlockSpec(a.shape, lambda b, i: (0,) * a.ndim)
    vec = pl.BlockSpec((1, 1, D), lambda b, i: (b, 0, 0))
    prm = pl.BlockSpec((1, 1, D), lambda b, i: (layer, 0, 0))
    depth = ln_w_all.shape[0]
    return pl.pallas_call(
        functools.partial(_merge_kernel, alpha=alpha),
        out_shape=(jax.ShapeDtypeStruct((B, N, D), F32), jax.ShapeDtypeStruct((B, N, D), F32)),
        grid=(B, N // tm),
        in_specs=[row(NA_W), row(FN_W), row(RT_HEADS * RT_DV), glb(0), glb(1), glb(2),
                  full(w_na), full(w_fn), full(w_rt), full(w_out),
                  row(D), vec, prm, prm, vec, vec],
        out_specs=(row(D), row(D)),
        compiler_params=_cparams(("parallel", "parallel")),
        name="merge_branches",
    )(y_na, y_fn, y_ret, proj, proj, proj, w_na, w_fn, w_rt, w_out, x, gate,
      ln_w_all.reshape(depth, 1, D), ln_b_all.reshape(depth, 1, D), shift2, scale2)


def _router_kernel(h_ref, w_ref, b_ref, e_ref, wcol_ref, rk_ref, cnt_ref, cnt_scr):
    tm = h_ref.shape[0]

    @pl.when(pl.program_id(0) == 0)
    def _():
        cnt_scr[...] = jnp.zeros_like(cnt_scr)

    logits = lax.dot_general(w_ref[...], h_ref[...].astype(BF16), (((1,), (1,)), ((), ())),
                             preferred_element_type=F32)
    scores = jax.nn.sigmoid(logits)
    sel = scores + b_ref[...]
    row = lax.broadcasted_iota(jnp.int32, (N_EXP, tm), 0)

    def first_max(vals, rows):
        m = jnp.max(vals, axis=0, keepdims=True)
        idx = jnp.min(jnp.where(vals == m, rows, N_EXP), axis=0, keepdims=True)
        return m, idx

    gscore = []
    grow = lax.broadcasted_iota(jnp.int32, (GRP_SZ, tm), 0)
    for g in range(N_GRP):
        gs = slice(g * GRP_SZ, (g + 1) * GRP_SZ)
        vals, rows = sel[gs], grow + g * GRP_SZ
        m1, i1 = first_max(vals, rows)
        m2, _ = first_max(jnp.where(rows == i1, -jnp.inf, vals), rows)
        gscore.append(m1 + m2)
    cand = []
    for g in range(N_GRP):
        beaten = jnp.zeros((1, tm), jnp.int32)
        for o in range(N_GRP):
            if o == g:
                continue
            ahead = (gscore[o] >= gscore[g]) if o < g else (gscore[o] > gscore[g])
            beaten = beaten + ahead.astype(jnp.int32)
        gs = slice(g * GRP_SZ, (g + 1) * GRP_SZ)
        cand.append(jnp.where(beaten < TOPK_GRP, sel[gs], -jnp.inf))
    cand = jnp.concatenate(cand, axis=0)

    chosen = jnp.zeros((N_EXP, tm), F32)
    ids, wts = [], []
    for _ in range(TOPK):
        _, idx = first_max(cand, row)
        hit = row == idx
        ids.append(idx)
        wts.append(jnp.sum(jnp.where(hit, scores, 0.0), axis=0, keepdims=True))
        chosen = jnp.where(hit, 1.0, chosen)
        cand = jnp.where(hit, -jnp.inf, cand)
    wsum = wts[0]
    for w in wts[1:]:
        wsum = wsum + w

    ti = lax.broadcasted_iota(jnp.int32, (tm, tm), 0)
    tj = lax.broadcasted_iota(jnp.int32, (tm, tm), 1)
    before = (ti < tj).astype(BF16)
    rank_full = jnp.dot(chosen.astype(BF16), before, preferred_element_type=F32) + cnt_scr[...]
    cnt_scr[...] = cnt_scr[...] + jnp.sum(chosen, axis=1, keepdims=True)

    ranks = [jnp.sum(jnp.where(row == ids[k], rank_full, 0.0), axis=0, keepdims=True) for k in range(TOPK)]
    e_ref[...] = jnp.concatenate(ids, axis=0)
    rk_ref[...] = jnp.concatenate(ranks, axis=0).astype(jnp.int32)
    wn = [w / wsum * ROUTED_SCALE for w in wts]
    wpad = jnp.concatenate(wn + [jnp.zeros((LANES - TOPK, tm), F32)], axis=0)
    wcol_ref[...] = wpad.T
    cnt_ref[...] = cnt_scr[...].astype(jnp.int32)


def router(h, w_t_bf16, bias):
    T, D = h.shape
    tm = TOKEN_TILE
    lane_out = pl.BlockSpec((TOPK, tm), lambda i: (0, i))
    return pl.pallas_call(
        _router_kernel,
        out_shape=(jax.ShapeDtypeStruct((TOPK, T), jnp.int32), jax.ShapeDtypeStruct((T, LANES), F32),
                   jax.ShapeDtypeStruct((TOPK, T), jnp.int32), jax.ShapeDtypeStruct((N_EXP, 1), jnp.int32)),
        grid=(T // tm,),
        in_specs=[pl.BlockSpec((tm, D), lambda i: (i, 0)),
                  pl.BlockSpec((N_EXP, D), lambda i: (0, 0)),
                  pl.BlockSpec((N_EXP, 1), lambda i: (0, 0))],
        out_specs=(lane_out, pl.BlockSpec((tm, LANES), lambda i: (i, 0)), lane_out,
                   pl.BlockSpec((N_EXP, 1), lambda i: (0, 0))),
        scratch_shapes=[pltpu.VMEM((N_EXP, 1), F32)],
        compiler_params=_cparams(("arbitrary",)),
        name="router",
    )(h, w_t_bf16, bias.reshape(N_EXP, 1))


def _slot_kernel(e_ref, rk_ref, ps_ref, o_ref):
    tm = e_ref.shape[1]
    row = lax.broadcasted_iota(jnp.int32, (N_EXP, tm), 0)
    ps = ps_ref[...].astype(F32)
    outs = []
    for k in range(TOPK):
        base = jnp.sum(jnp.where(row == e_ref[k:k + 1, :], ps, 0.0), axis=0, keepdims=True)
        outs.append(base.astype(jnp.int32) + rk_ref[k:k + 1, :])
    o_ref[...] = jnp.concatenate(outs, axis=0)


def assign_slots(top_e, rank, pad_start):
    K, T = top_e.shape
    tm = next(t for t in (2048, 1024, 512, 256) if T % t == 0)
    blk = pl.BlockSpec((K, tm), lambda i: (0, i))
    return pl.pallas_call(
        _slot_kernel,
        out_shape=jax.ShapeDtypeStruct((K, T), jnp.int32),
        grid=(T // tm,),
        in_specs=[blk, blk, pl.BlockSpec((N_EXP, 1), lambda i: (0, 0))],
        out_specs=blk,
        compiler_params=_cparams(("parallel",)),
        name="assign_slots",
    )(top_e, rank, pad_start.reshape(N_EXP, 1).astype(jnp.int32))


def _pack_pair(lo, hi):
    a = lax.bitcast_convert_type(lo.astype(BF16).astype(F32), U32) >> 16
    b = lax.bitcast_convert_type(hi.astype(BF16).astype(F32), U32) & jnp.uint32(0xFFFF0000)
    return a | b


def _unpack_pair(w):
    lo = lax.bitcast_convert_type(w << 16, F32)
    hi = lax.bitcast_convert_type(w & jnp.uint32(0xFFFF0000), F32)
    return lo, hi


def _dispatch_kernel(pe_ref, cnt_ref, nu_ref, slot_hbm, h_ref, sg_ref, su_ref, sd_ref, xs_hbm, sh_ref,
                     slot_smem, stage, zbuf, sem_idx, sem):
    tm, D = h_ref.shape
    ch = stage.shape[1]
    i = pl.program_id(0)
    nb = xs_hbm.shape[0] // EXPERT_BLOCK

    @pl.when(i == 0)
    def _():
        zbuf[...] = jnp.zeros_like(zbuf)

        def block_copy(b):
            start = pl.multiple_of(b * EXPERT_BLOCK, EXPERT_BLOCK)
            return pltpu.make_async_copy(zbuf, xs_hbm.at[pl.ds(start, EXPERT_BLOCK)], sem)

        def for_partial_blocks(fn):
            def body(e, carry):
                @pl.when(cnt_ref[e] % EXPERT_BLOCK != 0)
                def _():
                    fn(block_copy(pe_ref[e] // EXPERT_BLOCK - 1))
                return carry
            lax.fori_loop(0, N_EXP, body, 0)

        def for_unused_blocks(fn):
            def body(b, carry):
                fn(block_copy(b))
                return carry
            lax.fori_loop(nu_ref[0], nb, body, 0)

        for_partial_blocks(lambda cp: cp.start())
        for_unused_blocks(lambda cp: cp.start())
        for_partial_blocks(lambda cp: cp.wait())
        for_unused_blocks(lambda cp: cp.wait())

    idx_copy = pltpu.make_async_copy(slot_hbm.at[:, pl.ds(i * tm, tm)], slot_smem, sem_idx)
    idx_copy.start()
    x = h_ref[...]
    half = D // 2
    for j in range(ch):
        stage[:, j, :] = _pack_pair(x[:, j * LANES:(j + 1) * LANES],
                                    x[:, half + j * LANES:half + (j + 1) * LANES])
    idx_copy.wait()

    def issue(t, carry):
        for k in range(TOPK):
            pltpu.make_async_copy(stage.at[t], xs_hbm.at[slot_smem[k, t]], sem).start(priority=k % 2)
        return carry

    lax.fori_loop(0, tm, issue, 0)

    hb = x.astype(BF16)
    g = jnp.dot(hb, sg_ref[0], preferred_element_type=F32)
    u = jnp.dot(hb, su_ref[0], preferred_element_type=F32)
    sh_ref[...] = jnp.dot((_silu(g) * u).astype(BF16), sd_ref[0], preferred_element_type=F32)

    for k in range(TOPK):
        pltpu.make_async_copy(stage, xs_hbm.at[pl.ds(0, tm)], sem).wait()


def dispatch(h, slots, n_slots, pad_end, counts, n_used, sg, su, sd, layer):
    T, D = h.shape
    tm = TOKEN_TILE
    ch = D // 2 // LANES
    shw = lambda a: pl.BlockSpec((1,) + a.shape[1:], lambda i, pe, cnt, nu: (layer, 0, 0))
    grid_spec = pltpu.PrefetchScalarGridSpec(
        num_scalar_prefetch=3,
        grid=(T // tm,),
        in_specs=[pl.BlockSpec(memory_space=pl.ANY),
                  pl.BlockSpec((tm, D), lambda i, pe, cnt, nu: (i, 0)),
                  shw(sg), shw(su), shw(sd)],
        out_specs=(pl.BlockSpec(memory_space=pl.ANY),
                   pl.BlockSpec((tm, D), lambda i, pe, cnt, nu: (i, 0))),
        scratch_shapes=[pltpu.SMEM((TOPK, tm), jnp.int32),
                        pltpu.VMEM((tm, ch, LANES), U32),
                        pltpu.VMEM((EXPERT_BLOCK, ch, LANES), U32),
                        pltpu.SemaphoreType.DMA(()),
                        pltpu.SemaphoreType.DMA(())],
    )
    return pl.pallas_call(
        _dispatch_kernel,
        out_shape=(jax.ShapeDtypeStruct((n_slots, ch, LANES), U32), jax.ShapeDtypeStruct((T, D), F32)),
        grid_spec=grid_spec,
        compiler_params=_cparams(("arbitrary",)),
        name="moe_dispatch",
    )(pad_end.astype(jnp.int32), counts.astype(jnp.int32), n_used, slots, h, sg, su, sd)


def _expert_kernel(be_ref, nu_ref, par_ref, nxt_ref, xs_ref, wg_hbm, wu_hbm, wd_hbm, ys_ref,
                   wg_f, wu_f, wd_f, wg_s, wu_s, wd_s, sem, *, layer):
    i = pl.program_id(0)
    used = i < nu_ref[0]
    new_expert = jnp.logical_or(i == 0, be_ref[i] != be_ref[jnp.maximum(i - 1, 0)])

    def weight_copies(e, b):
        return (pltpu.make_async_copy(wg_hbm.at[layer, e], wg_f.at[b], sem.at[b]),
                pltpu.make_async_copy(wu_hbm.at[layer, e], wu_f.at[b], sem.at[b]),
                pltpu.make_async_copy(wd_hbm.at[layer, e], wd_f.at[b], sem.at[b]))

    @pl.when(i == 0)
    def _():
        for cp in weight_copies(be_ref[0], 0):
            cp.start()

    @pl.when(jnp.logical_not(used))
    def _():
        ys_ref[...] = jnp.zeros_like(ys_ref)

    @pl.when(jnp.logical_and(used, new_expert))
    def _():
        b = par_ref[i]
        for cp in weight_copies(be_ref[i], b):
            cp.wait()

        @pl.when(nxt_ref[i] >= 0)
        def _():
            for cp in weight_copies(nxt_ref[i], 1 - b):
                cp.start()

        wg_s[...] = wg_f[b].astype(BF16)
        wu_s[...] = wu_f[b].astype(BF16)
        wd_s[...] = wd_f[b].astype(BF16)

    @pl.when(used)
    def _():
        D = wg_s.shape[0]
        ch = D // 2 // LANES
        half = D // 2
        for part in range(xs_ref.shape[0] // (ch * MXU_ROWS)):
            base = part * MXU_ROWS * ch
            pairs = [_unpack_pair(xs_ref[pl.ds(base + j, MXU_ROWS, stride=ch), :]) for j in range(ch)]
            x = jnp.concatenate([lo.astype(BF16) for lo, _ in pairs] + [hi.astype(BF16) for _, hi in pairs],
                                axis=-1)
            g = jnp.dot(x, wg_s[...], preferred_element_type=F32)
            u = jnp.dot(x, wu_s[...], preferred_element_type=F32)
            a = (_silu(g) * u).astype(BF16)
            y = jnp.dot(a, wd_s[...], preferred_element_type=F32)
            for j in range(ch):
                ys_ref[pl.ds(base + j, MXU_ROWS, stride=ch), :] = _pack_pair(
                    y[:, j * LANES:(j + 1) * LANES], y[:, half + j * LANES:half + (j + 1) * LANES])


def routed_experts(xs, block_expert, n_used, counts, w_gate_all, w_up_all, w_down_all, layer):
    n_slots, ch, _ = xs.shape
    _, E, D, Fd = w_gate_all.shape
    nb = n_slots // EXPERT_BLOCK
    rows_blk = EXPERT_BLOCK * ch
    starts = jnp.concatenate([jnp.ones((1,), jnp.int32),
                              (block_expert[1:] != block_expert[:-1]).astype(jnp.int32)])
    parity = ((jnp.cumsum(starts) - 1) % 2).astype(jnp.int32)
    ids = jnp.arange(E, dtype=jnp.int32)
    later = jnp.where((counts[None, :] > 0) & (ids[None, :] > ids[:, None]), ids[None, :], E)
    next_used = jnp.min(later, axis=1)
    next_used = jnp.where(next_used == E, -1, next_used).astype(jnp.int32)
    next_expert = jnp.sum(jnp.where(block_expert[:, None] == ids[None, :], next_used[None, :], 0),
                          axis=1).astype(jnp.int32)
    idx = lambda f: (lambda i, be, nu, par, nxt: f(i, nu))
    grid_spec = pltpu.PrefetchScalarGridSpec(
        num_scalar_prefetch=4,
        grid=(nb,),
        in_specs=[pl.BlockSpec((rows_blk, LANES), idx(lambda i, nu: (jnp.minimum(i, nu[0] - 1), 0))),
                  pl.BlockSpec(memory_space=pl.ANY),
                  pl.BlockSpec(memory_space=pl.ANY),
                  pl.BlockSpec(memory_space=pl.ANY)],
        out_specs=pl.BlockSpec((rows_blk, LANES), idx(lambda i, nu: (i, 0))),
        scratch_shapes=[pltpu.VMEM((2, D, Fd), F32), pltpu.VMEM((2, D, Fd), F32), pltpu.VMEM((2, Fd, D), F32),
                        pltpu.VMEM((D, Fd), BF16), pltpu.VMEM((D, Fd), BF16), pltpu.VMEM((Fd, D), BF16),
                        pltpu.SemaphoreType.DMA((2,))],
    )
    ys = pl.pallas_call(
        functools.partial(_expert_kernel, layer=layer),
        out_shape=jax.ShapeDtypeStruct((n_slots * ch, LANES), U32),
        grid_spec=grid_spec,
        compiler_params=_cparams(("arbitrary",)),
        name="routed_experts",
    )(block_expert, n_used, parity, next_expert, xs.reshape(n_slots * ch, LANES),
      w_gate_all, w_up_all, w_down_all)
    return ys.reshape(n_slots, ch, LANES)


def _combine_kernel(slot_hbm, ys_hbm, wt_ref, sh_ref, x_ref, gate_ref, lnw_ref, lnb_ref, o_ref,
                    slot_smem, buf, acc, sem_idx, sem, *, alpha):
    tm = x_ref.shape[0]
    ch = buf.shape[2]
    i = pl.program_id(0)
    cur = i % 2

    last = pl.num_programs(0) - 1
    half = ch * LANES

    def load_slots(tile, b):
        idx_copy = pltpu.make_async_copy(slot_hbm.at[:, pl.ds(tile * tm, tm)], slot_smem.at[b], sem_idx)
        idx_copy.start()
        idx_copy.wait()

    def issue_rows(t, b):
        for k in range(TOPK):
            pltpu.make_async_copy(ys_hbm.at[slot_smem[b, k, t]], buf.at[b, k, :, t, :],
                                  sem.at[b]).start(priority=k % 2)

    def wait_rows(b):
        for k in range(TOPK):
            pltpu.make_async_copy(buf.at[b, k], buf.at[b, k], sem.at[b]).wait()

    @pl.when(i == 0)
    def _():
        load_slots(0, 0)

        def first(t, carry):
            issue_rows(t, 0)
            return carry

        lax.fori_loop(0, tm, first, 0)

    nxt = 1 - cur
    load_slots(jnp.minimum(i + 1, last), nxt)
    wait_rows(cur)

    def reduce_and_issue(sb, carry):
        r0 = pl.multiple_of(sb * SUBLANES, SUBLANES)
        rows = pl.ds(r0, SUBLANES)
        for tt in range(SUBLANES):
            issue_rows(r0 + tt, nxt)
        wt = wt_ref[rows, :]
        for j in range(ch):
            lo, hi = _unpack_pair(buf[cur, 0, j, rows, :])
            r_lo, r_hi = wt[:, 0:1] * lo, wt[:, 0:1] * hi
            for k in range(1, TOPK):
                lo, hi = _unpack_pair(buf[cur, k, j, rows, :])
                r_lo, r_hi = r_lo + wt[:, k:k + 1] * lo, r_hi + wt[:, k:k + 1] * hi
            acc[rows, j * LANES:(j + 1) * LANES] = r_lo
            acc[rows, half + j * LANES:half + (j + 1) * LANES] = r_hi
        return carry

    lax.fori_loop(0, tm // SUBLANES, reduce_and_issue, 0)

    f = sh_ref[...] + acc[...]
    o_ref[...] = _ln(alpha * x_ref[...] + gate_ref[0] * f) * lnw_ref[0] + lnb_ref[0]

    @pl.when(i == last)
    def _():
        wait_rows(nxt)


def combine(slots, ys, wt, shared, x, gate_tiles, ln_w_all, ln_b_all, layer, alpha):
    T, D = x.shape
    tm = TOKEN_TILE
    ch = ys.shape[1]
    depth = ln_w_all.shape[0]
    prm = pl.BlockSpec((1, 1, D), lambda i: (layer, 0, 0))
    return pl.pallas_call(
        functools.partial(_combine_kernel, alpha=alpha),
        out_shape=jax.ShapeDtypeStruct((T, D), F32),
        grid=(T // tm,),
        in_specs=[pl.BlockSpec(memory_space=pl.ANY),
                  pl.BlockSpec(memory_space=pl.ANY),
                  pl.BlockSpec((tm, LANES), lambda i: (i, 0)),
                  pl.BlockSpec((tm, D), lambda i: (i, 0)),
                  pl.BlockSpec((tm, D), lambda i: (i, 0)),
                  pl.BlockSpec((1, 1, D), lambda i: (i, 0, 0)),
                  prm, prm],
        out_specs=pl.BlockSpec((tm, D), lambda i: (i, 0)),
        scratch_shapes=[pltpu.SMEM((2, TOPK, tm), jnp.int32),
                        pltpu.VMEM((2, TOPK, ch, tm, LANES), U32),
                        pltpu.VMEM((tm, D), F32),
                        pltpu.SemaphoreType.DMA(()),
                        pltpu.SemaphoreType.DMA((2,))],
        compiler_params=_cparams(("arbitrary",)),
        name="moe_combine",
    )(slots, ys, wt, shared, x, gate_tiles, ln_w_all.reshape(depth, 1, D), ln_b_all.reshape(depth, 1, D))


def moe_sublayer(h, x, gate_tiles, router_w_t, router_bias, w_gate_all, w_up_all, w_down_all, sg, su, sd,
                 ln_w_all, ln_b_all, layer, alpha):
    T, D = h.shape
    top_e, w_col, rank, counts = router(h, router_w_t, router_bias)
    counts = counts[:, 0]
    padded = (counts + EXPERT_BLOCK - 1) // EXPERT_BLOCK * EXPERT_BLOCK
    pad_end = jnp.cumsum(padded)
    pad_start = pad_end - padded
    slots = assign_slots(top_e, rank, pad_start)
    n_slots = -(-(T * TOPK + N_EXP * (EXPERT_BLOCK - 1)) // EXPERT_BLOCK) * EXPERT_BLOCK
    nb = n_slots // EXPERT_BLOCK
    block_start = jnp.arange(nb, dtype=jnp.int32) * EXPERT_BLOCK
    block_expert = jnp.minimum(jnp.sum(block_start[:, None] >= pad_end[None, :], axis=1),
                               N_EXP - 1).astype(jnp.int32)
    n_used = (pad_end[-1:] // EXPERT_BLOCK).astype(jnp.int32)

    xs, shared = dispatch(h, slots, n_slots, pad_end, counts, n_used, sg, su, sd, layer)
    ys = routed_experts(xs, block_expert, n_used, counts, w_gate_all, w_up_all, w_down_all, layer)
    return combine(slots, ys, w_col, shared, x, gate_tiles, ln_w_all, ln_b_all, layer, alpha)


def kernel(x, c, ctx, c_ctx, ada_w, ada_b, w_in, na_rpb, w_o_na, w_fourier, ret_decay_fwd, ret_decay_bwd,
           ret_gn_w, w_o_ret, w_out, ln_mix_w, ln_mix_b, router_w, router_bias, exp_w_gate, exp_w_up,
           exp_w_down, sh_w_gate, sh_w_up, sh_w_down, ln_ffn_w, ln_ffn_b):
    B, N, D = x.shape
    L = ctx.shape[1]
    depth = ada_w.shape[0]
    alpha = (2.0 * depth) ** 0.25
    rows = N // GRID_COLS
    attn_rows = ATTN_ROWS if (rows % ATTN_ROWS == 0 and rows >= _window_span(ATTN_ROWS)) else 1

    rope_tabs = rope_tables(N)
    cn, sn = (t.astype(BF16) for t in dft_tables(N))
    cl, sl = (t.astype(BF16) for t in dft_tables(L))
    cc, sc = channel_dft_tables()
    zero_state = jnp.zeros((B, RT_HEADS, RT_DK, RT_DV), F32)
    mod_rows = -(-(B + 1) // BF16_ROWS) * BF16_ROWS
    c_rows = jnp.concatenate([c, c_ctx[None, :], jnp.zeros((mod_rows - B - 1, D), F32)], axis=0)
    sg_all, su_all, sd_all = sh_w_gate.astype(BF16), sh_w_up.astype(BF16), sh_w_down.astype(BF16)

    xc = ctx
    for l in range(depth):
        update_ctx = l < depth - 1
        mod = ada_mod(c_rows, ada_w, ada_b, l)
        sh1, sc1, g1, sh2, sc2, g2 = (m[:, None, :] for m in jnp.split(mod[:B], 6, axis=-1))
        mod_c = jnp.broadcast_to(mod[B:B + 1], (B, 6 * D))
        sh1c, sc1c, g1c, sh2c, sc2c, g2c = (m[:, None, :] for m in jnp.split(mod_c, 6, axis=-1))
        lg_f = jax.nn.log_sigmoid(ret_decay_fwd[l].astype(F32))
        lg_b = jax.nn.log_sigmoid(ret_decay_bwd[l].astype(F32))

        w_in_b = w_in[l].astype(BF16)
        proj = ln_proj(x, sh1, sc1, w_in_b)
        proj_c = ln_proj(xc, sh1c, sc1c, w_in_b)

        y_na = window_attention(proj, proj_c, *window_bias_tables(na_rpb[l], rows, attn_rows))
        y_fn = fourier_mix(proj, cn, sn, cc, sc)
        ob_c, s_b = retention(proj_c, lg_b, zero_state, backward=True)
        if update_ctx:
            yrt_c, s_f = retention(proj_c, lg_f, zero_state, backward=False, finish=(ob_c, ret_gn_w, l))
        else:
            _, s_f = retention(proj_c, lg_f, zero_state, backward=False)
        ob, _ = retention(proj, lg_b, s_b, backward=True, rope_tabs=rope_tabs)
        y_ret, _ = retention(proj, lg_f, s_f, backward=False, rope_tabs=rope_tabs,
                             finish=(ob, ret_gn_w, l))

        wts = (w_o_na[l].astype(BF16), w_fourier[l].astype(BF16), w_o_ret[l].astype(BF16),
               w_out[l].astype(BF16))
        x, h = merge_branches(y_na, y_fn, y_ret, proj, *wts, x, g1, ln_mix_w, ln_mix_b, l, sh2, sc2, alpha)
        h_all = h.reshape(B * N, D)
        x_all = x.reshape(B * N, D)
        gate_tiles = jnp.repeat(g2, N // TOKEN_TILE, axis=0)
        if update_ctx:
            yna_c = context_attention(proj_c)
            yfn_c = fourier_mix(proj_c, cl, sl, cc, sc)
            xc, h_c = merge_branches(yna_c, yfn_c, yrt_c, proj_c, *wts, xc, g1c, ln_mix_w, ln_mix_b, l,
                                     sh2c, sc2c, alpha)
            h_all = jnp.concatenate([h_all, h_c.reshape(B * L, D)], axis=0)
            x_all = jnp.concatenate([x_all, xc.reshape(B * L, D)], axis=0)
            gate_tiles = jnp.concatenate([gate_tiles, jnp.repeat(g2c, L // TOKEN_TILE, axis=0)], axis=0)

        out = moe_sublayer(h_all, x_all, gate_tiles, router_w[l].T.astype(BF16), router_bias[l],
                           exp_w_gate, exp_w_up, exp_w_down, sg_all, su_all, sd_all,
                           ln_ffn_w, ln_ffn_b, l, alpha)
        x = out[:B * N].reshape(B, N, D)
        if update_ctx:
            xc = out[B * N:].reshape(B, L, D)
    return x
```

```python
import functools
import math

import numpy as np
import jax
import jax.numpy as jnp
from jax import lax
from jax.experimental import pallas as pl
from jax.experimental.pallas import tpu as pltpu

F32 = jnp.float32
BF16 = jnp.bfloat16
U32 = jnp.uint32

GRID_COLS = 64
NA_HEADS = 8
NA_DH = 64
NA_W = NA_HEADS * NA_DH
NA_ROWS = 8
NA_COLS = 16
FN_GROUPS = 4
FN_GD = 128
FN_W = FN_GROUPS * FN_GD
RT_HEADS = 4
RT_DK = 128
RT_DV = 256
RT_CHUNK = 128
ROPE_THETA = 10000.0
N_EXP = 256
TOPK = 8
N_GRP = 8
TOPK_GRP = 4
GRP_SZ = N_EXP // N_GRP
ROUTED_SCALE = 2.5
LN_EPS = 1e-6
GN_EPS = 1e-5

COL_QA, COL_KA, COL_VA, COL_U, COL_QR, COL_KR, COL_VR, COL_GR, COL_GL = (
    0, 512, 1024, 1536, 2048, 2560, 3072, 4096, 5120)

VMEM_LIMIT = 56 * 1024 * 1024
LANES = 128
SUBLANES = 8
BF16_ROWS = 16
MXU_ROWS = 256
EXPERT_BLOCK = 2 * MXU_ROWS
TOKEN_TILE = 256
ATTN_ROWS = 4
FOURIER_TILE = 512
NEG = -1e30


def _cparams(sem):
    return pltpu.CompilerParams(dimension_semantics=sem, vmem_limit_bytes=VMEM_LIMIT)


def _ln(x):
    mu = jnp.mean(x, axis=-1, keepdims=True)
    xc = x - mu
    var = jnp.mean(xc * xc, axis=-1, keepdims=True)
    return xc * lax.rsqrt(var + LN_EPS)


def _silu(x):
    return x * jax.nn.sigmoid(x)


def _ada_kernel(c_ref, w_ref, b_ref, o_ref):
    a = _silu(c_ref[...]).astype(BF16)
    o_ref[...] = jnp.dot(a, w_ref[0].astype(BF16), preferred_element_type=F32) + b_ref[0]


def ada_mod(c_rows, w_all, b_all, layer):
    R, D = c_rows.shape
    W = w_all.shape[2]
    tn = 1024
    return pl.pallas_call(
        _ada_kernel,
        out_shape=jax.ShapeDtypeStruct((R, W), F32),
        grid=(W // tn,),
        in_specs=[pl.BlockSpec((R, D), lambda j: (0, 0)),
                  pl.BlockSpec((1, D, tn), lambda j: (layer, 0, j)),
                  pl.BlockSpec((1, 1, tn), lambda j: (layer, 0, j))],
        out_specs=pl.BlockSpec((R, tn), lambda j: (0, j)),
        compiler_params=_cparams(("parallel",)),
        name="ada_mod",
    )(c_rows, w_all, b_all.reshape(b_all.shape[0], 1, W))


def _ln_proj_kernel(x_ref, sh_ref, sc_ref, w_ref, o_ref, h_scr):
    @pl.when(pl.program_id(2) == 0)
    def _():
        h = _ln(x_ref[0]) * (1.0 + sc_ref[0]) + sh_ref[0]
        h_scr[...] = h.astype(BF16)

    o_ref[0] = jnp.dot(h_scr[...], w_ref[...], preferred_element_type=F32).astype(o_ref.dtype)


def ln_proj(x, shift, scale, w_bf16):
    B, N, D = x.shape
    W = w_bf16.shape[1]
    tm = min(N, 1024)
    tn = 1024
    return pl.pallas_call(
        _ln_proj_kernel,
        out_shape=jax.ShapeDtypeStruct((B, N, W), BF16),
        grid=(B, N // tm, W // tn),
        in_specs=[pl.BlockSpec((1, tm, D), lambda b, i, j: (b, i, 0)),
                  pl.BlockSpec((1, 1, D), lambda b, i, j: (b, 0, 0)),
                  pl.BlockSpec((1, 1, D), lambda b, i, j: (b, 0, 0)),
                  pl.BlockSpec((D, tn), lambda b, i, j: (0, j))],
        out_specs=pl.BlockSpec((1, tm, tn), lambda b, i, j: (b, i, j)),
        scratch_shapes=[pltpu.VMEM((tm, D), BF16)],
        compiler_params=_cparams(("parallel", "parallel", "arbitrary")),
        name="ln_proj",
    )(x, shift, scale, w_bf16)


def _softmax_pv(s_parts, v):
    m = jnp.max(s_parts[0], axis=-1, keepdims=True)
    for s in s_parts[1:]:
        m = jnp.maximum(m, jnp.max(s, axis=-1, keepdims=True))
    ps = [jnp.exp(s - m) for s in s_parts]
    l = jnp.sum(ps[0], axis=-1, keepdims=True)
    for p in ps[1:]:
        l = l + jnp.sum(p, axis=-1, keepdims=True)
    p = ps[0] if len(ps) == 1 else jnp.concatenate(ps, axis=-1)
    return jnp.dot(p.astype(BF16), v, preferred_element_type=F32) / l


def _window_span(rpb):
    return NA_ROWS + rpb - 1


def _window_attn_kernel(pat_ref, q_ref, k_ref, v_ref, kc_ref, vc_ref, bias_ref, o_ref, kcat, vcat, *,
                        rows, rpb):
    del pat_ref
    L = kc_ref.shape[1]
    span = _window_span(rpb)
    win = span * GRID_COLS
    i = pl.program_id(1)

    @pl.when(i == 0)
    def _():
        kcat[0:L, :] = kc_ref[0]
        vcat[0:L, :] = vc_ref[0]

    first = jnp.clip(i * rpb - NA_ROWS // 2, 0, rows - span)
    start = pl.multiple_of(first * GRID_COLS, GRID_COLS)
    kcat[L:L + win, :] = k_ref[0, pl.ds(start, win), :]
    vcat[L:L + win, :] = v_ref[0, pl.ds(start, win), :]
    q = q_ref[0] * (NA_DH ** -0.5)
    dn = (((1,), (1,)), ((), ()))
    outs = []
    for h in range(NA_HEADS):
        hs = slice(h * NA_DH, (h + 1) * NA_DH)
        s = lax.dot_general(q[:, hs], kcat[:, hs], dn, preferred_element_type=F32)
        outs.append(_softmax_pv([s[:, :L], s[:, L:] + bias_ref[0, h]], vcat[:, hs]))
    o_ref[0] = jnp.concatenate(outs, axis=-1).astype(o_ref.dtype)


def _ctx_attn_kernel(q_ref, kc_ref, vc_ref, o_ref):
    q = q_ref[0] * (NA_DH ** -0.5)
    kc = kc_ref[0]
    vc = vc_ref[0]
    dn = (((1,), (1,)), ((), ()))
    outs = []
    for h in range(NA_HEADS):
        hs = slice(h * NA_DH, (h + 1) * NA_DH)
        s = lax.dot_general(q[:, hs], kc[:, hs], dn, preferred_element_type=F32)
        outs.append(_softmax_pv([s], vc[:, hs]))
    o_ref[0] = jnp.concatenate(outs, axis=-1).astype(o_ref.dtype)


def window_bias_tables(rpb_table, rows, rpb):
    H = rpb_table.shape[0]
    span = _window_span(rpb)
    cidx = np.arange(GRID_COLS)
    c0 = np.clip(cidx - NA_COLS // 2, 0, GRID_COLS - NA_COLS)
    col_ok = (cidx[None, :] >= c0[:, None]) & (cidx[None, :] < c0[:, None] + NA_COLS)
    dc = np.clip(cidx[None, :] - cidx[:, None], -(NA_COLS - 1), NA_COLS - 1) + (NA_COLS - 1)
    col_sel = (dc[:, :, None] == np.arange(2 * NA_COLS - 1)).astype(np.float32)

    keys, patterns, index = [], [], []
    for blk in range(rows // rpb):
        first = int(np.clip(blk * rpb - NA_ROWS // 2, 0, rows - span))
        key_row = first + np.arange(span)
        r = blk * rpb + np.arange(rpb)
        r0 = np.clip(r - NA_ROWS // 2, 0, rows - NA_ROWS)
        assert first <= r0.min() and r0.max() + NA_ROWS <= first + span
        valid = (key_row[None, :] >= r0[:, None]) & (key_row[None, :] < r0[:, None] + NA_ROWS)
        dr = np.where(valid, key_row[None, :] - r[:, None] + NA_ROWS - 1, -1)
        if dr.tobytes() not in keys:
            keys.append(dr.tobytes())
            patterns.append(dr)
        index.append(keys.index(dr.tobytes()))
    dr = np.stack(patterns)
    row_sel = (dr[..., None] == np.arange(2 * NA_ROWS - 1)).astype(np.float32)
    t = jnp.einsum('hab,prja,qkb->phrqjk', rpb_table.astype(F32), row_sel, col_sel,
                   precision=lax.Precision.HIGHEST)
    ok = (dr >= 0)[:, None, :, None, :, None] & col_ok[None, None, None, :, None, :]
    t = jnp.where(ok, t, NEG)
    tabs = t.reshape(len(patterns), H, rpb * GRID_COLS, span * GRID_COLS)
    return tabs, jnp.asarray(np.array(index, np.int32))


def window_attention(proj, proj_ctx, bias_tabs, bias_index):
    B, N, _ = proj.shape
    L = proj_ctx.shape[1]
    rows = N // GRID_COLS
    tq = bias_tabs.shape[2]
    rpb = tq // GRID_COLS
    win = _window_span(rpb) * GRID_COLS
    cb = lambda off: off // NA_W
    grid_spec = pltpu.PrefetchScalarGridSpec(
        num_scalar_prefetch=1,
        grid=(B, rows // rpb),
        in_specs=[pl.BlockSpec((1, tq, NA_W), lambda b, i, pat: (b, i, cb(COL_QA))),
                  pl.BlockSpec((1, N, NA_W), lambda b, i, pat: (b, 0, cb(COL_KA))),
                  pl.BlockSpec((1, N, NA_W), lambda b, i, pat: (b, 0, cb(COL_VA))),
                  pl.BlockSpec((1, L, NA_W), lambda b, i, pat: (b, 0, cb(COL_KA))),
                  pl.BlockSpec((1, L, NA_W), lambda b, i, pat: (b, 0, cb(COL_VA))),
                  pl.BlockSpec((1, NA_HEADS, tq, win), lambda b, i, pat: (pat[i], 0, 0, 0))],
        out_specs=pl.BlockSpec((1, tq, NA_W), lambda b, i, pat: (b, i, 0)),
        scratch_shapes=[pltpu.VMEM((L + win, NA_W), BF16), pltpu.VMEM((L + win, NA_W), BF16)],
    )
    return pl.pallas_call(
        functools.partial(_window_attn_kernel, rows=rows, rpb=rpb),
        out_shape=jax.ShapeDtypeStruct((B, N, NA_W), BF16),
        grid_spec=grid_spec,
        compiler_params=_cparams(("parallel", "arbitrary")),
        name="window_attention",
    )(bias_index, proj, proj, proj, proj_ctx, proj_ctx, bias_tabs)


def context_attention(proj_ctx):
    B, L, _ = proj_ctx.shape
    cb = lambda off: off // NA_W
    tq = min(L, 128)
    return pl.pallas_call(
        _ctx_attn_kernel,
        out_shape=jax.ShapeDtypeStruct((B, L, NA_W), BF16),
        grid=(B, L // tq),
        in_specs=[pl.BlockSpec((1, tq, NA_W), lambda b, r: (b, r, cb(COL_QA))),
                  pl.BlockSpec((1, L, NA_W), lambda b, r: (b, 0, cb(COL_KA))),
                  pl.BlockSpec((1, L, NA_W), lambda b, r: (b, 0, cb(COL_VA)))],
        out_specs=pl.BlockSpec((1, tq, NA_W), lambda b, r: (b, r, 0)),
        compiler_params=_cparams(("parallel", "arbitrary")),
        name="context_attention",
    )(proj_ctx, proj_ctx, proj_ctx)


def _dft_kernel(cn_ref, sn_ref, u_ref, cc_ref, sc_ref, o_ref, *, norm):
    u = u_ref[0]
    a = jnp.dot(cn_ref[...], u, preferred_element_type=F32).astype(BF16)
    b = jnp.dot(sn_ref[...], u, preferred_element_type=F32).astype(BF16)
    y = (jnp.dot(a, cc_ref[...], preferred_element_type=F32)
         - jnp.dot(b, sc_ref[...], preferred_element_type=F32))
    o_ref[0] = (y * norm).astype(o_ref.dtype)


def _unit_circle(rows, cols, period):
    ang = ((rows[:, None] * cols[None, :]) % period).astype(F32) * (2.0 * math.pi / period)
    return jnp.cos(ang), jnp.sin(ang)


def dft_tables(n):
    m = jnp.arange(n, dtype=jnp.int32)
    split = GRID_COLS
    if n <= split * split // 16 or n % split:
        return _unit_circle(m, m, n)
    hi_c, hi_s = _unit_circle(jnp.arange(n // split, dtype=jnp.int32), m, n // split)
    lo_c, lo_s = _unit_circle(jnp.arange(split, dtype=jnp.int32), m, n)
    cos = hi_c[:, None, :] * lo_c[None, :, :] - hi_s[:, None, :] * lo_s[None, :, :]
    sin = hi_s[:, None, :] * lo_c[None, :, :] + hi_c[:, None, :] * lo_s[None, :, :]
    return cos.reshape(n, n), sin.reshape(n, n)


def channel_dft_tables():
    c, s = dft_tables(FN_GD)
    eye = jnp.eye(FN_GROUPS, dtype=F32)
    return jnp.kron(eye, c).astype(BF16), jnp.kron(eye, s).astype(BF16)


def fourier_mix(proj, cn, sn, cc, sc):
    B, N, _ = proj.shape
    tk = min(N, FOURIER_TILE)
    norm = 1.0 / math.sqrt(N * FN_GD)
    return pl.pallas_call(
        functools.partial(_dft_kernel, norm=norm),
        out_shape=jax.ShapeDtypeStruct((B, N, FN_W), BF16),
        grid=(N // tk, B),
        in_specs=[pl.BlockSpec((tk, N), lambda i, b: (i, 0)),
                  pl.BlockSpec((tk, N), lambda i, b: (i, 0)),
                  pl.BlockSpec((1, N, FN_W), lambda i, b: (b, 0, COL_U // FN_W)),
                  pl.BlockSpec((FN_W, FN_W), lambda i, b: (0, 0)),
                  pl.BlockSpec((FN_W, FN_W), lambda i, b: (0, 0))],
        out_specs=pl.BlockSpec((1, tk, FN_W), lambda i, b: (b, i, 0)),
        compiler_params=_cparams(("parallel", "arbitrary")),
        name="fourier_mix",
    )(cn, sn, proj, cc, sc)


def _dft_half_kernel(cn_ref, sn_ref, cn1_ref, sn1_ref, u_ref, cc_ref, sc_ref, flip_ref, lo_ref, hi_ref, *,
                     norm):
    u = u_ref[0]

    def pq(c_rows, s_rows):
        a = jnp.dot(c_rows, u, preferred_element_type=F32).astype(BF16)
        b = jnp.dot(s_rows, u, preferred_element_type=F32).astype(BF16)
        return (jnp.dot(a, cc_ref[...], preferred_element_type=F32),
                jnp.dot(b, sc_ref[...], preferred_element_type=F32))

    p, q = pq(cn_ref[...], sn_ref[...])
    lo_ref[0] = ((p - q) * norm).astype(lo_ref.dtype)
    m = ((p + q) * norm).astype(BF16)
    mirrored = jnp.dot(flip_ref[...], m, preferred_element_type=F32)
    p1, q1 = pq(cn1_ref[...], sn1_ref[...])
    first = ((p1 + q1) * norm)[0:1]
    row = lax.broadcasted_iota(jnp.int32, mirrored.shape, 0)
    hi_ref[0] = jnp.where(row == 0, first, mirrored).astype(hi_ref.dtype)


def fourier_mix_half(proj, cn, sn, cc, sc):
    B, N, _ = proj.shape
    tk = FOURIER_TILE
    nt = N // (2 * tk)
    norm = 1.0 / math.sqrt(N * FN_GD)
    r = jnp.arange(tk, dtype=jnp.int32)
    flip = ((r[None, :] == tk - r[:, None]) & (r[:, None] >= 1)).astype(BF16)
    nxt = lambda t, b: ((t + 1) * (tk // BF16_ROWS), 0)
    half = jax.ShapeDtypeStruct((B, N // 2, FN_W), BF16)
    full = lambda a: pl.BlockSpec(a.shape, lambda t, b: (0,) * a.ndim)
    lo, hi = pl.pallas_call(
        functools.partial(_dft_half_kernel, norm=norm),
        out_shape=(half, half),
        grid=(nt, B),
        in_specs=[pl.BlockSpec((tk, N), lambda t, b: (t, 0)),
                  pl.BlockSpec((tk, N), lambda t, b: (t, 0)),
                  pl.BlockSpec((BF16_ROWS, N), nxt),
                  pl.BlockSpec((BF16_ROWS, N), nxt),
                  pl.BlockSpec((1, N, FN_W), lambda t, b: (b, 0, COL_U // FN_W)),
                  full(cc), full(sc), full(flip)],
        out_specs=(pl.BlockSpec((1, tk, FN_W), lambda t, b: (b, t, 0)),
                   pl.BlockSpec((1, tk, FN_W), lambda t, b: (b, nt - 1 - t, 0))),
        compiler_params=_cparams(("parallel", "arbitrary")),
        name="fourier_mix_half",
    )(cn, sn, cn, sn, proj, cc, sc, flip)
    return jnp.concatenate([lo, hi], axis=1)


def _retention_kernel(*refs, backward, rope, finish, cpb):
    it = iter(refs)
    lg_ref = next(it)
    q_ref, k_ref, v_ref = next(it), next(it), next(it)
    if rope:
        cos_ref, sin_ref, perm_ref = next(it), next(it), next(it)
    s0_ref = next(it)
    if finish:
        ob_ref, gate_ref, gn_ref = next(it), next(it), next(it)
    o_ref, sf_ref = next(it), next(it)
    state = next(it)

    C = RT_CHUNK
    step = pl.program_id(1)

    @pl.when(step == 0)
    def _():
        state[...] = s0_ref[0]

    pos_r = lax.broadcasted_iota(jnp.int32, (C, 1), 0).astype(F32)
    ci = lax.broadcasted_iota(jnp.int32, (C, C), 0)
    mi = lax.broadcasted_iota(jnp.int32, (C, C), 1)
    if backward:
        dist = jnp.maximum(mi - ci, 0).astype(F32)
        band = mi > ci
    else:
        dist = jnp.maximum(ci - mi, 0).astype(F32)
        band = ci >= mi

    q = q_ref[0]
    k = k_ref[0]
    if rope:
        cos = jnp.concatenate([cos_ref[...]] * RT_HEADS, axis=-1)
        sin = jnp.concatenate([sin_ref[...]] * RT_HEADS, axis=-1)
        perm = perm_ref[...]
        qf = q.astype(F32) * cos + jnp.dot(q, perm, preferred_element_type=F32) * sin
        kf = k.astype(F32) * cos + jnp.dot(k, perm, preferred_element_type=F32) * sin
    else:
        qf = q.astype(F32)
        kf = k.astype(F32)
    qf = qf * (RT_DK ** -0.5)

    order = range(cpb - 1, -1, -1) if backward else range(cpb)
    for h in range(RT_HEADS):
        lg = lg_ref[h]
        if backward:
            k_dec = jnp.exp(lg * pos_r)
            q_dec = jnp.exp(lg * (C - pos_r))
        else:
            k_dec = jnp.exp(lg * (C - 1 - pos_r))
            q_dec = jnp.exp(lg * (pos_r + 1))
        decay = jnp.where(band, jnp.exp(lg * dist), 0.0)
        chunk_dec = jnp.exp(lg * C)
        kcol = slice(h * RT_DK, (h + 1) * RT_DK)
        vcol = slice(h * RT_DV, (h + 1) * RT_DV)
        s_prev = state[h]
        for c in order:
            sl = slice(c * C, (c + 1) * C)
            qc = qf[sl, kcol]
            kc = kf[sl, kcol]
            vc = v_ref[0, sl, vcol]
            scores = lax.dot_general(qc.astype(BF16), kc.astype(BF16), (((1,), (1,)), ((), ())),
                                     preferred_element_type=F32) * decay
            lhs = jnp.concatenate([scores, qc * q_dec], axis=-1).astype(BF16)
            rhs = jnp.concatenate([vc, s_prev.astype(BF16)], axis=0)
            o = jnp.dot(lhs, rhs, preferred_element_type=F32)
            kv = lax.dot_general((kc * k_dec).astype(BF16), vc, (((0,), (0,)), ((), ())),
                                 preferred_element_type=F32)
            s_prev = chunk_dec * s_prev + kv
            if finish:
                o = o + ob_ref[0, sl, vcol]
                mu = jnp.mean(o, axis=-1, keepdims=True)
                oc = o - mu
                var = jnp.mean(oc * oc, axis=-1, keepdims=True)
                y = oc * lax.rsqrt(var + GN_EPS) * gn_ref[0, :, vcol]
                o = y * _silu(gate_ref[0, sl, vcol].astype(F32))
            o_ref[0, sl, vcol] = o.astype(o_ref.dtype)
        state[h] = s_prev

    @pl.when(step == pl.num_programs(1) - 1)
    def _():
        sf_ref[0] = state[...]


def rope_tables(n):
    t = jnp.arange(n)
    row, col = t // GRID_COLS, t % GRID_COLS
    half = RT_DK // 2
    n_pairs = half // 2
    inv_freq = ROPE_THETA ** (-jnp.arange(n_pairs, dtype=F32) / n_pairs)

    def cs(pos):
        ang = pos.astype(F32)[:, None] * inv_freq[None, :]
        c, s = jnp.cos(ang), jnp.sin(ang)
        return jnp.concatenate([c, c], axis=-1), jnp.concatenate([-s, s], axis=-1)

    cr, sr = cs(row)
    cc, sc = cs(col)
    cos = jnp.concatenate([cr, cc], axis=-1)
    sin = jnp.concatenate([sr, sc], axis=-1)
    idx = jnp.arange(RT_DK)
    src = (idx // half) * half + (idx % half + n_pairs) % half
    perm = (idx[:, None] == src[None, :]).astype(F32)
    return cos, sin, jnp.kron(jnp.eye(RT_HEADS, dtype=F32), perm).astype(BF16)


def retention(proj, log_g, s0, *, backward, rope_tabs=None, finish=None):
    B, N, _ = proj.shape
    nc = N // RT_CHUNK
    cpb = min(nc, 4)
    tb = cpb * RT_CHUNK
    nsteps = nc // cpb
    blk = (lambda s: nsteps - 1 - s) if backward else (lambda s: s)
    rope = rope_tabs is not None
    fin = finish is not None

    qk_w = RT_HEADS * RT_DK
    v_w = RT_HEADS * RT_DV
    state_spec = pl.BlockSpec((1, RT_HEADS, RT_DK, RT_DV), lambda b, s: (b, 0, 0, 0))
    in_specs = [pl.BlockSpec(memory_space=pltpu.SMEM),
                pl.BlockSpec((1, tb, qk_w), lambda b, s: (b, blk(s), COL_QR // qk_w)),
                pl.BlockSpec((1, tb, qk_w), lambda b, s: (b, blk(s), COL_KR // qk_w)),
                pl.BlockSpec((1, tb, v_w), lambda b, s: (b, blk(s), COL_VR // v_w))]
    args = [log_g, proj, proj, proj]
    if rope:
        in_specs += [pl.BlockSpec((tb, RT_DK), lambda b, s: (blk(s), 0)),
                     pl.BlockSpec((tb, RT_DK), lambda b, s: (blk(s), 0)),
                     pl.BlockSpec((qk_w, qk_w), lambda b, s: (0, 0))]
        args += list(rope_tabs)
    in_specs.append(state_spec)
    args.append(s0)
    if fin:
        o_b, gn_all, layer = finish
        in_specs += [pl.BlockSpec((1, tb, v_w), lambda b, s: (b, blk(s), 0)),
                     pl.BlockSpec((1, tb, v_w), lambda b, s: (b, blk(s), COL_GR // v_w)),
                     pl.BlockSpec((1, 1, v_w), lambda b, s: (layer, 0, 0))]
        args += [o_b, proj, gn_all.reshape(gn_all.shape[0], 1, -1)]
    out_dtype = BF16 if fin else F32
    o, s_fin = pl.pallas_call(
        functools.partial(_retention_kernel, backward=backward, rope=rope, finish=fin, cpb=cpb),
        out_shape=(jax.ShapeDtypeStruct((B, N, v_w), out_dtype),
                   jax.ShapeDtypeStruct((B, RT_HEADS, RT_DK, RT_DV), F32)),
        grid=(B, nsteps),
        in_specs=in_specs,
        out_specs=(pl.BlockSpec((1, tb, v_w), lambda b, s: (b, blk(s), 0)), state_spec),
        scratch_shapes=[pltpu.VMEM((RT_HEADS, RT_DK, RT_DV), F32)],
        compiler_params=_cparams(("parallel", "arbitrary")),
        name="retention_bwd" if backward else "retention_fwd",
    )(*args)
    return o, s_fin


def _merge_kernel(yna_ref, yfn_ref, yrt_ref, g0_ref, g1_ref, g2_ref, wna_ref, wfn_ref, wrt_ref,
                  wout_ref, x_ref, gate_ref, lnw_ref, lnb_ref, sh_ref, sc_ref, xo_ref, h_ref, *, alpha):
    a = jnp.dot(yna_ref[0], wna_ref[...], preferred_element_type=F32)
    y = jax.nn.sigmoid(g0_ref[0].astype(F32)) * a
    b = jnp.dot(yfn_ref[0], wfn_ref[...], preferred_element_type=F32)
    y = y + jax.nn.sigmoid(g1_ref[0].astype(F32)) * b
    c = jnp.dot(yrt_ref[0], wrt_ref[...], preferred_element_type=F32)
    y = y + jax.nn.sigmoid(g2_ref[0].astype(F32)) * c
    y = jnp.dot(y.astype(BF16), wout_ref[...], preferred_element_type=F32)
    xn = _ln(alpha * x_ref[0] + gate_ref[0] * y) * lnw_ref[0] + lnb_ref[0]
    xo_ref[0] = xn
    h_ref[0] = _ln(xn) * (1.0 + sc_ref[0]) + sh_ref[0]


def merge_branches(y_na, y_fn, y_ret, proj, w_na, w_fn, w_rt, w_out, x, gate, ln_w_all, ln_b_all, layer,
                   shift2, scale2, alpha):
    B, N, D = x.shape
    tm = min(N, 512)
    row = lambda w: pl.BlockSpec((1, tm, w), lambda b, i: (b, i, 0))
    glb = lambda j: pl.BlockSpec((1, tm, D), lambda b, i: (b, i, COL_GL // D + j))
    full = lambda a: pl.BlockSpec(a.shape, lambda b, i: (0,) * a.ndim)
    vec = pl.BlockSpec((1, 1, D), lambda b, i: (b, 0, 0))
    prm = pl.BlockSpec((1, 1, D), lambda b, i: (layer, 0, 0))
    depth = ln_w_all.shape[0]
    return pl.pallas_call(
        functools.partial(_merge_kernel, alpha=alpha),
        out_shape=(jax.ShapeDtypeStruct((B, N, D), F32), jax.ShapeDtypeStruct((B, N, D), F32)),
        grid=(B, N // tm),
        in_specs=[row(NA_W), row(FN_W), row(RT_HEADS * RT_DV), glb(0), glb(1), glb(2),
                  full(w_na), full(w_fn), full(w_rt), full(w_out),
                  row(D), vec, prm, prm, vec, vec],
        out_specs=(row(D), row(D)),
        compiler_params=_cparams(("parallel", "parallel")),
        name="merge_branches",
    )(y_na, y_fn, y_ret, proj, proj, proj, w_na, w_fn, w_rt, w_out, x, gate,
      ln_w_all.reshape(depth, 1, D), ln_b_all.reshape(depth, 1, D), shift2, scale2)


def _router_kernel(h_ref, w_ref, b_ref, e_ref, wcol_ref, rk_ref, cnt_ref, cnt_scr):
    tm = h_ref.shape[0]

    @pl.when(pl.program_id(0) == 0)
    def _():
        cnt_scr[...] = jnp.zeros_like(cnt_scr)

    logits = lax.dot_general(w_ref[...], h_ref[...].astype(BF16), (((1,), (1,)), ((), ())),
                             preferred_element_type=F32)
    scores = jax.nn.sigmoid(logits)
    sel = scores + b_ref[...]
    row = lax.broadcasted_iota(jnp.int32, (N_EXP, tm), 0)

    def first_max(vals, rows):
        m = jnp.max(vals, axis=0, keepdims=True)
        idx = jnp.min(jnp.where(vals == m, rows, N_EXP), axis=0, keepdims=True)
        return m, idx

    gscore = []
    grow = lax.broadcasted_iota(jnp.int32, (GRP_SZ, tm), 0)
    for g in range(N_GRP):
        gs = slice(g * GRP_SZ, (g + 1) * GRP_SZ)
        vals, rows = sel[gs], grow + g * GRP_SZ
        m1, i1 = first_max(vals, rows)
        m2, _ = first_max(jnp.where(rows == i1, -jnp.inf, vals), rows)
        gscore.append(m1 + m2)
    cand = []
    for g in range(N_GRP):
        beaten = jnp.zeros((1, tm), jnp.int32)
        for o in range(N_GRP):
            if o == g:
                continue
            ahead = (gscore[o] >= gscore[g]) if o < g else (gscore[o] > gscore[g])
            beaten = beaten + ahead.astype(jnp.int32)
        gs = slice(g * GRP_SZ, (g + 1) * GRP_SZ)
        cand.append(jnp.where(beaten < TOPK_GRP, sel[gs], -jnp.inf))
    cand = jnp.concatenate(cand, axis=0)

    chosen = jnp.zeros((N_EXP, tm), F32)
    ids, wts = [], []
    for _ in range(TOPK):
        _, idx = first_max(cand, row)
        hit = row == idx
        ids.append(idx)
        wts.append(jnp.sum(jnp.where(hit, scores, 0.0), axis=0, keepdims=True))
        chosen = jnp.where(hit, 1.0, chosen)
        cand = jnp.where(hit, -jnp.inf, cand)
    wsum = wts[0]
    for w in wts[1:]:
        wsum = wsum + w

    ti = lax.broadcasted_iota(jnp.int32, (tm, tm), 0)
    tj = lax.broadcasted_iota(jnp.int32, (tm, tm), 1)
    before = (ti < tj).astype(BF16)
    rank_full = jnp.dot(chosen.astype(BF16), before, preferred_element_type=F32) + cnt_scr[...]
    cnt_scr[...] = cnt_scr[...] + jnp.sum(chosen, axis=1, keepdims=True)

    ranks = [jnp.sum(jnp.where(row == ids[k], rank_full, 0.0), axis=0, keepdims=True) for k in range(TOPK)]
    e_ref[...] = jnp.concatenate(ids, axis=0)
    rk_ref[...] = jnp.concatenate(ranks, axis=0).astype(jnp.int32)
    wn = [w / wsum * ROUTED_SCALE for w in wts]
    wpad = jnp.concatenate(wn + [jnp.zeros((LANES - TOPK, tm), F32)], axis=0)
    wcol_ref[...] = wpad.T
    cnt_ref[...] = cnt_scr[...].astype(jnp.int32)


def router(h, w_t_bf16, bias):
    T, D = h.shape
    tm = TOKEN_TILE
    lane_out = pl.BlockSpec((TOPK, tm), lambda i: (0, i))
    return pl.pallas_call(
        _router_kernel,
        out_shape=(jax.ShapeDtypeStruct((TOPK, T), jnp.int32), jax.ShapeDtypeStruct((T, LANES), F32),
                   jax.ShapeDtypeStruct((TOPK, T), jnp.int32), jax.ShapeDtypeStruct((N_EXP, 1), jnp.int32)),
        grid=(T // tm,),
        in_specs=[pl.BlockSpec((tm, D), lambda i: (i, 0)),
                  pl.BlockSpec((N_EXP, D), lambda i: (0, 0)),
                  pl.BlockSpec((N_EXP, 1), lambda i: (0, 0))],
        out_specs=(lane_out, pl.BlockSpec((tm, LANES), lambda i: (i, 0)), lane_out,
                   pl.BlockSpec((N_EXP, 1), lambda i: (0, 0))),
        scratch_shapes=[pltpu.VMEM((N_EXP, 1), F32)],
        compiler_params=_cparams(("arbitrary",)),
        name="router",
    )(h, w_t_bf16, bias.reshape(N_EXP, 1))


def _slot_kernel(e_ref, rk_ref, ps_ref, o_ref):
    tm = e_ref.shape[1]
    row = lax.broadcasted_iota(jnp.int32, (N_EXP, tm), 0)
    ps = ps_ref[...].astype(F32)
    outs = []
    for k in range(TOPK):
        base = jnp.sum(jnp.where(row == e_ref[k:k + 1, :], ps, 0.0), axis=0, keepdims=True)
        outs.append(base.astype(jnp.int32) + rk_ref[k:k + 1, :])
    o_ref[...] = jnp.concatenate(outs, axis=0)


def assign_slots(top_e, rank, pad_start):
    K, T = top_e.shape
    tm = next(t for t in (2048, 1024, 512, 256) if T % t == 0)
    blk = pl.BlockSpec((K, tm), lambda i: (0, i))
    return pl.pallas_call(
        _slot_kernel,
        out_shape=jax.ShapeDtypeStruct((K, T), jnp.int32),
        grid=(T // tm,),
        in_specs=[blk, blk, pl.BlockSpec((N_EXP, 1), lambda i: (0, 0))],
        out_specs=blk,
        compiler_params=_cparams(("parallel",)),
        name="assign_slots",
    )(top_e, rank, pad_start.reshape(N_EXP, 1).astype(jnp.int32))


def _pack_pair(lo, hi):
    a = lax.bitcast_convert_type(lo.astype(BF16).astype(F32), U32) >> 16
    b = lax.bitcast_convert_type(hi.astype(BF16).astype(F32), U32) & jnp.uint32(0xFFFF0000)
    return a | b


def _unpack_pair(w):
    lo = lax.bitcast_convert_type(w << 16, F32)
    hi = lax.bitcast_convert_type(w & jnp.uint32(0xFFFF0000), F32)
    return lo, hi


def _dispatch_kernel(pe_ref, cnt_ref, nu_ref, slot_hbm, h_ref, sg_ref, su_ref, sd_ref, xs_hbm, sh_ref,
                     slot_smem, stage, zbuf, sem_idx, sem):
    tm, D = h_ref.shape
    ch = stage.shape[1]
    i = pl.program_id(0)
    nb = xs_hbm.shape[0] // EXPERT_BLOCK

    @pl.when(i == 0)
    def _():
        zbuf[...] = jnp.zeros_like(zbuf)

        def block_copy(b):
            start = pl.multiple_of(b * EXPERT_BLOCK, EXPERT_BLOCK)
            return pltpu.make_async_copy(zbuf, xs_hbm.at[pl.ds(start, EXPERT_BLOCK)], sem)

        def for_partial_blocks(fn):
            def body(e, carry):
                @pl.when(cnt_ref[e] % EXPERT_BLOCK != 0)
                def _():
                    fn(block_copy(pe_ref[e] // EXPERT_BLOCK - 1))
                return carry
            lax.fori_loop(0, N_EXP, body, 0)

        def for_unused_blocks(fn):
            def body(b, carry):
                fn(block_copy(b))
                return carry
            lax.fori_loop(nu_ref[0], nb, body, 0)

        for_partial_blocks(lambda cp: cp.start())
        for_unused_blocks(lambda cp: cp.start())
        for_partial_blocks(lambda cp: cp.wait())
        for_unused_blocks(lambda cp: cp.wait())

    idx_copy = pltpu.make_async_copy(slot_hbm.at[i], slot_smem, sem_idx)
    idx_copy.start()
    x = h_ref[...]
    half = D // 2
    for j in range(ch):
        stage[:, j, :] = _pack_pair(x[:, j * LANES:(j + 1) * LANES],
                                    x[:, half + j * LANES:half + (j + 1) * LANES])
    idx_copy.wait()

    def issue(t, carry):
        for k in range(TOPK):
            pltpu.make_async_copy(stage.at[t], xs_hbm.at[slot_smem[k * tm + t]], sem).start(priority=k % 2)
        return carry

    lax.fori_loop(0, tm, issue, 0)

    hb = x.astype(BF16)
    g = jnp.dot(hb, sg_ref[0], preferred_element_type=F32)
    u = jnp.dot(hb, su_ref[0], preferred_element_type=F32)
    sh_ref[...] = jnp.dot((_silu(g) * u).astype(BF16), sd_ref[0], preferred_element_type=F32)

    for k in range(TOPK):
        pltpu.make_async_copy(stage, xs_hbm.at[pl.ds(0, tm)], sem).wait()


def dispatch(h, slots, n_slots, pad_end, counts, n_used, sg, su, sd, layer):
    T, D = h.shape
    tm = TOKEN_TILE
    ch = D // 2 // LANES
    shw = lambda a: pl.BlockSpec((1,) + a.shape[1:], lambda i, pe, cnt, nu: (layer, 0, 0))
    grid_spec = pltpu.PrefetchScalarGridSpec(
        num_scalar_prefetch=3,
        grid=(T // tm,),
        in_specs=[pl.BlockSpec(memory_space=pl.ANY),
                  pl.BlockSpec((tm, D), lambda i, pe, cnt, nu: (i, 0)),
                  shw(sg), shw(su), shw(sd)],
        out_specs=(pl.BlockSpec(memory_space=pl.ANY),
                   pl.BlockSpec((tm, D), lambda i, pe, cnt, nu: (i, 0))),
        scratch_shapes=[pltpu.SMEM((TOPK * tm,), jnp.int32),
                        pltpu.VMEM((tm, ch, LANES), U32),
                        pltpu.VMEM((EXPERT_BLOCK, ch, LANES), U32),
                        pltpu.SemaphoreType.DMA(()),
                        pltpu.SemaphoreType.DMA(())],
    )
    return pl.pallas_call(
        _dispatch_kernel,
        out_shape=(jax.ShapeDtypeStruct((n_slots, ch, LANES), U32), jax.ShapeDtypeStruct((T, D), F32)),
        grid_spec=grid_spec,
        compiler_params=_cparams(("arbitrary",)),
        name="moe_dispatch",
    )(pad_end.astype(jnp.int32), counts.astype(jnp.int32), n_used, slots, h, sg, su, sd)


def _expert_kernel(be_ref, nu_ref, par_ref, nxt_ref, xs_ref, wg_hbm, wu_hbm, wd_hbm, ys_ref,
                   wg_f, wu_f, wd_f, wg_s, wu_s, wd_s, sem, *, layer):
    i = pl.program_id(0)
    used = i < nu_ref[0]
    new_expert = jnp.logical_or(i == 0, be_ref[i] != be_ref[jnp.maximum(i - 1, 0)])

    def weight_copies(e, b):
        return (pltpu.make_async_copy(wg_hbm.at[layer, e], wg_f.at[b], sem.at[b]),
                pltpu.make_async_copy(wu_hbm.at[layer, e], wu_f.at[b], sem.at[b]),
                pltpu.make_async_copy(wd_hbm.at[layer, e], wd_f.at[b], sem.at[b]))

    @pl.when(i == 0)
    def _():
        for cp in weight_copies(be_ref[0], 0):
            cp.start()

    @pl.when(jnp.logical_not(used))
    def _():
        ys_ref[...] = jnp.zeros_like(ys_ref)

    @pl.when(jnp.logical_and(used, new_expert))
    def _():
        b = par_ref[i]
        for cp in weight_copies(be_ref[i], b):
            cp.wait()

        @pl.when(nxt_ref[i] >= 0)
        def _():
            for cp in weight_copies(nxt_ref[i], 1 - b):
                cp.start()

        wg_s[...] = wg_f[b].astype(BF16)
        wu_s[...] = wu_f[b].astype(BF16)
        wd_s[...] = wd_f[b].astype(BF16)

    @pl.when(used)
    def _():
        D = wg_s.shape[0]
        ch = D // 2 // LANES
        half = D // 2
        for part in range(xs_ref.shape[0] // (ch * MXU_ROWS)):
            base = part * MXU_ROWS * ch
            pairs = [_unpack_pair(xs_ref[pl.ds(base + j, MXU_ROWS, stride=ch), :]) for j in range(ch)]
            x = jnp.concatenate([lo.astype(BF16) for lo, _ in pairs] + [hi.astype(BF16) for _, hi in pairs],
                                axis=-1)
            g = jnp.dot(x, wg_s[...], preferred_element_type=F32)
            u = jnp.dot(x, wu_s[...], preferred_element_type=F32)
            a = (_silu(g) * u).astype(BF16)
            y = jnp.dot(a, wd_s[...], preferred_element_type=F32)
            for j in range(ch):
                ys_ref[pl.ds(base + j, MXU_ROWS, stride=ch), :] = _pack_pair(
                    y[:, j * LANES:(j + 1) * LANES], y[:, half + j * LANES:half + (j + 1) * LANES])


def routed_experts(xs, block_expert, n_used, counts, w_gate_all, w_up_all, w_down_all, layer):
    n_slots, ch, _ = xs.shape
    _, E, D, Fd = w_gate_all.shape
    nb = n_slots // EXPERT_BLOCK
    rows_blk = EXPERT_BLOCK * ch
    starts = jnp.concatenate([jnp.ones((1,), jnp.int32),
                              (block_expert[1:] != block_expert[:-1]).astype(jnp.int32)])
    parity = ((jnp.cumsum(starts) - 1) % 2).astype(jnp.int32)
    ids = jnp.arange(E, dtype=jnp.int32)
    later = jnp.where((counts[None, :] > 0) & (ids[None, :] > ids[:, None]), ids[None, :], E)
    next_used = jnp.min(later, axis=1)
    next_used = jnp.where(next_used == E, -1, next_used).astype(jnp.int32)
    next_expert = jnp.sum(jnp.where(block_expert[:, None] == ids[None, :], next_used[None, :], 0),
                          axis=1).astype(jnp.int32)
    idx = lambda f: (lambda i, be, nu, par, nxt: f(i, nu))
    grid_spec = pltpu.PrefetchScalarGridSpec(
        num_scalar_prefetch=4,
        grid=(nb,),
        in_specs=[pl.BlockSpec((rows_blk, LANES), idx(lambda i, nu: (jnp.minimum(i, nu[0] - 1), 0))),
                  pl.BlockSpec(memory_space=pl.ANY),
                  pl.BlockSpec(memory_space=pl.ANY),
                  pl.BlockSpec(memory_space=pl.ANY)],
        out_specs=pl.BlockSpec((rows_blk, LANES), idx(lambda i, nu: (i, 0))),
        scratch_shapes=[pltpu.VMEM((2, D, Fd), F32), pltpu.VMEM((2, D, Fd), F32), pltpu.VMEM((2, Fd, D), F32),
                        pltpu.VMEM((D, Fd), BF16), pltpu.VMEM((D, Fd), BF16), pltpu.VMEM((Fd, D), BF16),
                        pltpu.SemaphoreType.DMA((2,))],
    )
    ys = pl.pallas_call(
        functools.partial(_expert_kernel, layer=layer),
        out_shape=jax.ShapeDtypeStruct((n_slots * ch, LANES), U32),
        grid_spec=grid_spec,
        compiler_params=_cparams(("arbitrary",)),
        name="routed_experts",
    )(block_expert, n_used, parity, next_expert, xs.reshape(n_slots * ch, LANES),
      w_gate_all, w_up_all, w_down_all)
    return ys.reshape(n_slots, ch, LANES)


def _combine_kernel(slot_hbm, ys_hbm, wt_ref, sh_ref, x_ref, gate_ref, lnw_ref, lnb_ref, o_ref,
                    slot_smem, buf, acc, sem_idx, sem, *, alpha):
    tm = x_ref.shape[0]
    ch = buf.shape[2]
    i = pl.program_id(0)
    cur = i % 2

    last = pl.num_programs(0) - 1
    half = ch * LANES

    per_tile = TOPK * tm

    def load_slots(tile, b):
        dst = slot_smem.at[pl.ds(pl.multiple_of(b * per_tile, per_tile), per_tile)]
        idx_copy = pltpu.make_async_copy(slot_hbm.at[tile], dst, sem_idx)
        idx_copy.start()
        idx_copy.wait()

    def issue_rows(t, b):
        for k in range(TOPK):
            pltpu.make_async_copy(ys_hbm.at[slot_smem[b * per_tile + k * tm + t]], buf.at[b, k, :, t, :],
                                  sem.at[b]).start(priority=k % 2)

    def wait_rows(b):
        for k in range(TOPK):
            pltpu.make_async_copy(buf.at[b, k], buf.at[b, k], sem.at[b]).wait()

    @pl.when(i == 0)
    def _():
        load_slots(0, 0)

        def first(t, carry):
            issue_rows(t, 0)
            return carry

        lax.fori_loop(0, tm, first, 0)

    nxt = 1 - cur
    load_slots(jnp.minimum(i + 1, last), nxt)
    wait_rows(cur)

    def reduce_and_issue(sb, carry):
        r0 = pl.multiple_of(sb * SUBLANES, SUBLANES)
        rows = pl.ds(r0, SUBLANES)
        for tt in range(SUBLANES):
            issue_rows(r0 + tt, nxt)
        wt = wt_ref[rows, :]
        for j in range(ch):
            lo, hi = _unpack_pair(buf[cur, 0, j, rows, :])
            r_lo, r_hi = wt[:, 0:1] * lo, wt[:, 0:1] * hi
            for k in range(1, TOPK):
                lo, hi = _unpack_pair(buf[cur, k, j, rows, :])
                r_lo, r_hi = r_lo + wt[:, k:k + 1] * lo, r_hi + wt[:, k:k + 1] * hi
            acc[rows, j * LANES:(j + 1) * LANES] = r_lo
            acc[rows, half + j * LANES:half + (j + 1) * LANES] = r_hi
        return carry

    lax.fori_loop(0, tm // SUBLANES, reduce_and_issue, 0)

    f = sh_ref[...] + acc[...]
    o_ref[...] = _ln(alpha * x_ref[...] + gate_ref[0] * f) * lnw_ref[0] + lnb_ref[0]

    @pl.when(i == last)
    def _():
        wait_rows(nxt)


def combine(slots, ys, wt, shared, x, gate_tiles, ln_w_all, ln_b_all, layer, alpha):
    T, D = x.shape
    tm = TOKEN_TILE
    ch = ys.shape[1]
    depth = ln_w_all.shape[0]
    prm = pl.BlockSpec((1, 1, D), lambda i: (layer, 0, 0))
    return pl.pallas_call(
        functools.partial(_combine_kernel, alpha=alpha),
        out_shape=jax.ShapeDtypeStruct((T, D), F32),
        grid=(T // tm,),
        in_specs=[pl.BlockSpec(memory_space=pl.ANY),
                  pl.BlockSpec(memory_space=pl.ANY),
                  pl.BlockSpec((tm, LANES), lambda i: (i, 0)),
                  pl.BlockSpec((tm, D), lambda i: (i, 0)),
                  pl.BlockSpec((tm, D), lambda i: (i, 0)),
                  pl.BlockSpec((1, 1, D), lambda i: (i, 0, 0)),
                  prm, prm],
        out_specs=pl.BlockSpec((tm, D), lambda i: (i, 0)),
        scratch_shapes=[pltpu.SMEM((2 * TOPK * tm,), jnp.int32),
                        pltpu.VMEM((2, TOPK, ch, tm, LANES), U32),
                        pltpu.VMEM((tm, D), F32),
                        pltpu.SemaphoreType.DMA(()),
                        pltpu.SemaphoreType.DMA((2,))],
        compiler_params=_cparams(("arbitrary",)),
        name="moe_combine",
    )(slots, ys, wt, shared, x, gate_tiles, ln_w_all.reshape(depth, 1, D), ln_b_all.reshape(depth, 1, D))


def moe_sublayer(h, x, gate_tiles, router_w_t, router_bias, w_gate_all, w_up_all, w_down_all, sg, su, sd,
                 ln_w_all, ln_b_all, layer, alpha):
    T, D = h.shape
    top_e, w_col, rank, counts = router(h, router_w_t, router_bias)
    counts = counts[:, 0]
    padded = (counts + EXPERT_BLOCK - 1) // EXPERT_BLOCK * EXPERT_BLOCK
    pad_end = jnp.cumsum(padded)
    pad_start = pad_end - padded
    slots = assign_slots(top_e, rank, pad_start)
    n_tiles = T // TOKEN_TILE
    slots = slots.reshape(TOPK, n_tiles, TOKEN_TILE).transpose(1, 0, 2).reshape(n_tiles, TOPK * TOKEN_TILE)
    n_slots = -(-(T * TOPK + N_EXP * (EXPERT_BLOCK - 1)) // EXPERT_BLOCK) * EXPERT_BLOCK
    nb = n_slots // EXPERT_BLOCK
    block_start = jnp.arange(nb, dtype=jnp.int32) * EXPERT_BLOCK
    block_expert = jnp.minimum(jnp.sum(block_start[:, None] >= pad_end[None, :], axis=1),
                               N_EXP - 1).astype(jnp.int32)
    n_used = (pad_end[-1:] // EXPERT_BLOCK).astype(jnp.int32)

    xs, shared = dispatch(h, slots, n_slots, pad_end, counts, n_used, sg, su, sd, layer)
    ys = routed_experts(xs, block_expert, n_used, counts, w_gate_all, w_up_all, w_down_all, layer)
    return combine(slots, ys, w_col, shared, x, gate_tiles, ln_w_all, ln_b_all, layer, alpha)


def kernel(x, c, ctx, c_ctx, ada_w, ada_b, w_in, na_rpb, w_o_na, w_fourier, ret_decay_fwd, ret_decay_bwd,
           ret_gn_w, w_o_ret, w_out, ln_mix_w, ln_mix_b, router_w, router_bias, exp_w_gate, exp_w_up,
           exp_w_down, sh_w_gate, sh_w_up, sh_w_down, ln_ffn_w, ln_ffn_b):
    B, N, D = x.shape
    L = ctx.shape[1]
    depth = ada_w.shape[0]
    alpha = (2.0 * depth) ** 0.25
    rows = N // GRID_COLS
    attn_rows = ATTN_ROWS if (rows % ATTN_ROWS == 0 and rows >= _window_span(ATTN_ROWS)) else 1

    rope_tabs = rope_tables(N)
    cn, sn = (t.astype(BF16) for t in dft_tables(N))
    cl, sl = (t.astype(BF16) for t in dft_tables(L))
    cc, sc = channel_dft_tables()
    zero_state = jnp.zeros((B, RT_HEADS, RT_DK, RT_DV), F32)
    mod_rows = -(-(B + 1) // BF16_ROWS) * BF16_ROWS
    c_rows = jnp.concatenate([c, c_ctx[None, :], jnp.zeros((mod_rows - B - 1, D), F32)], axis=0)
    sg_all, su_all, sd_all = sh_w_gate.astype(BF16), sh_w_up.astype(BF16), sh_w_down.astype(BF16)

    xc = ctx
    for l in range(depth):
        update_ctx = l < depth - 1
        mod = ada_mod(c_rows, ada_w, ada_b, l)
        sh1, sc1, g1, sh2, sc2, g2 = (m[:, None, :] for m in jnp.split(mod[:B], 6, axis=-1))
        mod_c = jnp.broadcast_to(mod[B:B + 1], (B, 6 * D))
        sh1c, sc1c, g1c, sh2c, sc2c, g2c = (m[:, None, :] for m in jnp.split(mod_c, 6, axis=-1))
        lg_f = jax.nn.log_sigmoid(ret_decay_fwd[l].astype(F32))
        lg_b = jax.nn.log_sigmoid(ret_decay_bwd[l].astype(F32))

        w_in_b = w_in[l].astype(BF16)
        proj = ln_proj(x, sh1, sc1, w_in_b)
        proj_c = ln_proj(xc, sh1c, sc1c, w_in_b)

        y_na = window_attention(proj, proj_c, *window_bias_tables(na_rpb[l], rows, attn_rows))
        y_fn = (fourier_mix_half if N % (2 * FOURIER_TILE) == 0 else fourier_mix)(proj, cn, sn, cc, sc)
        ob_c, s_b = retention(proj_c, lg_b, zero_state, backward=True)
        if update_ctx:
            yrt_c, s_f = retention(proj_c, lg_f, zero_state, backward=False, finish=(ob_c, ret_gn_w, l))
        else:
            _, s_f = retention(proj_c, lg_f, zero_state, backward=False)
        ob, _ = retention(proj, lg_b, s_b, backward=True, rope_tabs=rope_tabs)
        y_ret, _ = retention(proj, lg_f, s_f, backward=False, rope_tabs=rope_tabs,
                             finish=(ob, ret_gn_w, l))

        wts = (w_o_na[l].astype(BF16), w_fourier[l].astype(BF16), w_o_ret[l].astype(BF16),
               w_out[l].astype(BF16))
        x, h = merge_branches(y_na, y_fn, y_ret, proj, *wts, x, g1, ln_mix_w, ln_mix_b, l, sh2, sc2, alpha)
        h_all = h.reshape(B * N, D)
        x_all = x.reshape(B * N, D)
        gate_tiles = jnp.repeat(g2, N // TOKEN_TILE, axis=0)
        if update_ctx:
            yna_c = context_attention(proj_c)
            yfn_c = fourier_mix(proj_c, cl, sl, cc, sc)
            xc, h_c = merge_branches(yna_c, yfn_c, yrt_c, proj_c, *wts, xc, g1c, ln_mix_w, ln_mix_b, l,
                                     sh2c, sc2c, alpha)
            h_all = jnp.concatenate([h_all, h_c.reshape(B * L, D)], axis=0)
            x_all = jnp.concatenate([x_all, xc.reshape(B * L, D)], axis=0)
            gate_tiles = jnp.concatenate([gate_tiles, jnp.repeat(g2c, L // TOKEN_TILE, axis=0)], axis=0)

        out = moe_sublayer(h_all, x_all, gate_tiles, router_w[l].T.astype(BF16), router_bias[l],
                           exp_w_gate, exp_w_up, exp_w_down, sg_all, su_all, sd_all,
                           ln_ffn_w, ln_ffn_b, l, alpha)
        x = out[:B * N].reshape(B, N, D)
        if update_ctx:
            xc = out[B * N:].reshape(B, L, D)
    return x
```

```python
import functools
import math

import numpy as np
import jax
import jax.numpy as jnp
from jax import lax
from jax.experimental import pallas as pl
from jax.experimental.pallas import tpu as pltpu

F32 = jnp.float32
BF16 = jnp.bfloat16
U32 = jnp.uint32

GRID_COLS = 64
NA_HEADS = 8
NA_DH = 64
NA_W = NA_HEADS * NA_DH
NA_ROWS = 8
NA_COLS = 16
FN_GROUPS = 4
FN_GD = 128
FN_W = FN_GROUPS * FN_GD
RT_HEADS = 4
RT_DK = 128
RT_DV = 256
RT_CHUNK = 128
ROPE_THETA = 10000.0
N_EXP = 256
TOPK = 8
N_GRP = 8
TOPK_GRP = 4
GRP_SZ = N_EXP // N_GRP
ROUTED_SCALE = 2.5
LN_EPS = 1e-6
GN_EPS = 1e-5

COL_QA, COL_KA, COL_VA, COL_U, COL_QR, COL_KR, COL_VR, COL_GR, COL_GL = (
    0, 512, 1024, 1536, 2048, 2560, 3072, 4096, 5120)

VMEM_LIMIT = 56 * 1024 * 1024
LANES = 128
SUBLANES = 8
BF16_ROWS = 16
MXU_ROWS = 256
EXPERT_BLOCK = 2 * MXU_ROWS
TOKEN_TILE = 256
ATTN_ROWS = 4
FOURIER_TILE = 512
NEG = -1e30


def _cparams(sem):
    return pltpu.CompilerParams(dimension_semantics=sem, vmem_limit_bytes=VMEM_LIMIT)


def _ln(x):
    mu = jnp.mean(x, axis=-1, keepdims=True)
    xc = x - mu
    var = jnp.mean(xc * xc, axis=-1, keepdims=True)
    return xc * lax.rsqrt(var + LN_EPS)


def _silu(x):
    return x * jax.nn.sigmoid(x)


def _ada_kernel(c_ref, w_ref, b_ref, o_ref):
    a = _silu(c_ref[...]).astype(BF16)
    o_ref[...] = jnp.dot(a, w_ref[0].astype(BF16), preferred_element_type=F32) + b_ref[0]


def ada_mod(c_rows, w_all, b_all, layer):
    R, D = c_rows.shape
    W = w_all.shape[2]
    tn = 1024
    return pl.pallas_call(
        _ada_kernel,
        out_shape=jax.ShapeDtypeStruct((R, W), F32),
        grid=(W // tn,),
        in_specs=[pl.BlockSpec((R, D), lambda j: (0, 0)),
                  pl.BlockSpec((1, D, tn), lambda j: (layer, 0, j)),
                  pl.BlockSpec((1, 1, tn), lambda j: (layer, 0, j))],
        out_specs=pl.BlockSpec((R, tn), lambda j: (0, j)),
        compiler_params=_cparams(("parallel",)),
        name="ada_mod",
    )(c_rows, w_all, b_all.reshape(b_all.shape[0], 1, W))


def _ln_proj_kernel(x_ref, sh_ref, sc_ref, w_ref, o_ref, h_scr):
    @pl.when(pl.program_id(2) == 0)
    def _():
        h = _ln(x_ref[0]) * (1.0 + sc_ref[0]) + sh_ref[0]
        h_scr[...] = h.astype(BF16)

    o_ref[0] = jnp.dot(h_scr[...], w_ref[...], preferred_element_type=F32).astype(o_ref.dtype)


def ln_proj(x, shift, scale, w_bf16):
    B, N, D = x.shape
    W = w_bf16.shape[1]
    tm = min(N, 1024)
    tn = 2048
    return pl.pallas_call(
        _ln_proj_kernel,
        out_shape=jax.ShapeDtypeStruct((B, N, W), BF16),
        grid=(B, N // tm, W // tn),
        in_specs=[pl.BlockSpec((1, tm, D), lambda b, i, j: (b, i, 0)),
                  pl.BlockSpec((1, 1, D), lambda b, i, j: (b, 0, 0)),
                  pl.BlockSpec((1, 1, D), lambda b, i, j: (b, 0, 0)),
                  pl.BlockSpec((D, tn), lambda b, i, j: (0, j))],
        out_specs=pl.BlockSpec((1, tm, tn), lambda b, i, j: (b, i, j)),
        scratch_shapes=[pltpu.VMEM((tm, D), BF16)],
        compiler_params=_cparams(("parallel", "parallel", "arbitrary")),
        name="ln_proj",
    )(x, shift, scale, w_bf16)


def _softmax_pv(s_parts, v):
    m = jnp.max(s_parts[0], axis=-1, keepdims=True)
    for s in s_parts[1:]:
        m = jnp.maximum(m, jnp.max(s, axis=-1, keepdims=True))
    ps = [jnp.exp(s - m) for s in s_parts]
    l = jnp.sum(ps[0], axis=-1, keepdims=True)
    for p in ps[1:]:
        l = l + jnp.sum(p, axis=-1, keepdims=True)
    p = ps[0] if len(ps) == 1 else jnp.concatenate(ps, axis=-1)
    return jnp.dot(p.astype(BF16), v, preferred_element_type=F32) / l


def _window_span(rpb):
    return NA_ROWS + rpb - 1


def _window_attn_kernel(pat_ref, q_ref, k_ref, v_ref, kc_ref, vc_ref, bias_ref, o_ref, kcat, vcat, *,
                        rows, rpb):
    del pat_ref
    L = kc_ref.shape[1]
    span = _window_span(rpb)
    win = span * GRID_COLS
    i = pl.program_id(1)

    @pl.when(i == 0)
    def _():
        kcat[0:L, :] = kc_ref[0]
        vcat[0:L, :] = vc_ref[0]

    first = jnp.clip(i * rpb - NA_ROWS // 2, 0, rows - span)
    start = pl.multiple_of(first * GRID_COLS, GRID_COLS)
    kcat[L:L + win, :] = k_ref[0, pl.ds(start, win), :]
    vcat[L:L + win, :] = v_ref[0, pl.ds(start, win), :]
    q = q_ref[0] * (NA_DH ** -0.5)
    dn = (((1,), (1,)), ((), ()))
    outs = []
    for h in range(NA_HEADS):
        hs = slice(h * NA_DH, (h + 1) * NA_DH)
        s = lax.dot_general(q[:, hs], kcat[:, hs], dn, preferred_element_type=F32)
        outs.append(_softmax_pv([s[:, :L], s[:, L:] + bias_ref[0, h]], vcat[:, hs]))
    o_ref[0] = jnp.concatenate(outs, axis=-1).astype(o_ref.dtype)


def _ctx_attn_kernel(q_ref, kc_ref, vc_ref, o_ref):
    q = q_ref[0] * (NA_DH ** -0.5)
    kc = kc_ref[0]
    vc = vc_ref[0]
    dn = (((1,), (1,)), ((), ()))
    outs = []
    for h in range(NA_HEADS):
        hs = slice(h * NA_DH, (h + 1) * NA_DH)
        s = lax.dot_general(q[:, hs], kc[:, hs], dn, preferred_element_type=F32)
        outs.append(_softmax_pv([s], vc[:, hs]))
    o_ref[0] = jnp.concatenate(outs, axis=-1).astype(o_ref.dtype)


def window_bias_tables(rpb_table, rows, rpb):
    H = rpb_table.shape[0]
    span = _window_span(rpb)
    cidx = np.arange(GRID_COLS)
    c0 = np.clip(cidx - NA_COLS // 2, 0, GRID_COLS - NA_COLS)
    col_ok = (cidx[None, :] >= c0[:, None]) & (cidx[None, :] < c0[:, None] + NA_COLS)
    dc = np.clip(cidx[None, :] - cidx[:, None], -(NA_COLS - 1), NA_COLS - 1) + (NA_COLS - 1)
    col_sel = (dc[:, :, None] == np.arange(2 * NA_COLS - 1)).astype(np.float32)

    keys, patterns, index = [], [], []
    for blk in range(rows // rpb):
        first = int(np.clip(blk * rpb - NA_ROWS // 2, 0, rows - span))
        key_row = first + np.arange(span)
        r = blk * rpb + np.arange(rpb)
        r0 = np.clip(r - NA_ROWS // 2, 0, rows - NA_ROWS)
        assert first <= r0.min() and r0.max() + NA_ROWS <= first + span
        valid = (key_row[None, :] >= r0[:, None]) & (key_row[None, :] < r0[:, None] + NA_ROWS)
        dr = np.where(valid, key_row[None, :] - r[:, None] + NA_ROWS - 1, -1)
        if dr.tobytes() not in keys:
            keys.append(dr.tobytes())
            patterns.append(dr)
        index.append(keys.index(dr.tobytes()))
    dr = np.stack(patterns)
    row_sel = (dr[..., None] == np.arange(2 * NA_ROWS - 1)).astype(np.float32)
    t = jnp.einsum('hab,prja,qkb->phrqjk', rpb_table.astype(F32), row_sel, col_sel,
                   precision=lax.Precision.HIGHEST)
    ok = (dr >= 0)[:, None, :, None, :, None] & col_ok[None, None, None, :, None, :]
    t = jnp.where(ok, t, NEG)
    tabs = t.reshape(len(patterns), H, rpb * GRID_COLS, span * GRID_COLS)
    return tabs, jnp.asarray(np.array(index, np.int32))


def window_attention(proj, proj_ctx, bias_tabs, bias_index):
    B, N, _ = proj.shape
    L = proj_ctx.shape[1]
    rows = N // GRID_COLS
    tq = bias_tabs.shape[2]
    rpb = tq // GRID_COLS
    win = _window_span(rpb) * GRID_COLS
    cb = lambda off: off // NA_W
    grid_spec = pltpu.PrefetchScalarGridSpec(
        num_scalar_prefetch=1,
        grid=(B, rows // rpb),
        in_specs=[pl.BlockSpec((1, tq, NA_W), lambda b, i, pat: (b, i, cb(COL_QA))),
                  pl.BlockSpec((1, N, NA_W), lambda b, i, pat: (b, 0, cb(COL_KA))),
                  pl.BlockSpec((1, N, NA_W), lambda b, i, pat: (b, 0, cb(COL_VA))),
                  pl.BlockSpec((1, L, NA_W), lambda b, i, pat: (b, 0, cb(COL_KA))),
                  pl.BlockSpec((1, L, NA_W), lambda b, i, pat: (b, 0, cb(COL_VA))),
                  pl.BlockSpec((1, NA_HEADS, tq, win), lambda b, i, pat: (pat[i], 0, 0, 0))],
        out_specs=pl.BlockSpec((1, tq, NA_W), lambda b, i, pat: (b, i, 0)),
        scratch_shapes=[pltpu.VMEM((L + win, NA_W), BF16), pltpu.VMEM((L + win, NA_W), BF16)],
    )
    return pl.pallas_call(
        functools.partial(_window_attn_kernel, rows=rows, rpb=rpb),
        out_shape=jax.ShapeDtypeStruct((B, N, NA_W), BF16),
        grid_spec=grid_spec,
        compiler_params=_cparams(("parallel", "arbitrary")),
        name="window_attention",
    )(bias_index, proj, proj, proj, proj_ctx, proj_ctx, bias_tabs)


def context_attention(proj_ctx):
    B, L, _ = proj_ctx.shape
    cb = lambda off: off // NA_W
    tq = min(L, 128)
    return pl.pallas_call(
        _ctx_attn_kernel,
        out_shape=jax.ShapeDtypeStruct((B, L, NA_W), BF16),
        grid=(B, L // tq),
        in_specs=[pl.BlockSpec((1, tq, NA_W), lambda b, r: (b, r, cb(COL_QA))),
                  pl.BlockSpec((1, L, NA_W), lambda b, r: (b, 0, cb(COL_KA))),
                  pl.BlockSpec((1, L, NA_W), lambda b, r: (b, 0, cb(COL_VA)))],
        out_specs=pl.BlockSpec((1, tq, NA_W), lambda b, r: (b, r, 0)),
        compiler_params=_cparams(("parallel", "arbitrary")),
        name="context_attention",
    )(proj_ctx, proj_ctx, proj_ctx)


def _dft_kernel(cn_ref, sn_ref, u_ref, cc_ref, sc_ref, o_ref, *, norm):
    u = u_ref[0]
    a = jnp.dot(cn_ref[...], u, preferred_element_type=F32).astype(BF16)
    b = jnp.dot(sn_ref[...], u, preferred_element_type=F32).astype(BF16)
    y = (jnp.dot(a, cc_ref[...], preferred_element_type=F32)
         - jnp.dot(b, sc_ref[...], preferred_element_type=F32))
    o_ref[0] = (y * norm).astype(o_ref.dtype)


def _unit_circle(rows, cols, period):
    ang = ((rows[:, None] * cols[None, :]) % period).astype(F32) * (2.0 * math.pi / period)
    return jnp.cos(ang), jnp.sin(ang)


def dft_tables(n):
    m = jnp.arange(n, dtype=jnp.int32)
    split = GRID_COLS
    if n <= split * split // 16 or n % split:
        return _unit_circle(m, m, n)
    hi_c, hi_s = _unit_circle(jnp.arange(n // split, dtype=jnp.int32), m, n // split)
    lo_c, lo_s = _unit_circle(jnp.arange(split, dtype=jnp.int32), m, n)
    cos = hi_c[:, None, :] * lo_c[None, :, :] - hi_s[:, None, :] * lo_s[None, :, :]
    sin = hi_s[:, None, :] * lo_c[None, :, :] + hi_c[:, None, :] * lo_s[None, :, :]
    return cos.reshape(n, n), sin.reshape(n, n)


def channel_dft_tables():
    c, s = dft_tables(FN_GD)
    eye = jnp.eye(FN_GROUPS, dtype=F32)
    return jnp.kron(eye, c).astype(BF16), jnp.kron(eye, s).astype(BF16)


def fourier_mix(proj, cn, sn, cc, sc):
    B, N, _ = proj.shape
    tk = min(N, FOURIER_TILE)
    norm = 1.0 / math.sqrt(N * FN_GD)
    return pl.pallas_call(
        functools.partial(_dft_kernel, norm=norm),
        out_shape=jax.ShapeDtypeStruct((B, N, FN_W), BF16),
        grid=(N // tk, B),
        in_specs=[pl.BlockSpec((tk, N), lambda i, b: (i, 0)),
                  pl.BlockSpec((tk, N), lambda i, b: (i, 0)),
                  pl.BlockSpec((1, N, FN_W), lambda i, b: (b, 0, COL_U // FN_W)),
                  pl.BlockSpec((FN_W, FN_W), lambda i, b: (0, 0)),
                  pl.BlockSpec((FN_W, FN_W), lambda i, b: (0, 0))],
        out_specs=pl.BlockSpec((1, tk, FN_W), lambda i, b: (b, i, 0)),
        compiler_params=_cparams(("parallel", "arbitrary")),
        name="fourier_mix",
    )(cn, sn, proj, cc, sc)


def _dft_half_kernel(cn_ref, sn_ref, cn1_ref, sn1_ref, u_ref, cc_ref, sc_ref, flip_ref, lo_ref, hi_ref, *,
                     norm):
    u = u_ref[0]

    def pq(c_rows, s_rows):
        a = jnp.dot(c_rows, u, preferred_element_type=F32).astype(BF16)
        b = jnp.dot(s_rows, u, preferred_element_type=F32).astype(BF16)
        return (jnp.dot(a, cc_ref[...], preferred_element_type=F32),
                jnp.dot(b, sc_ref[...], preferred_element_type=F32))

    p, q = pq(cn_ref[...], sn_ref[...])
    lo_ref[0] = ((p - q) * norm).astype(lo_ref.dtype)
    m = ((p + q) * norm).astype(BF16)
    mirrored = jnp.dot(flip_ref[...], m, preferred_element_type=F32)
    p1, q1 = pq(cn1_ref[...], sn1_ref[...])
    first = ((p1 + q1) * norm)[0:1]
    row = lax.broadcasted_iota(jnp.int32, mirrored.shape, 0)
    hi_ref[0] = jnp.where(row == 0, first, mirrored).astype(hi_ref.dtype)


def fourier_mix_half(proj, cn, sn, cc, sc):
    B, N, _ = proj.shape
    tk = FOURIER_TILE
    nt = N // (2 * tk)
    norm = 1.0 / math.sqrt(N * FN_GD)
    r = jnp.arange(tk, dtype=jnp.int32)
    flip = ((r[None, :] == tk - r[:, None]) & (r[:, None] >= 1)).astype(BF16)
    nxt = lambda t, b: ((t + 1) * (tk // BF16_ROWS), 0)
    half = jax.ShapeDtypeStruct((B, N // 2, FN_W), BF16)
    full = lambda a: pl.BlockSpec(a.shape, lambda t, b: (0,) * a.ndim)
    lo, hi = pl.pallas_call(
        functools.partial(_dft_half_kernel, norm=norm),
        out_shape=(half, half),
        grid=(nt, B),
        in_specs=[pl.BlockSpec((tk, N), lambda t, b: (t, 0)),
                  pl.BlockSpec((tk, N), lambda t, b: (t, 0)),
                  pl.BlockSpec((BF16_ROWS, N), nxt),
                  pl.BlockSpec((BF16_ROWS, N), nxt),
                  pl.BlockSpec((1, N, FN_W), lambda t, b: (b, 0, COL_U // FN_W)),
                  full(cc), full(sc), full(flip)],
        out_specs=(pl.BlockSpec((1, tk, FN_W), lambda t, b: (b, t, 0)),
                   pl.BlockSpec((1, tk, FN_W), lambda t, b: (b, nt - 1 - t, 0))),
        compiler_params=_cparams(("parallel", "arbitrary")),
        name="fourier_mix_half",
    )(cn, sn, cn, sn, proj, cc, sc, flip)
    return jnp.concatenate([lo, hi], axis=1)


def _retention_kernel(*refs, backward, rope, finish, cpb):
    it = iter(refs)
    lg_ref = next(it)
    q_ref, k_ref, v_ref = next(it), next(it), next(it)
    if rope:
        cos_ref, sin_ref, perm_ref = next(it), next(it), next(it)
    s0_ref = next(it)
    if finish:
        ob_ref, gate_ref, gn_ref = next(it), next(it), next(it)
    o_ref, sf_ref = next(it), next(it)
    state = next(it)

    C = RT_CHUNK
    step = pl.program_id(1)

    @pl.when(step == 0)
    def _():
        state[...] = s0_ref[...]

    pos_r = lax.broadcasted_iota(jnp.int32, (C, 1), 0).astype(F32)
    ci = lax.broadcasted_iota(jnp.int32, (C, C), 0)
    mi = lax.broadcasted_iota(jnp.int32, (C, C), 1)
    if backward:
        dist = jnp.maximum(mi - ci, 0).astype(F32)
        band = mi > ci
    else:
        dist = jnp.maximum(ci - mi, 0).astype(F32)
        band = ci >= mi
    heads = []
    for h in range(RT_HEADS):
        lg = lg_ref[h]
        if backward:
            k_dec = jnp.exp(lg * pos_r)
            q_dec = jnp.exp(lg * (C - pos_r))
        else:
            k_dec = jnp.exp(lg * (C - 1 - pos_r))
            q_dec = jnp.exp(lg * (pos_r + 1))
        heads.append((k_dec, q_dec, jnp.where(band, jnp.exp(lg * dist), 0.0), jnp.exp(lg * C)))
    if rope:
        cos = jnp.concatenate([cos_ref[...]] * RT_HEADS, axis=-1)
        sin = jnp.concatenate([sin_ref[...]] * RT_HEADS, axis=-1)
        perm = perm_ref[...]

    order = range(cpb - 1, -1, -1) if backward else range(cpb)
    for bi in range(q_ref.shape[0]):
        q = q_ref[bi]
        k = k_ref[bi]
        if rope:
            qf = q.astype(F32) * cos + jnp.dot(q, perm, preferred_element_type=F32) * sin
            kf = k.astype(F32) * cos + jnp.dot(k, perm, preferred_element_type=F32) * sin
        else:
            qf = q.astype(F32)
            kf = k.astype(F32)
        qf = qf * (RT_DK ** -0.5)
        for h in range(RT_HEADS):
            k_dec, q_dec, decay, chunk_dec = heads[h]
            kcol = slice(h * RT_DK, (h + 1) * RT_DK)
            vcol = slice(h * RT_DV, (h + 1) * RT_DV)
            s_prev = state[bi, h]
            for c in order:
                sl = slice(c * C, (c + 1) * C)
                qc = qf[sl, kcol]
                kc = kf[sl, kcol]
                vc = v_ref[bi, sl, vcol]
                scores = lax.dot_general(qc.astype(BF16), kc.astype(BF16), (((1,), (1,)), ((), ())),
                                         preferred_element_type=F32) * decay
                lhs = jnp.concatenate([scores, qc * q_dec], axis=-1).astype(BF16)
                rhs = jnp.concatenate([vc, s_prev.astype(BF16)], axis=0)
                o = jnp.dot(lhs, rhs, preferred_element_type=F32)
                kv = lax.dot_general((kc * k_dec).astype(BF16), vc, (((0,), (0,)), ((), ())),
                                     preferred_element_type=F32)
                s_prev = chunk_dec * s_prev + kv
                if finish:
                    o = o + ob_ref[bi, sl, vcol]
                    mu = jnp.mean(o, axis=-1, keepdims=True)
                    oc = o - mu
                    var = jnp.mean(oc * oc, axis=-1, keepdims=True)
                    y = oc * lax.rsqrt(var + GN_EPS) * gn_ref[0, :, vcol]
                    o = y * _silu(gate_ref[bi, sl, vcol].astype(F32))
                o_ref[bi, sl, vcol] = o.astype(o_ref.dtype)
            state[bi, h] = s_prev

    @pl.when(step == pl.num_programs(1) - 1)
    def _():
        sf_ref[...] = state[...]


def rope_tables(n):
    t = jnp.arange(n)
    row, col = t // GRID_COLS, t % GRID_COLS
    half = RT_DK // 2
    n_pairs = half // 2
    inv_freq = ROPE_THETA ** (-jnp.arange(n_pairs, dtype=F32) / n_pairs)

    def cs(pos):
        ang = pos.astype(F32)[:, None] * inv_freq[None, :]
        c, s = jnp.cos(ang), jnp.sin(ang)
        return jnp.concatenate([c, c], axis=-1), jnp.concatenate([-s, s], axis=-1)

    cr, sr = cs(row)
    cc, sc = cs(col)
    cos = jnp.concatenate([cr, cc], axis=-1)
    sin = jnp.concatenate([sr, sc], axis=-1)
    idx = jnp.arange(RT_DK)
    src = (idx // half) * half + (idx % half + n_pairs) % half
    perm = (idx[:, None] == src[None, :]).astype(F32)
    return cos, sin, jnp.kron(jnp.eye(RT_HEADS, dtype=F32), perm).astype(BF16)


def retention(proj, log_g, s0, *, backward, rope_tabs=None, finish=None):
    B, N, _ = proj.shape
    nc = N // RT_CHUNK
    cpb = min(nc, 4)
    tb = cpb * RT_CHUNK
    nsteps = nc // cpb
    bpb = 1
    blk = (lambda s: nsteps - 1 - s) if backward else (lambda s: s)
    rope = rope_tabs is not None
    fin = finish is not None

    qk_w = RT_HEADS * RT_DK
    v_w = RT_HEADS * RT_DV
    state_spec = pl.BlockSpec((bpb, RT_HEADS, RT_DK, RT_DV), lambda b, s: (b, 0, 0, 0))
    in_specs = [pl.BlockSpec(memory_space=pltpu.SMEM),
                pl.BlockSpec((bpb, tb, qk_w), lambda b, s: (b, blk(s), COL_QR // qk_w)),
                pl.BlockSpec((bpb, tb, qk_w), lambda b, s: (b, blk(s), COL_KR // qk_w)),
                pl.BlockSpec((bpb, tb, v_w), lambda b, s: (b, blk(s), COL_VR // v_w))]
    args = [log_g, proj, proj, proj]
    if rope:
        in_specs += [pl.BlockSpec((tb, RT_DK), lambda b, s: (blk(s), 0)),
                     pl.BlockSpec((tb, RT_DK), lambda b, s: (blk(s), 0)),
                     pl.BlockSpec((qk_w, qk_w), lambda b, s: (0, 0))]
        args += list(rope_tabs)
    in_specs.append(state_spec)
    args.append(s0)
    if fin:
        o_b, gn_all, layer = finish
        in_specs += [pl.BlockSpec((bpb, tb, v_w), lambda b, s: (b, blk(s), 0)),
                     pl.BlockSpec((bpb, tb, v_w), lambda b, s: (b, blk(s), COL_GR // v_w)),
                     pl.BlockSpec((1, 1, v_w), lambda b, s: (layer, 0, 0))]
        args += [o_b, proj, gn_all.reshape(gn_all.shape[0], 1, -1)]
    out_dtype = BF16 if fin else F32
    o, s_fin = pl.pallas_call(
        functools.partial(_retention_kernel, backward=backward, rope=rope, finish=fin, cpb=cpb),
        out_shape=(jax.ShapeDtypeStruct((B, N, v_w), out_dtype),
                   jax.ShapeDtypeStruct((B, RT_HEADS, RT_DK, RT_DV), F32)),
        grid=(B // bpb, nsteps),
        in_specs=in_specs,
        out_specs=(pl.BlockSpec((bpb, tb, v_w), lambda b, s: (b, blk(s), 0)), state_spec),
        scratch_shapes=[pltpu.VMEM((bpb, RT_HEADS, RT_DK, RT_DV), F32)],
        compiler_params=_cparams(("parallel", "arbitrary")),
        name="retention_bwd" if backward else "retention_fwd",
    )(*args)
    return o, s_fin


def _merge_kernel(yna_ref, yfn_ref, yrt_ref, g0_ref, g1_ref, g2_ref, wna_ref, wfn_ref, wrt_ref,
                  wout_ref, x_ref, gate_ref, lnw_ref, lnb_ref, sh_ref, sc_ref, xo_ref, h_ref, *, alpha):
    a = jnp.dot(yna_ref[0], wna_ref[...], preferred_element_type=F32)
    y = jax.nn.sigmoid(g0_ref[0].astype(F32)) * a
    b = jnp.dot(yfn_ref[0], wfn_ref[...], preferred_element_type=F32)
    y = y + jax.nn.sigmoid(g1_ref[0].astype(F32)) * b
    c = jnp.dot(yrt_ref[0], wrt_ref[...], preferred_element_type=F32)
    y = y + jax.nn.sigmoid(g2_ref[0].astype(F32)) * c
    y = jnp.dot(y.astype(BF16), wout_ref[...], preferred_element_type=F32)
    xn = _ln(alpha * x_ref[0] + gate_ref[0] * y) * lnw_ref[0] + lnb_ref[0]
    xo_ref[0] = xn
    h_ref[0] = _ln(xn) * (1.0 + sc_ref[0]) + sh_ref[0]


def merge_branches(y_na, y_fn, y_ret, proj, w_na, w_fn, w_rt, w_out, x, gate, ln_w_all, ln_b_all, layer,
                   shift2, scale2, alpha):
    B, N, D = x.shape
    tm = min(N, 512)
    row = lambda w: pl.BlockSpec((1, tm, w), lambda b, i: (b, i, 0))
    glb = lambda j: pl.BlockSpec((1, tm, D), lambda b, i: (b, i, COL_GL // D + j))
    full = lambda a: pl.BlockSpec(a.shape, lambda b, i: (0,) * a.ndim)
    vec = pl.BlockSpec((1, 1, D), lambda b, i: (b, 0, 0))
    prm = pl.BlockSpec((1, 1, D), lambda b, i: (layer, 0, 0))
    depth = ln_w_all.shape[0]
    return pl.pallas_call(
        functools.partial(_merge_kernel, alpha=alpha),
        out_shape=(jax.ShapeDtypeStruct((B, N, D), F32), jax.ShapeDtypeStruct((B, N, D), F32)),
        grid=(B, N // tm),
        in_specs=[row(NA_W), row(FN_W), row(RT_HEADS * RT_DV), glb(0), glb(1), glb(2),
                  full(w_na), full(w_fn), full(w_rt), full(w_out),
                  row(D), vec, prm, prm, vec, vec],
        out_specs=(row(D), row(D)),
        compiler_params=_cparams(("parallel", "parallel")),
        name="merge_branches",
    )(y_na, y_fn, y_ret, proj, proj, proj, w_na, w_fn, w_rt, w_out, x, gate,
      ln_w_all.reshape(depth, 1, D), ln_b_all.reshape(depth, 1, D), shift2, scale2)


def _router_kernel(h_ref, w_ref, b_ref, e_ref, wcol_ref, rk_ref, cnt_ref, cnt_scr):
    tm = h_ref.shape[0]

    @pl.when(pl.program_id(0) == 0)
    def _():
        cnt_scr[...] = jnp.zeros_like(cnt_scr)

    logits = lax.dot_general(w_ref[...], h_ref[...].astype(BF16), (((1,), (1,)), ((), ())),
                             preferred_element_type=F32)
    scores = jax.nn.sigmoid(logits)
    sel = scores + b_ref[...]
    row = lax.broadcasted_iota(jnp.int32, (N_EXP, tm), 0)

    def first_max(vals, rows):
        m = jnp.max(vals, axis=0, keepdims=True)
        idx = jnp.min(jnp.where(vals == m, rows, N_EXP), axis=0, keepdims=True)
        return m, idx

    gscore = []
    grow = lax.broadcasted_iota(jnp.int32, (GRP_SZ, tm), 0)
    for g in range(N_GRP):
        gs = slice(g * GRP_SZ, (g + 1) * GRP_SZ)
        vals, rows = sel[gs], grow + g * GRP_SZ
        m1, i1 = first_max(vals, rows)
        m2, _ = first_max(jnp.where(rows == i1, -jnp.inf, vals), rows)
        gscore.append(m1 + m2)
    cand = []
    for g in range(N_GRP):
        beaten = jnp.zeros((1, tm), jnp.int32)
        for o in range(N_GRP):
            if o == g:
                continue
            ahead = (gscore[o] >= gscore[g]) if o < g else (gscore[o] > gscore[g])
            beaten = beaten + ahead.astype(jnp.int32)
        gs = slice(g * GRP_SZ, (g + 1) * GRP_SZ)
        cand.append(jnp.where(beaten < TOPK_GRP, sel[gs], -jnp.inf))
    cand = jnp.concatenate(cand, axis=0)

    chosen = jnp.zeros((N_EXP, tm), F32)
    ids, wts = [], []
    for _ in range(TOPK):
        _, idx = first_max(cand, row)
        hit = row == idx
        ids.append(idx)
        wts.append(jnp.sum(jnp.where(hit, scores, 0.0), axis=0, keepdims=True))
        chosen = jnp.where(hit, 1.0, chosen)
        cand = jnp.where(hit, -jnp.inf, cand)
    wsum = wts[0]
    for w in wts[1:]:
        wsum = wsum + w

    ti = lax.broadcasted_iota(jnp.int32, (tm, tm), 0)
    tj = lax.broadcasted_iota(jnp.int32, (tm, tm), 1)
    before = (ti < tj).astype(BF16)
    rank_full = jnp.dot(chosen.astype(BF16), before, preferred_element_type=F32) + cnt_scr[...]
    cnt_scr[...] = cnt_scr[...] + jnp.sum(chosen, axis=1, keepdims=True)

    ranks = [jnp.sum(jnp.where(row == ids[k], rank_full, 0.0), axis=0, keepdims=True) for k in range(TOPK)]
    e_ref[...] = jnp.concatenate(ids, axis=0)
    rk_ref[...] = jnp.concatenate(ranks, axis=0).astype(jnp.int32)
    wn = [w / wsum * ROUTED_SCALE for w in wts]
    wpad = jnp.concatenate(wn + [jnp.zeros((LANES - TOPK, tm), F32)], axis=0)
    wcol_ref[...] = wpad.T
    cnt_ref[...] = cnt_scr[...].astype(jnp.int32)


def router(h, w_t_bf16, bias):
    T, D = h.shape
    tm = TOKEN_TILE
    lane_out = pl.BlockSpec((TOPK, tm), lambda i: (0, i))
    return pl.pallas_call(
        _router_kernel,
        out_shape=(jax.ShapeDtypeStruct((TOPK, T), jnp.int32), jax.ShapeDtypeStruct((T, LANES), F32),
                   jax.ShapeDtypeStruct((TOPK, T), jnp.int32), jax.ShapeDtypeStruct((N_EXP, 1), jnp.int32)),
        grid=(T // tm,),
        in_specs=[pl.BlockSpec((tm, D), lambda i: (i, 0)),
                  pl.BlockSpec((N_EXP, D), lambda i: (0, 0)),
                  pl.BlockSpec((N_EXP, 1), lambda i: (0, 0))],
        out_specs=(lane_out, pl.BlockSpec((tm, LANES), lambda i: (i, 0)), lane_out,
                   pl.BlockSpec((N_EXP, 1), lambda i: (0, 0))),
        scratch_shapes=[pltpu.VMEM((N_EXP, 1), F32)],
        compiler_params=_cparams(("arbitrary",)),
        name="router",
    )(h, w_t_bf16, bias.reshape(N_EXP, 1))


def _slot_kernel(e_ref, rk_ref, ps_ref, o_ref):
    tm = e_ref.shape[1]
    row = lax.broadcasted_iota(jnp.int32, (N_EXP, tm), 0)
    ps = ps_ref[...].astype(F32)
    outs = []
    for k in range(TOPK):
        base = jnp.sum(jnp.where(row == e_ref[k:k + 1, :], ps, 0.0), axis=0, keepdims=True)
        outs.append(base.astype(jnp.int32) + rk_ref[k:k + 1, :])
    o_ref[...] = jnp.concatenate(outs, axis=0)


def assign_slots(top_e, rank, pad_start):
    K, T = top_e.shape
    tm = next(t for t in (2048, 1024, 512, 256) if T % t == 0)
    blk = pl.BlockSpec((K, tm), lambda i: (0, i))
    return pl.pallas_call(
        _slot_kernel,
        out_shape=jax.ShapeDtypeStruct((K, T), jnp.int32),
        grid=(T // tm,),
        in_specs=[blk, blk, pl.BlockSpec((N_EXP, 1), lambda i: (0, 0))],
        out_specs=blk,
        compiler_params=_cparams(("parallel",)),
        name="assign_slots",
    )(top_e, rank, pad_start.reshape(N_EXP, 1).astype(jnp.int32))


def _pack_pair(lo, hi):
    a = lax.bitcast_convert_type(lo.astype(BF16).astype(F32), U32) >> 16
    b = lax.bitcast_convert_type(hi.astype(BF16).astype(F32), U32) & jnp.uint32(0xFFFF0000)
    return a | b


def _unpack_pair(w):
    lo = lax.bitcast_convert_type(w << 16, F32)
    hi = lax.bitcast_convert_type(w & jnp.uint32(0xFFFF0000), F32)
    return lo, hi


def _dispatch_kernel(pe_ref, cnt_ref, nu_ref, slot_hbm, h_ref, sg_ref, su_ref, sd_ref, xs_hbm, sh_ref,
                     slot_smem, stage, zbuf, sem_idx, sem):
    tm, D = h_ref.shape
    ch = stage.shape[1]
    i = pl.program_id(0)
    nb = xs_hbm.shape[0] // EXPERT_BLOCK

    @pl.when(i == 0)
    def _():
        zbuf[...] = jnp.zeros_like(zbuf)

        def block_copy(b):
            start = pl.multiple_of(b * EXPERT_BLOCK, EXPERT_BLOCK)
            return pltpu.make_async_copy(zbuf, xs_hbm.at[pl.ds(start, EXPERT_BLOCK)], sem)

        def for_partial_blocks(fn):
            def body(e, carry):
                @pl.when(cnt_ref[e] % EXPERT_BLOCK != 0)
                def _():
                    fn(block_copy(pe_ref[e] // EXPERT_BLOCK - 1))
                return carry
            lax.fori_loop(0, N_EXP, body, 0)

        def for_unused_blocks(fn):
            def body(b, carry):
                fn(block_copy(b))
                return carry
            lax.fori_loop(nu_ref[0], nb, body, 0)

        for_partial_blocks(lambda cp: cp.start())
        for_unused_blocks(lambda cp: cp.start())
        for_partial_blocks(lambda cp: cp.wait())
        for_unused_blocks(lambda cp: cp.wait())

    idx_copy = pltpu.make_async_copy(slot_hbm.at[i], slot_smem, sem_idx)
    idx_copy.start()
    x = h_ref[...]
    half = D // 2
    for j in range(ch):
        stage[:, j, :] = _pack_pair(x[:, j * LANES:(j + 1) * LANES],
                                    x[:, half + j * LANES:half + (j + 1) * LANES])
    idx_copy.wait()

    def issue(t, carry):
        for k in range(TOPK):
            pltpu.make_async_copy(stage.at[t], xs_hbm.at[slot_smem[k * tm + t]], sem).start(priority=k % 2)
        return carry

    lax.fori_loop(0, tm, issue, 0)

    hb = x.astype(BF16)
    g = jnp.dot(hb, sg_ref[0], preferred_element_type=F32)
    u = jnp.dot(hb, su_ref[0], preferred_element_type=F32)
    sh_ref[...] = jnp.dot((_silu(g) * u).astype(BF16), sd_ref[0], preferred_element_type=F32)

    for k in range(TOPK):
        pltpu.make_async_copy(stage, xs_hbm.at[pl.ds(0, tm)], sem).wait()


def dispatch(h, slots, n_slots, pad_end, counts, n_used, sg, su, sd, layer):
    T, D = h.shape
    tm = TOKEN_TILE
    ch = D // 2 // LANES
    shw = lambda a: pl.BlockSpec((1,) + a.shape[1:], lambda i, pe, cnt, nu: (layer, 0, 0))
    grid_spec = pltpu.PrefetchScalarGridSpec(
        num_scalar_prefetch=3,
        grid=(T // tm,),
        in_specs=[pl.BlockSpec(memory_space=pl.ANY),
                  pl.BlockSpec((tm, D), lambda i, pe, cnt, nu: (i, 0)),
                  shw(sg), shw(su), shw(sd)],
        out_specs=(pl.BlockSpec(memory_space=pl.ANY),
                   pl.BlockSpec((tm, D), lambda i, pe, cnt, nu: (i, 0))),
        scratch_shapes=[pltpu.SMEM((TOPK * tm,), jnp.int32),
                        pltpu.VMEM((tm, ch, LANES), U32),
                        pltpu.VMEM((EXPERT_BLOCK, ch, LANES), U32),
                        pltpu.SemaphoreType.DMA(()),
                        pltpu.SemaphoreType.DMA(())],
    )
    return pl.pallas_call(
        _dispatch_kernel,
        out_shape=(jax.ShapeDtypeStruct((n_slots, ch, LANES), U32), jax.ShapeDtypeStruct((T, D), F32)),
        grid_spec=grid_spec,
        compiler_params=_cparams(("arbitrary",)),
        name="moe_dispatch",
    )(pad_end.astype(jnp.int32), counts.astype(jnp.int32), n_used, slots, h, sg, su, sd)


def _expert_kernel(be_ref, nu_ref, par_ref, nxt_ref, xs_ref, wg_hbm, wu_hbm, wd_hbm, ys_ref,
                   wg_f, wu_f, wd_f, wg_s, wu_s, wd_s, sem, *, layer):
    i = pl.program_id(0)
    used = i < nu_ref[0]
    new_expert = jnp.logical_or(i == 0, be_ref[i] != be_ref[jnp.maximum(i - 1, 0)])

    def weight_copies(e, b):
        return (pltpu.make_async_copy(wg_hbm.at[layer, e], wg_f.at[b], sem.at[b]),
                pltpu.make_async_copy(wu_hbm.at[layer, e], wu_f.at[b], sem.at[b]),
                pltpu.make_async_copy(wd_hbm.at[layer, e], wd_f.at[b], sem.at[b]))

    @pl.when(i == 0)
    def _():
        for cp in weight_copies(be_ref[0], 0):
            cp.start()

    @pl.when(jnp.logical_not(used))
    def _():
        ys_ref[...] = jnp.zeros_like(ys_ref)

    @pl.when(jnp.logical_and(used, new_expert))
    def _():
        b = par_ref[i]
        for cp in weight_copies(be_ref[i], b):
            cp.wait()

        @pl.when(nxt_ref[i] >= 0)
        def _():
            for cp in weight_copies(nxt_ref[i], 1 - b):
                cp.start()

        wg_s[...] = wg_f[b].astype(BF16)
        wu_s[...] = wu_f[b].astype(BF16)
        wd_s[...] = wd_f[b].astype(BF16)

    @pl.when(used)
    def _():
        D = wg_s.shape[0]
        ch = D // 2 // LANES
        half = D // 2
        for part in range(xs_ref.shape[0] // (ch * MXU_ROWS)):
            base = part * MXU_ROWS * ch
            pairs = [_unpack_pair(xs_ref[pl.ds(base + j, MXU_ROWS, stride=ch), :]) for j in range(ch)]
            x = jnp.concatenate([lo.astype(BF16) for lo, _ in pairs] + [hi.astype(BF16) for _, hi in pairs],
                                axis=-1)
            g = jnp.dot(x, wg_s[...], preferred_element_type=F32)
            u = jnp.dot(x, wu_s[...], preferred_element_type=F32)
            a = (_silu(g) * u).astype(BF16)
            y = jnp.dot(a, wd_s[...], preferred_element_type=F32)
            for j in range(ch):
                ys_ref[pl.ds(base + j, MXU_ROWS, stride=ch), :] = _pack_pair(
                    y[:, j * LANES:(j + 1) * LANES], y[:, half + j * LANES:half + (j + 1) * LANES])


def routed_experts(xs, block_expert, n_used, counts, w_gate_all, w_up_all, w_down_all, layer):
    n_slots, ch, _ = xs.shape
    _, E, D, Fd = w_gate_all.shape
    nb = n_slots // EXPERT_BLOCK
    rows_blk = EXPERT_BLOCK * ch
    starts = jnp.concatenate([jnp.ones((1,), jnp.int32),
                              (block_expert[1:] != block_expert[:-1]).astype(jnp.int32)])
    parity = ((jnp.cumsum(starts) - 1) % 2).astype(jnp.int32)
    ids = jnp.arange(E, dtype=jnp.int32)
    later = jnp.where((counts[None, :] > 0) & (ids[None, :] > ids[:, None]), ids[None, :], E)
    next_used = jnp.min(later, axis=1)
    next_used = jnp.where(next_used == E, -1, next_used).astype(jnp.int32)
    next_expert = jnp.sum(jnp.where(block_expert[:, None] == ids[None, :], next_used[None, :], 0),
                          axis=1).astype(jnp.int32)
    idx = lambda f: (lambda i, be, nu, par, nxt: f(i, nu))
    grid_spec = pltpu.PrefetchScalarGridSpec(
        num_scalar_prefetch=4,
        grid=(nb,),
        in_specs=[pl.BlockSpec((rows_blk, LANES), idx(lambda i, nu: (jnp.minimum(i, nu[0] - 1), 0))),
                  pl.BlockSpec(memory_space=pl.ANY),
                  pl.BlockSpec(memory_space=pl.ANY),
                  pl.BlockSpec(memory_space=pl.ANY)],
        out_specs=pl.BlockSpec((rows_blk, LANES), idx(lambda i, nu: (i, 0))),
        scratch_shapes=[pltpu.VMEM((2, D, Fd), F32), pltpu.VMEM((2, D, Fd), F32), pltpu.VMEM((2, Fd, D), F32),
                        pltpu.VMEM((D, Fd), BF16), pltpu.VMEM((D, Fd), BF16), pltpu.VMEM((Fd, D), BF16),
                        pltpu.SemaphoreType.DMA((2,))],
    )
    ys = pl.pallas_call(
        functools.partial(_expert_kernel, layer=layer),
        out_shape=jax.ShapeDtypeStruct((n_slots * ch, LANES), U32),
        grid_spec=grid_spec,
        compiler_params=_cparams(("arbitrary",)),
        name="routed_experts",
    )(block_expert, n_used, parity, next_expert, xs.reshape(n_slots * ch, LANES),
      w_gate_all, w_up_all, w_down_all)
    return ys.reshape(n_slots, ch, LANES)


def _combine_kernel(slot_hbm, ys_hbm, wt_ref, sh_ref, x_ref, gate_ref, lnw_ref, lnb_ref, o_ref,
                    slot_smem, buf, acc, sem_idx, sem, *, alpha):
    tm = x_ref.shape[0]
    ch = buf.shape[2]
    i = pl.program_id(0)
    cur = i % 2

    last = pl.num_programs(0) - 1
    half = ch * LANES

    per_tile = TOPK * tm

    def load_slots(tile, b):
        dst = slot_smem.at[pl.ds(pl.multiple_of(b * per_tile, per_tile), per_tile)]
        idx_copy = pltpu.make_async_copy(slot_hbm.at[tile], dst, sem_idx)
        idx_copy.start()
        idx_copy.wait()

    def issue_rows(t, b):
        for k in range(TOPK):
            pltpu.make_async_copy(ys_hbm.at[slot_smem[b * per_tile + k * tm + t]], buf.at[b, k, :, t, :],
                                  sem.at[b]).start(priority=k % 2)

    def wait_rows(b):
        for k in range(TOPK):
            pltpu.make_async_copy(buf.at[b, k], buf.at[b, k], sem.at[b]).wait()

    @pl.when(i == 0)
    def _():
        load_slots(0, 0)

        def first(t, carry):
            issue_rows(t, 0)
            return carry

        lax.fori_loop(0, tm, first, 0)

    nxt = 1 - cur
    load_slots(jnp.minimum(i + 1, last), nxt)
    wait_rows(cur)

    def reduce_and_issue(sb, carry):
        r0 = pl.multiple_of(sb * SUBLANES, SUBLANES)
        rows = pl.ds(r0, SUBLANES)
        for tt in range(SUBLANES):
            issue_rows(r0 + tt, nxt)
        wt = wt_ref[rows, :]
        for j in range(ch):
            lo, hi = _unpack_pair(buf[cur, 0, j, rows, :])
            r_lo, r_hi = wt[:, 0:1] * lo, wt[:, 0:1] * hi
            for k in range(1, TOPK):
                lo, hi = _unpack_pair(buf[cur, k, j, rows, :])
                r_lo, r_hi = r_lo + wt[:, k:k + 1] * lo, r_hi + wt[:, k:k + 1] * hi
            acc[rows, j * LANES:(j + 1) * LANES] = r_lo
            acc[rows, half + j * LANES:half + (j + 1) * LANES] = r_hi
        return carry

    lax.fori_loop(0, tm // SUBLANES, reduce_and_issue, 0)

    f = sh_ref[...] + acc[...]
    o_ref[...] = _ln(alpha * x_ref[...] + gate_ref[0] * f) * lnw_ref[0] + lnb_ref[0]

    @pl.when(i == last)
    def _():
        wait_rows(nxt)


def combine(slots, ys, wt, shared, x, gate_tiles, ln_w_all, ln_b_all, layer, alpha):
    T, D = x.shape
    tm = TOKEN_TILE
    ch = ys.shape[1]
    depth = ln_w_all.shape[0]
    prm = pl.BlockSpec((1, 1, D), lambda i: (layer, 0, 0))
    return pl.pallas_call(
        functools.partial(_combine_kernel, alpha=alpha),
        out_shape=jax.ShapeDtypeStruct((T, D), F32),
        grid=(T // tm,),
        in_specs=[pl.BlockSpec(memory_space=pl.ANY),
                  pl.BlockSpec(memory_space=pl.ANY),
                  pl.BlockSpec((tm, LANES), lambda i: (i, 0)),
                  pl.BlockSpec((tm, D), lambda i: (i, 0)),
                  pl.BlockSpec((tm, D), lambda i: (i, 0)),
                  pl.BlockSpec((1, 1, D), lambda i: (i, 0, 0)),
                  prm, prm],
        out_specs=pl.BlockSpec((tm, D), lambda i: (i, 0)),
        scratch_shapes=[pltpu.SMEM((2 * TOPK * tm,), jnp.int32),
                        pltpu.VMEM((2, TOPK, ch, tm, LANES), U32),
                        pltpu.VMEM((tm, D), F32),
                        pltpu.SemaphoreType.DMA(()),
                        pltpu.SemaphoreType.DMA((2,))],
        compiler_params=_cparams(("arbitrary",)),
        name="moe_combine",
    )(slots, ys, wt, shared, x, gate_tiles, ln_w_all.reshape(depth, 1, D), ln_b_all.reshape(depth, 1, D))


def moe_sublayer(h, x, gate_tiles, router_w_t, router_bias, w_gate_all, w_up_all, w_down_all, sg, su, sd,
                 ln_w_all, ln_b_all, layer, alpha):
    T, D = h.shape
    top_e, w_col, rank, counts = router(h, router_w_t, router_bias)
    counts = counts[:, 0]
    padded = (counts + EXPERT_BLOCK - 1) // EXPERT_BLOCK * EXPERT_BLOCK
    pad_end = jnp.cumsum(padded)
    pad_start = pad_end - padded
    slots = assign_slots(top_e, rank, pad_start)
    n_tiles = T // TOKEN_TILE
    slots = slots.reshape(TOPK, n_tiles, TOKEN_TILE).transpose(1, 0, 2).reshape(n_tiles, TOPK * TOKEN_TILE)
    n_slots = -(-(T * TOPK + N_EXP * (EXPERT_BLOCK - 1)) // EXPERT_BLOCK) * EXPERT_BLOCK
    nb = n_slots // EXPERT_BLOCK
    block_start = jnp.arange(nb, dtype=jnp.int32) * EXPERT_BLOCK
    block_expert = jnp.minimum(jnp.sum(block_start[:, None] >= pad_end[None, :], axis=1),
                               N_EXP - 1).astype(jnp.int32)
    n_used = (pad_end[-1:] // EXPERT_BLOCK).astype(jnp.int32)

    xs, shared = dispatch(h, slots, n_slots, pad_end, counts, n_used, sg, su, sd, layer)
    ys = routed_experts(xs, block_expert, n_used, counts, w_gate_all, w_up_all, w_down_all, layer)
    return combine(slots, ys, w_col, shared, x, gate_tiles, ln_w_all, ln_b_all, layer, alpha)


def kernel(x, c, ctx, c_ctx, ada_w, ada_b, w_in, na_rpb, w_o_na, w_fourier, ret_decay_fwd, ret_decay_bwd,
           ret_gn_w, w_o_ret, w_out, ln_mix_w, ln_mix_b, router_w, router_bias, exp_w_gate, exp_w_up,
           exp_w_down, sh_w_gate, sh_w_up, sh_w_down, ln_ffn_w, ln_ffn_b):
    B, N, D = x.shape
    L = ctx.shape[1]
    depth = ada_w.shape[0]
    alpha = (2.0 * depth) ** 0.25
    rows = N // GRID_COLS
    attn_rows = ATTN_ROWS if (rows % ATTN_ROWS == 0 and rows >= _window_span(ATTN_ROWS)) else 1

    rope_tabs = rope_tables(N)
    cn, sn = (t.astype(BF16) for t in dft_tables(N))
    cl, sl = (t.astype(BF16) for t in dft_tables(L))
    cc, sc = channel_dft_tables()
    zero_state = jnp.zeros((B, RT_HEADS, RT_DK, RT_DV), F32)
    mod_rows = -(-(B + 1) // BF16_ROWS) * BF16_ROWS
    c_rows = jnp.concatenate([c, c_ctx[None, :], jnp.zeros((mod_rows - B - 1, D), F32)], axis=0)
    sg_all, su_all, sd_all = sh_w_gate.astype(BF16), sh_w_up.astype(BF16), sh_w_down.astype(BF16)

    xc = ctx
    for l in range(depth):
        update_ctx = l < depth - 1
        mod = ada_mod(c_rows, ada_w, ada_b, l)
        sh1, sc1, g1, sh2, sc2, g2 = (m[:, None, :] for m in jnp.split(mod[:B], 6, axis=-1))
        mod_c = jnp.broadcast_to(mod[B:B + 1], (B, 6 * D))
        sh1c, sc1c, g1c, sh2c, sc2c, g2c = (m[:, None, :] for m in jnp.split(mod_c, 6, axis=-1))
        lg_f = jax.nn.log_sigmoid(ret_decay_fwd[l].astype(F32))
        lg_b = jax.nn.log_sigmoid(ret_decay_bwd[l].astype(F32))

        w_in_b = w_in[l].astype(BF16)
        proj = ln_proj(x, sh1, sc1, w_in_b)
        proj_c = ln_proj(xc, sh1c, sc1c, w_in_b)

        y_na = window_attention(proj, proj_c, *window_bias_tables(na_rpb[l], rows, attn_rows))
        y_fn = (fourier_mix_half if N % (2 * FOURIER_TILE) == 0 else fourier_mix)(proj, cn, sn, cc, sc)
        ob_c, s_b = retention(proj_c, lg_b, zero_state, backward=True)
        if update_ctx:
            yrt_c, s_f = retention(proj_c, lg_f, zero_state, backward=False, finish=(ob_c, ret_gn_w, l))
        else:
            _, s_f = retention(proj_c, lg_f, zero_state, backward=False)
        ob, _ = retention(proj, lg_b, s_b, backward=True, rope_tabs=rope_tabs)
        y_ret, _ = retention(proj, lg_f, s_f, backward=False, rope_tabs=rope_tabs,
                             finish=(ob, ret_gn_w, l))

        wts = (w_o_na[l].astype(BF16), w_fourier[l].astype(BF16), w_o_ret[l].astype(BF16),
               w_out[l].astype(BF16))
        x, h = merge_branches(y_na, y_fn, y_ret, proj, *wts, x, g1, ln_mix_w, ln_mix_b, l, sh2, sc2, alpha)
        h_all = h.reshape(B * N, D)
        x_all = x.reshape(B * N, D)
        gate_tiles = jnp.repeat(g2, N // TOKEN_TILE, axis=0)
        if update_ctx:
            yna_c = context_attention(proj_c)
            yfn_c = fourier_mix(proj_c, cl, sl, cc, sc)
            xc, h_c = merge_branches(yna_c, yfn_c, yrt_c, proj_c, *wts, xc, g1c, ln_mix_w, ln_mix_b, l,
                                     sh2c, sc2c, alpha)
            h_all = jnp.concatenate([h_all, h_c.reshape(B * L, D)], axis=0)
            x_all = jnp.concatenate([x_all, xc.reshape(B * L, D)], axis=0)
            gate_tiles = jnp.concatenate([gate_tiles, jnp.repeat(g2c, L // TOKEN_TILE, axis=0)], axis=0)

        out = moe_sublayer(h_all, x_all, gate_tiles, router_w[l].T.astype(BF16), router_bias[l],
                           exp_w_gate, exp_w_up, exp_w_down, sg_all, su_all, sd_all,
                           ln_ffn_w, ln_ffn_b, l, alpha)
        x = out[:B * N].reshape(B, N, D)
        if update_ctx:
            xc = out[B * N:].reshape(B, L, D)
    return x
```

```python
import functools
import math

import numpy as np
import jax
import jax.numpy as jnp
from jax import lax
from jax.experimental import pallas as pl
from jax.experimental.pallas import tpu as pltpu

F32 = jnp.float32
BF16 = jnp.bfloat16
U32 = jnp.uint32

GRID_COLS = 64
NA_HEADS = 8
NA_DH = 64
NA_W = NA_HEADS * NA_DH
NA_ROWS = 8
NA_COLS = 16
FN_GROUPS = 4
FN_GD = 128
FN_W = FN_GROUPS * FN_GD
RT_HEADS = 4
RT_DK = 128
RT_DV = 256
RT_CHUNK = 128
ROPE_THETA = 10000.0
N_EXP = 256
TOPK = 8
N_GRP = 8
TOPK_GRP = 4
GRP_SZ = N_EXP // N_GRP
ROUTED_SCALE = 2.5
LN_EPS = 1e-6
GN_EPS = 1e-5

COL_QA, COL_KA, COL_VA, COL_U, COL_QR, COL_KR, COL_VR, COL_GR, COL_GL = (
    0, 512, 1024, 1536, 2048, 2560, 3072, 4096, 5120)

VMEM_LIMIT = 56 * 1024 * 1024
LANES = 128
SUBLANES = 8
BF16_ROWS = 16
MXU_ROWS = 256
EXPERT_BLOCK = 2 * MXU_ROWS
TOKEN_TILE = 256
ATTN_ROWS = 4
FOURIER_TILE = 512
NEG = -1e30


def _cparams(sem):
    return pltpu.CompilerParams(dimension_semantics=sem, vmem_limit_bytes=VMEM_LIMIT)


def _ln(x):
    mu = jnp.mean(x, axis=-1, keepdims=True)
    xc = x - mu
    var = jnp.mean(xc * xc, axis=-1, keepdims=True)
    return xc * lax.rsqrt(var + LN_EPS)


def _silu(x):
    return x * jax.nn.sigmoid(x)


def _ada_kernel(c_ref, w_ref, b_ref, o_ref):
    a = _silu(c_ref[...]).astype(BF16)
    o_ref[...] = jnp.dot(a, w_ref[0].astype(BF16), preferred_element_type=F32) + b_ref[0]


def ada_mod(c_rows, w_all, b_all, layer):
    R, D = c_rows.shape
    W = w_all.shape[2]
    tn = 1024
    return pl.pallas_call(
        _ada_kernel,
        out_shape=jax.ShapeDtypeStruct((R, W), F32),
        grid=(W // tn,),
        in_specs=[pl.BlockSpec((R, D), lambda j: (0, 0)),
                  pl.BlockSpec((1, D, tn), lambda j: (layer, 0, j)),
                  pl.BlockSpec((1, 1, tn), lambda j: (layer, 0, j))],
        out_specs=pl.BlockSpec((R, tn), lambda j: (0, j)),
        compiler_params=_cparams(("parallel",)),
        name="ada_mod",
    )(c_rows, w_all, b_all.reshape(b_all.shape[0], 1, W))


def _ln_proj_kernel(x_ref, sh_ref, sc_ref, w_ref, o_ref, h_scr):
    @pl.when(pl.program_id(2) == 0)
    def _():
        h = _ln(x_ref[...]) * (1.0 + sc_ref[0]) + sh_ref[0]
        h_scr[...] = h.astype(BF16)

    o_ref[0] = jnp.dot(h_scr[...], w_ref[...], preferred_element_type=F32).astype(o_ref.dtype)


def ln_proj(x_rows, B, N, shift, scale, w_bf16):
    D = x_rows.shape[1]
    W = w_bf16.shape[1]
    tm = min(N, 1024)
    tn = 2048
    return pl.pallas_call(
        _ln_proj_kernel,
        out_shape=jax.ShapeDtypeStruct((B, N, W), BF16),
        grid=(B, N // tm, W // tn),
        in_specs=[pl.BlockSpec((tm, D), lambda b, i, j: (b * (N // tm) + i, 0)),
                  pl.BlockSpec((1, 1, D), lambda b, i, j: (b, 0, 0)),
                  pl.BlockSpec((1, 1, D), lambda b, i, j: (b, 0, 0)),
                  pl.BlockSpec((D, tn), lambda b, i, j: (0, j))],
        out_specs=pl.BlockSpec((1, tm, tn), lambda b, i, j: (b, i, j)),
        scratch_shapes=[pltpu.VMEM((tm, D), BF16)],
        compiler_params=_cparams(("parallel", "parallel", "arbitrary")),
        name="ln_proj",
    )(x_rows, shift, scale, w_bf16)


def _softmax_pv(s_parts, v):
    m = jnp.max(s_parts[0], axis=-1, keepdims=True)
    for s in s_parts[1:]:
        m = jnp.maximum(m, jnp.max(s, axis=-1, keepdims=True))
    ps = [jnp.exp(s - m) for s in s_parts]
    l = jnp.sum(ps[0], axis=-1, keepdims=True)
    for p in ps[1:]:
        l = l + jnp.sum(p, axis=-1, keepdims=True)
    p = ps[0] if len(ps) == 1 else jnp.concatenate(ps, axis=-1)
    return jnp.dot(p.astype(BF16), v, preferred_element_type=F32) / l


def _window_span(rpb):
    return NA_ROWS + rpb - 1


def _window_attn_kernel(pat_ref, q_ref, k_ref, v_ref, kc_ref, vc_ref, bias_ref, o_ref, kcat, vcat, *,
                        rows, rpb):
    del pat_ref
    L = kc_ref.shape[1]
    span = _window_span(rpb)
    win = span * GRID_COLS
    i = pl.program_id(1)

    @pl.when(i == 0)
    def _():
        kcat[0:L, :] = kc_ref[0]
        vcat[0:L, :] = vc_ref[0]

    first = jnp.clip(i * rpb - NA_ROWS // 2, 0, rows - span)
    start = pl.multiple_of(first * GRID_COLS, GRID_COLS)
    kcat[L:L + win, :] = k_ref[0, pl.ds(start, win), :]
    vcat[L:L + win, :] = v_ref[0, pl.ds(start, win), :]
    q = q_ref[0] * (NA_DH ** -0.5)
    dn = (((1,), (1,)), ((), ()))
    outs = []
    for h in range(NA_HEADS):
        hs = slice(h * NA_DH, (h + 1) * NA_DH)
        s = lax.dot_general(q[:, hs], kcat[:, hs], dn, preferred_element_type=F32)
        outs.append(_softmax_pv([s[:, :L], s[:, L:] + bias_ref[0, h]], vcat[:, hs]))
    o_ref[0] = jnp.concatenate(outs, axis=-1).astype(o_ref.dtype)


def _ctx_attn_kernel(q_ref, kc_ref, vc_ref, o_ref):
    q = q_ref[0] * (NA_DH ** -0.5)
    kc = kc_ref[0]
    vc = vc_ref[0]
    dn = (((1,), (1,)), ((), ()))
    outs = []
    for h in range(NA_HEADS):
        hs = slice(h * NA_DH, (h + 1) * NA_DH)
        s = lax.dot_general(q[:, hs], kc[:, hs], dn, preferred_element_type=F32)
        outs.append(_softmax_pv([s], vc[:, hs]))
    o_ref[0] = jnp.concatenate(outs, axis=-1).astype(o_ref.dtype)


def window_bias_tables(rpb_table, rows, rpb):
    H = rpb_table.shape[0]
    span = _window_span(rpb)
    cidx = np.arange(GRID_COLS)
    c0 = np.clip(cidx - NA_COLS // 2, 0, GRID_COLS - NA_COLS)
    col_ok = (cidx[None, :] >= c0[:, None]) & (cidx[None, :] < c0[:, None] + NA_COLS)
    dc = np.clip(cidx[None, :] - cidx[:, None], -(NA_COLS - 1), NA_COLS - 1) + (NA_COLS - 1)
    col_sel = (dc[:, :, None] == np.arange(2 * NA_COLS - 1)).astype(np.float32)

    keys, patterns, index = [], [], []
    for blk in range(rows // rpb):
        first = int(np.clip(blk * rpb - NA_ROWS // 2, 0, rows - span))
        key_row = first + np.arange(span)
        r = blk * rpb + np.arange(rpb)
        r0 = np.clip(r - NA_ROWS // 2, 0, rows - NA_ROWS)
        assert first <= r0.min() and r0.max() + NA_ROWS <= first + span
        valid = (key_row[None, :] >= r0[:, None]) & (key_row[None, :] < r0[:, None] + NA_ROWS)
        dr = np.where(valid, key_row[None, :] - r[:, None] + NA_ROWS - 1, -1)
        if dr.tobytes() not in keys:
            keys.append(dr.tobytes())
            patterns.append(dr)
        index.append(keys.index(dr.tobytes()))
    dr = np.stack(patterns)
    row_sel = (dr[..., None] == np.arange(2 * NA_ROWS - 1)).astype(np.float32)
    t = jnp.einsum('hab,prja,qkb->phrqjk', rpb_table.astype(F32), row_sel, col_sel,
                   precision=lax.Precision.HIGHEST)
    ok = (dr >= 0)[:, None, :, None, :, None] & col_ok[None, None, None, :, None, :]
    t = jnp.where(ok, t, NEG)
    tabs = t.reshape(len(patterns), H, rpb * GRID_COLS, span * GRID_COLS)
    return tabs, jnp.asarray(np.array(index, np.int32))


def window_attention(proj, proj_ctx, bias_tabs, bias_index):
    B, N, _ = proj.shape
    L = proj_ctx.shape[1]
    rows = N // GRID_COLS
    tq = bias_tabs.shape[2]
    rpb = tq // GRID_COLS
    win = _window_span(rpb) * GRID_COLS
    cb = lambda off: off // NA_W
    grid_spec = pltpu.PrefetchScalarGridSpec(
        num_scalar_prefetch=1,
        grid=(B, rows // rpb),
        in_specs=[pl.BlockSpec((1, tq, NA_W), lambda b, i, pat: (b, i, cb(COL_QA))),
                  pl.BlockSpec((1, N, NA_W), lambda b, i, pat: (b, 0, cb(COL_KA))),
                  pl.BlockSpec((1, N, NA_W), lambda b, i, pat: (b, 0, cb(COL_VA))),
                  pl.BlockSpec((1, L, NA_W), lambda b, i, pat: (b, 0, cb(COL_KA))),
                  pl.BlockSpec((1, L, NA_W), lambda b, i, pat: (b, 0, cb(COL_VA))),
                  pl.BlockSpec((1, NA_HEADS, tq, win), lambda b, i, pat: (pat[i], 0, 0, 0))],
        out_specs=pl.BlockSpec((1, tq, NA_W), lambda b, i, pat: (b, i, 0)),
        scratch_shapes=[pltpu.VMEM((L + win, NA_W), BF16), pltpu.VMEM((L + win, NA_W), BF16)],
    )
    return pl.pallas_call(
        functools.partial(_window_attn_kernel, rows=rows, rpb=rpb),
        out_shape=jax.ShapeDtypeStruct((B, N, NA_W), BF16),
        grid_spec=grid_spec,
        compiler_params=_cparams(("parallel", "arbitrary")),
        name="window_attention",
    )(bias_index, proj, proj, proj, proj_ctx, proj_ctx, bias_tabs)


def context_attention(proj_ctx):
    B, L, _ = proj_ctx.shape
    cb = lambda off: off // NA_W
    tq = min(L, 128)
    return pl.pallas_call(
        _ctx_attn_kernel,
        out_shape=jax.ShapeDtypeStruct((B, L, NA_W), BF16),
        grid=(B, L // tq),
        in_specs=[pl.BlockSpec((1, tq, NA_W), lambda b, r: (b, r, cb(COL_QA))),
                  pl.BlockSpec((1, L, NA_W), lambda b, r: (b, 0, cb(COL_KA))),
                  pl.BlockSpec((1, L, NA_W), lambda b, r: (b, 0, cb(COL_VA)))],
        out_specs=pl.BlockSpec((1, tq, NA_W), lambda b, r: (b, r, 0)),
        compiler_params=_cparams(("parallel", "arbitrary")),
        name="context_attention",
    )(proj_ctx, proj_ctx, proj_ctx)


def _dft_kernel(cn_ref, sn_ref, u_ref, cc_ref, sc_ref, o_ref, *, norm):
    u = u_ref[0]
    a = jnp.dot(cn_ref[...], u, preferred_element_type=F32).astype(BF16)
    b = jnp.dot(sn_ref[...], u, preferred_element_type=F32).astype(BF16)
    y = (jnp.dot(a, cc_ref[...], preferred_element_type=F32)
         - jnp.dot(b, sc_ref[...], preferred_element_type=F32))
    o_ref[0] = (y * norm).astype(o_ref.dtype)


def _unit_circle(rows, cols, period):
    ang = ((rows[:, None] * cols[None, :]) % period).astype(F32) * (2.0 * math.pi / period)
    return jnp.cos(ang), jnp.sin(ang)


def dft_tables(n):
    m = jnp.arange(n, dtype=jnp.int32)
    split = GRID_COLS
    if n <= split * split // 16 or n % split:
        return _unit_circle(m, m, n)
    hi_c, hi_s = _unit_circle(jnp.arange(n // split, dtype=jnp.int32), m, n // split)
    lo_c, lo_s = _unit_circle(jnp.arange(split, dtype=jnp.int32), m, n)
    cos = hi_c[:, None, :] * lo_c[None, :, :] - hi_s[:, None, :] * lo_s[None, :, :]
    sin = hi_s[:, None, :] * lo_c[None, :, :] + hi_c[:, None, :] * lo_s[None, :, :]
    return cos.reshape(n, n), sin.reshape(n, n)


def channel_dft_tables():
    c, s = dft_tables(FN_GD)
    eye = jnp.eye(FN_GROUPS, dtype=F32)
    return jnp.kron(eye, c).astype(BF16), jnp.kron(eye, s).astype(BF16)


def fourier_mix(proj, cn, sn, cc, sc):
    B, N, _ = proj.shape
    tk = min(N, FOURIER_TILE)
    norm = 1.0 / math.sqrt(N * FN_GD)
    return pl.pallas_call(
        functools.partial(_dft_kernel, norm=norm),
        out_shape=jax.ShapeDtypeStruct((B, N, FN_W), BF16),
        grid=(N // tk, B),
        in_specs=[pl.BlockSpec((tk, N), lambda i, b: (i, 0)),
                  pl.BlockSpec((tk, N), lambda i, b: (i, 0)),
                  pl.BlockSpec((1, N, FN_W), lambda i, b: (b, 0, COL_U // FN_W)),
                  pl.BlockSpec((FN_W, FN_W), lambda i, b: (0, 0)),
                  pl.BlockSpec((FN_W, FN_W), lambda i, b: (0, 0))],
        out_specs=pl.BlockSpec((1, tk, FN_W), lambda i, b: (b, i, 0)),
        compiler_params=_cparams(("parallel", "arbitrary")),
        name="fourier_mix",
    )(cn, sn, proj, cc, sc)


def _dft_half_kernel(cn_ref, sn_ref, cn1_ref, sn1_ref, u_ref, cc_ref, sc_ref, flip_ref, lo_ref, hi_ref, *,
                     norm):
    u = u_ref[0]

    def pq(c_rows, s_rows):
        a = jnp.dot(c_rows, u, preferred_element_type=F32).astype(BF16)
        b = jnp.dot(s_rows, u, preferred_element_type=F32).astype(BF16)
        return (jnp.dot(a, cc_ref[...], preferred_element_type=F32),
                jnp.dot(b, sc_ref[...], preferred_element_type=F32))

    p, q = pq(cn_ref[...], sn_ref[...])
    lo_ref[0] = ((p - q) * norm).astype(lo_ref.dtype)
    m = ((p + q) * norm).astype(BF16)
    mirrored = jnp.dot(flip_ref[...], m, preferred_element_type=F32)
    p1, q1 = pq(cn1_ref[...], sn1_ref[...])
    first = ((p1 + q1) * norm)[0:1]
    row = lax.broadcasted_iota(jnp.int32, mirrored.shape, 0)
    hi_ref[0] = jnp.where(row == 0, first, mirrored).astype(hi_ref.dtype)


def fourier_mix_half(proj, cn, sn, cc, sc):
    B, N, _ = proj.shape
    tk = FOURIER_TILE
    nt = N // (2 * tk)
    norm = 1.0 / math.sqrt(N * FN_GD)
    r = jnp.arange(tk, dtype=jnp.int32)
    flip = ((r[None, :] == tk - r[:, None]) & (r[:, None] >= 1)).astype(BF16)
    nxt = lambda t, b: ((t + 1) * (tk // BF16_ROWS), 0)
    half = jax.ShapeDtypeStruct((B, N // 2, FN_W), BF16)
    full = lambda a: pl.BlockSpec(a.shape, lambda t, b: (0,) * a.ndim)
    lo, hi = pl.pallas_call(
        functools.partial(_dft_half_kernel, norm=norm),
        out_shape=(half, half),
        grid=(nt, B),
        in_specs=[pl.BlockSpec((tk, N), lambda t, b: (t, 0)),
                  pl.BlockSpec((tk, N), lambda t, b: (t, 0)),
                  pl.BlockSpec((BF16_ROWS, N), nxt),
                  pl.BlockSpec((BF16_ROWS, N), nxt),
                  pl.BlockSpec((1, N, FN_W), lambda t, b: (b, 0, COL_U // FN_W)),
                  full(cc), full(sc), full(flip)],
        out_specs=(pl.BlockSpec((1, tk, FN_W), lambda t, b: (b, t, 0)),
                   pl.BlockSpec((1, tk, FN_W), lambda t, b: (b, nt - 1 - t, 0))),
        compiler_params=_cparams(("parallel", "arbitrary")),
        name="fourier_mix_half",
    )(cn, sn, cn, sn, proj, cc, sc, flip)
    return jnp.concatenate([lo, hi], axis=1)


def _retention_kernel(*refs, backward, rope, finish, cpb):
    it = iter(refs)
    lg_ref = next(it)
    q_ref, k_ref, v_ref = next(it), next(it), next(it)
    if rope:
        cos_ref, sin_ref, perm_ref = next(it), next(it), next(it)
    s0_ref = next(it)
    if finish:
        ob_ref, gate_ref, gn_ref = next(it), next(it), next(it)
    o_ref, sf_ref = next(it), next(it)
    state = next(it)

    C = RT_CHUNK
    step = pl.program_id(1)

    @pl.when(step == 0)
    def _():
        state[...] = s0_ref[...]

    pos_r = lax.broadcasted_iota(jnp.int32, (C, 1), 0).astype(F32)
    ci = lax.broadcasted_iota(jnp.int32, (C, C), 0)
    mi = lax.broadcasted_iota(jnp.int32, (C, C), 1)
    if backward:
        dist = jnp.maximum(mi - ci, 0).astype(F32)
        band = mi > ci
    else:
        dist = jnp.maximum(ci - mi, 0).astype(F32)
        band = ci >= mi
    heads = []
    for h in range(RT_HEADS):
        lg = lg_ref[h]
        if backward:
            k_dec = jnp.exp(lg * pos_r)
            q_dec = jnp.exp(lg * (C - pos_r))
        else:
            k_dec = jnp.exp(lg * (C - 1 - pos_r))
            q_dec = jnp.exp(lg * (pos_r + 1))
        heads.append((k_dec, q_dec, jnp.where(band, jnp.exp(lg * dist), 0.0), jnp.exp(lg * C)))
    if rope:
        cos = jnp.concatenate([cos_ref[...]] * RT_HEADS, axis=-1)
        sin = jnp.concatenate([sin_ref[...]] * RT_HEADS, axis=-1)
        perm = perm_ref[...]

    order = range(cpb - 1, -1, -1) if backward else range(cpb)
    for bi in range(q_ref.shape[0]):
        q = q_ref[bi]
        k = k_ref[bi]
        if rope:
            qf = q.astype(F32) * cos + jnp.dot(q, perm, preferred_element_type=F32) * sin
            kf = k.astype(F32) * cos + jnp.dot(k, perm, preferred_element_type=F32) * sin
        else:
            qf = q.astype(F32)
            kf = k.astype(F32)
        qf = qf * (RT_DK ** -0.5)
        for h in range(RT_HEADS):
            k_dec, q_dec, decay, chunk_dec = heads[h]
            kcol = slice(h * RT_DK, (h + 1) * RT_DK)
            vcol = slice(h * RT_DV, (h + 1) * RT_DV)
            s_prev = state[bi, h]
            for c in order:
                sl = slice(c * C, (c + 1) * C)
                qc = qf[sl, kcol]
                kc = kf[sl, kcol]
                vc = v_ref[bi, sl, vcol]
                scores = lax.dot_general(qc.astype(BF16), kc.astype(BF16), (((1,), (1,)), ((), ())),
                                         preferred_element_type=F32) * decay
                lhs = jnp.concatenate([scores, qc * q_dec], axis=-1).astype(BF16)
                rhs = jnp.concatenate([vc, s_prev.astype(BF16)], axis=0)
                o = jnp.dot(lhs, rhs, preferred_element_type=F32)
                kv = lax.dot_general((kc * k_dec).astype(BF16), vc, (((0,), (0,)), ((), ())),
                                     preferred_element_type=F32)
                s_prev = chunk_dec * s_prev + kv
                if finish:
                    o = o + ob_ref[bi, sl, vcol]
                    mu = jnp.mean(o, axis=-1, keepdims=True)
                    oc = o - mu
                    var = jnp.mean(oc * oc, axis=-1, keepdims=True)
                    y = oc * lax.rsqrt(var + GN_EPS) * gn_ref[0, :, vcol]
                    o = y * _silu(gate_ref[bi, sl, vcol].astype(F32))
                o_ref[bi, sl, vcol] = o.astype(o_ref.dtype)
            state[bi, h] = s_prev

    @pl.when(step == pl.num_programs(1) - 1)
    def _():
        sf_ref[...] = state[...]


def rope_tables(n):
    t = jnp.arange(n)
    row, col = t // GRID_COLS, t % GRID_COLS
    half = RT_DK // 2
    n_pairs = half // 2
    inv_freq = ROPE_THETA ** (-jnp.arange(n_pairs, dtype=F32) / n_pairs)

    def cs(pos):
        ang = pos.astype(F32)[:, None] * inv_freq[None, :]
        c, s = jnp.cos(ang), jnp.sin(ang)
        return jnp.concatenate([c, c], axis=-1), jnp.concatenate([-s, s], axis=-1)

    cr, sr = cs(row)
    cc, sc = cs(col)
    cos = jnp.concatenate([cr, cc], axis=-1)
    sin = jnp.concatenate([sr, sc], axis=-1)
    idx = jnp.arange(RT_DK)
    src = (idx // half) * half + (idx % half + n_pairs) % half
    perm = (idx[:, None] == src[None, :]).astype(F32)
    return cos, sin, jnp.kron(jnp.eye(RT_HEADS, dtype=F32), perm).astype(BF16)


def retention(proj, log_g, s0, *, backward, rope_tabs=None, finish=None):
    B, N, _ = proj.shape
    nc = N // RT_CHUNK
    cpb = min(nc, 4)
    tb = cpb * RT_CHUNK
    nsteps = nc // cpb
    bpb = 1
    blk = (lambda s: nsteps - 1 - s) if backward else (lambda s: s)
    rope = rope_tabs is not None
    fin = finish is not None

    qk_w = RT_HEADS * RT_DK
    v_w = RT_HEADS * RT_DV
    state_spec = pl.BlockSpec((bpb, RT_HEADS, RT_DK, RT_DV), lambda b, s: (b, 0, 0, 0))
    in_specs = [pl.BlockSpec(memory_space=pltpu.SMEM),
                pl.BlockSpec((bpb, tb, qk_w), lambda b, s: (b, blk(s), COL_QR // qk_w)),
                pl.BlockSpec((bpb, tb, qk_w), lambda b, s: (b, blk(s), COL_KR // qk_w)),
                pl.BlockSpec((bpb, tb, v_w), lambda b, s: (b, blk(s), COL_VR // v_w))]
    args = [log_g, proj, proj, proj]
    if rope:
        in_specs += [pl.BlockSpec((tb, RT_DK), lambda b, s: (blk(s), 0)),
                     pl.BlockSpec((tb, RT_DK), lambda b, s: (blk(s), 0)),
                     pl.BlockSpec((qk_w, qk_w), lambda b, s: (0, 0))]
        args += list(rope_tabs)
    in_specs.append(state_spec)
    args.append(s0)
    if fin:
        o_b, gn_all, layer = finish
        in_specs += [pl.BlockSpec((bpb, tb, v_w), lambda b, s: (b, blk(s), 0)),
                     pl.BlockSpec((bpb, tb, v_w), lambda b, s: (b, blk(s), COL_GR // v_w)),
                     pl.BlockSpec((1, 1, v_w), lambda b, s: (layer, 0, 0))]
        args += [o_b, proj, gn_all.reshape(gn_all.shape[0], 1, -1)]
    out_dtype = BF16 if fin else F32
    o, s_fin = pl.pallas_call(
        functools.partial(_retention_kernel, backward=backward, rope=rope, finish=fin, cpb=cpb),
        out_shape=(jax.ShapeDtypeStruct((B, N, v_w), out_dtype),
                   jax.ShapeDtypeStruct((B, RT_HEADS, RT_DK, RT_DV), F32)),
        grid=(B // bpb, nsteps),
        in_specs=in_specs,
        out_specs=(pl.BlockSpec((bpb, tb, v_w), lambda b, s: (b, blk(s), 0)), state_spec),
        scratch_shapes=[pltpu.VMEM((bpb, RT_HEADS, RT_DK, RT_DV), F32)],
        compiler_params=_cparams(("parallel", "arbitrary")),
        name="retention_bwd" if backward else "retention_fwd",
    )(*args)
    return o, s_fin


def _merge_kernel(yna_ref, yfn_ref, yrt_ref, g0_ref, g1_ref, g2_ref, wna_ref, wfn_ref, wrt_ref,
                  wout_ref, x_ref, gate_ref, lnw_ref, lnb_ref, sh_ref, sc_ref, xo_ref, h_ref, *, alpha, n_real):
    real = pl.program_id(0) < n_real

    @pl.when(real)
    def _():
        a = jnp.dot(yna_ref[0], wna_ref[...], preferred_element_type=F32)
        y = jax.nn.sigmoid(g0_ref[0].astype(F32)) * a
        b = jnp.dot(yfn_ref[0], wfn_ref[...], preferred_element_type=F32)
        y = y + jax.nn.sigmoid(g1_ref[0].astype(F32)) * b
        c = jnp.dot(yrt_ref[0], wrt_ref[...], preferred_element_type=F32)
        y = y + jax.nn.sigmoid(g2_ref[0].astype(F32)) * c
        y = jnp.dot(y.astype(BF16), wout_ref[...], preferred_element_type=F32)
        xn = _ln(alpha * x_ref[...] + gate_ref[0] * y) * lnw_ref[0] + lnb_ref[0]
        xo_ref[...] = xn
        h_ref[...] = _ln(xn) * (1.0 + sc_ref[0]) + sh_ref[0]

    @pl.when(jnp.logical_not(real))
    def _():
        xo_ref[...] = jnp.zeros_like(xo_ref)
        h_ref[...] = jnp.zeros_like(h_ref)


def merge_branches(y_na, y_fn, y_ret, proj, w_na, w_fn, w_rt, w_out, x_rows, gate, ln_w_all, ln_b_all, layer,
                   shift2, scale2, alpha, extra_rows=0):
    B, N, _ = y_na.shape
    D = x_rows.shape[1]
    tm = math.gcd(min(N, 512), extra_rows) if extra_rows else min(N, 512)
    nt = N // tm
    n_real = B * nt
    bi = lambda g: (jnp.minimum(g, n_real - 1) // nt, jnp.minimum(g, n_real - 1) % nt)
    row = lambda w: pl.BlockSpec((1, tm, w), lambda g: (*bi(g), 0))
    glb = lambda j: pl.BlockSpec((1, tm, D), lambda g: (*bi(g), COL_GL // D + j))
    full = lambda a: pl.BlockSpec(a.shape, lambda g: (0,) * a.ndim)
    vec = pl.BlockSpec((1, 1, D), lambda g: (bi(g)[0], 0, 0))
    prm = pl.BlockSpec((1, 1, D), lambda g: (layer, 0, 0))
    flat = pl.BlockSpec((tm, D), lambda g: (g, 0))
    depth = ln_w_all.shape[0]
    out = jax.ShapeDtypeStruct((B * N + extra_rows, D), F32)
    return pl.pallas_call(
        functools.partial(_merge_kernel, alpha=alpha, n_real=n_real),
        out_shape=(out, out),
        grid=(n_real + extra_rows // tm,),
        in_specs=[row(NA_W), row(FN_W), row(RT_HEADS * RT_DV), glb(0), glb(1), glb(2),
                  full(w_na), full(w_fn), full(w_rt), full(w_out),
                  pl.BlockSpec((tm, D), lambda g: (jnp.minimum(g, n_real - 1), 0)), vec, prm, prm, vec, vec],
        out_specs=(flat, flat),
        compiler_params=_cparams(("parallel",)),
        name="merge_branches",
    )(y_na, y_fn, y_ret, proj, proj, proj, w_na, w_fn, w_rt, w_out, x_rows, gate,
      ln_w_all.reshape(depth, 1, D), ln_b_all.reshape(depth, 1, D), shift2, scale2)


def _router_kernel(h_ref, w_ref, b_ref, e_ref, wcol_ref, rk_ref, cnt_ref, cnt_scr):
    tm = h_ref.shape[0]

    @pl.when(pl.program_id(0) == 0)
    def _():
        cnt_scr[...] = jnp.zeros_like(cnt_scr)

    logits = lax.dot_general(w_ref[...], h_ref[...].astype(BF16), (((1,), (1,)), ((), ())),
                             preferred_element_type=F32)
    scores = jax.nn.sigmoid(logits)
    sel = scores + b_ref[...]
    row = lax.broadcasted_iota(jnp.int32, (N_EXP, tm), 0)

    def first_max(vals, rows):
        m = jnp.max(vals, axis=0, keepdims=True)
        idx = jnp.min(jnp.where(vals == m, rows, N_EXP), axis=0, keepdims=True)
        return m, idx

    gscore = []
    grow = lax.broadcasted_iota(jnp.int32, (GRP_SZ, tm), 0)
    for g in range(N_GRP):
        gs = slice(g * GRP_SZ, (g + 1) * GRP_SZ)
        vals, rows = sel[gs], grow + g * GRP_SZ
        m1, i1 = first_max(vals, rows)
        m2, _ = first_max(jnp.where(rows == i1, -jnp.inf, vals), rows)
        gscore.append(m1 + m2)
    cand = []
    for g in range(N_GRP):
        beaten = jnp.zeros((1, tm), jnp.int32)
        for o in range(N_GRP):
            if o == g:
                continue
            ahead = (gscore[o] >= gscore[g]) if o < g else (gscore[o] > gscore[g])
            beaten = beaten + ahead.astype(jnp.int32)
        gs = slice(g * GRP_SZ, (g + 1) * GRP_SZ)
        cand.append(jnp.where(beaten < TOPK_GRP, sel[gs], -jnp.inf))
    cand = jnp.concatenate(cand, axis=0)

    chosen = jnp.zeros((N_EXP, tm), F32)
    ids, wts = [], []
    for _ in range(TOPK):
        _, idx = first_max(cand, row)
        hit = row == idx
        ids.append(idx)
        wts.append(jnp.sum(jnp.where(hit, scores, 0.0), axis=0, keepdims=True))
        chosen = jnp.where(hit, 1.0, chosen)
        cand = jnp.where(hit, -jnp.inf, cand)
    wsum = wts[0]
    for w in wts[1:]:
        wsum = wsum + w

    ti = lax.broadcasted_iota(jnp.int32, (tm, tm), 0)
    tj = lax.broadcasted_iota(jnp.int32, (tm, tm), 1)
    before = (ti < tj).astype(BF16)
    rank_full = jnp.dot(chosen.astype(BF16), before, preferred_element_type=F32) + cnt_scr[...]
    cnt_scr[...] = cnt_scr[...] + jnp.sum(chosen, axis=1, keepdims=True)

    ranks = [jnp.sum(jnp.where(row == ids[k], rank_full, 0.0), axis=0, keepdims=True) for k in range(TOPK)]
    e_ref[...] = jnp.concatenate(ids, axis=0)
    rk_ref[...] = jnp.concatenate(ranks, axis=0).astype(jnp.int32)
    wn = [w / wsum * ROUTED_SCALE for w in wts]
    wpad = jnp.concatenate(wn + [jnp.zeros((LANES - TOPK, tm), F32)], axis=0)
    wcol_ref[...] = wpad.T
    cnt_ref[...] = cnt_scr[...].astype(jnp.int32)


def router(h, w_t_bf16, bias):
    T, D = h.shape
    tm = TOKEN_TILE
    lane_out = pl.BlockSpec((TOPK, tm), lambda i: (0, i))
    return pl.pallas_call(
        _router_kernel,
        out_shape=(jax.ShapeDtypeStruct((TOPK, T), jnp.int32), jax.ShapeDtypeStruct((T, LANES), F32),
                   jax.ShapeDtypeStruct((TOPK, T), jnp.int32), jax.ShapeDtypeStruct((N_EXP, 1), jnp.int32)),
        grid=(T // tm,),
        in_specs=[pl.BlockSpec((tm, D), lambda i: (i, 0)),
                  pl.BlockSpec((N_EXP, D), lambda i: (0, 0)),
                  pl.BlockSpec((N_EXP, 1), lambda i: (0, 0))],
        out_specs=(lane_out, pl.BlockSpec((tm, LANES), lambda i: (i, 0)), lane_out,
                   pl.BlockSpec((N_EXP, 1), lambda i: (0, 0))),
        scratch_shapes=[pltpu.VMEM((N_EXP, 1), F32)],
        compiler_params=_cparams(("arbitrary",)),
        name="router",
    )(h, w_t_bf16, bias.reshape(N_EXP, 1))


def _slot_kernel(e_ref, rk_ref, ps_ref, o_ref):
    tm = e_ref.shape[1]
    row = lax.broadcasted_iota(jnp.int32, (N_EXP, tm), 0)
    ps = ps_ref[...].astype(F32)
    outs = []
    for k in range(TOPK):
        base = jnp.sum(jnp.where(row == e_ref[k:k + 1, :], ps, 0.0), axis=0, keepdims=True)
        outs.append(base.astype(jnp.int32) + rk_ref[k:k + 1, :])
    o_ref[...] = jnp.concatenate(outs, axis=0)


def assign_slots(top_e, rank, pad_start):
    K, T = top_e.shape
    tm = next(t for t in (2048, 1024, 512, 256) if T % t == 0)
    blk = pl.BlockSpec((K, tm), lambda i: (0, i))
    return pl.pallas_call(
        _slot_kernel,
        out_shape=jax.ShapeDtypeStruct((K, T), jnp.int32),
        grid=(T // tm,),
        in_specs=[blk, blk, pl.BlockSpec((N_EXP, 1), lambda i: (0, 0))],
        out_specs=blk,
        compiler_params=_cparams(("parallel",)),
        name="assign_slots",
    )(top_e, rank, pad_start.reshape(N_EXP, 1).astype(jnp.int32))


def _pack_pair(lo, hi):
    a = lax.bitcast_convert_type(lo.astype(BF16).astype(F32), U32) >> 16
    b = lax.bitcast_convert_type(hi.astype(BF16).astype(F32), U32) & jnp.uint32(0xFFFF0000)
    return a | b


def _unpack_pair(w):
    lo = lax.bitcast_convert_type(w << 16, F32)
    hi = lax.bitcast_convert_type(w & jnp.uint32(0xFFFF0000), F32)
    return lo, hi


def _dispatch_kernel(pe_ref, cnt_ref, nu_ref, slot_hbm, h_ref, sg_ref, su_ref, sd_ref, xs_hbm, sh_ref,
                     slot_smem, stage, zbuf, sem_idx, sem):
    tm, D = h_ref.shape
    ch = stage.shape[1]
    i = pl.program_id(0)
    nb = xs_hbm.shape[0] // EXPERT_BLOCK

    @pl.when(i == 0)
    def _():
        zbuf[...] = jnp.zeros_like(zbuf)

        def block_copy(b):
            start = pl.multiple_of(b * EXPERT_BLOCK, EXPERT_BLOCK)
            return pltpu.make_async_copy(zbuf, xs_hbm.at[pl.ds(start, EXPERT_BLOCK)], sem)

        def for_partial_blocks(fn):
            def body(e, carry):
                @pl.when(cnt_ref[e] % EXPERT_BLOCK != 0)
                def _():
                    fn(block_copy(pe_ref[e] // EXPERT_BLOCK - 1))
                return carry
            lax.fori_loop(0, N_EXP, body, 0)

        def for_unused_blocks(fn):
            def body(b, carry):
                fn(block_copy(b))
                return carry
            lax.fori_loop(nu_ref[0], nb, body, 0)

        for_partial_blocks(lambda cp: cp.start())
        for_unused_blocks(lambda cp: cp.start())
        for_partial_blocks(lambda cp: cp.wait())
        for_unused_blocks(lambda cp: cp.wait())

    idx_copy = pltpu.make_async_copy(slot_hbm.at[i], slot_smem, sem_idx)
    idx_copy.start()
    x = h_ref[...]
    half = D // 2
    for j in range(ch):
        stage[:, j, :] = _pack_pair(x[:, j * LANES:(j + 1) * LANES],
                                    x[:, half + j * LANES:half + (j + 1) * LANES])
    idx_copy.wait()

    def issue(t, carry):
        for k in range(TOPK):
            pltpu.make_async_copy(stage.at[t], xs_hbm.at[slot_smem[k * tm + t]], sem).start(priority=k % 2)
        return carry

    lax.fori_loop(0, tm, issue, 0)

    hb = x.astype(BF16)
    g = jnp.dot(hb, sg_ref[0], preferred_element_type=F32)
    u = jnp.dot(hb, su_ref[0], preferred_element_type=F32)
    sh_ref[...] = jnp.dot((_silu(g) * u).astype(BF16), sd_ref[0], preferred_element_type=F32)

    for k in range(TOPK):
        pltpu.make_async_copy(stage, xs_hbm.at[pl.ds(0, tm)], sem).wait()


def dispatch(h, slots, n_slots, pad_end, counts, n_used, sg, su, sd, layer):
    T, D = h.shape
    tm = TOKEN_TILE
    ch = D // 2 // LANES
    shw = lambda a: pl.BlockSpec((1,) + a.shape[1:], lambda i, pe, cnt, nu: (layer, 0, 0))
    grid_spec = pltpu.PrefetchScalarGridSpec(
        num_scalar_prefetch=3,
        grid=(T // tm,),
        in_specs=[pl.BlockSpec(memory_space=pl.ANY),
                  pl.BlockSpec((tm, D), lambda i, pe, cnt, nu: (i, 0)),
                  shw(sg), shw(su), shw(sd)],
        out_specs=(pl.BlockSpec(memory_space=pl.ANY),
                   pl.BlockSpec((tm, D), lambda i, pe, cnt, nu: (i, 0))),
        scratch_shapes=[pltpu.SMEM((TOPK * tm,), jnp.int32),
                        pltpu.VMEM((tm, ch, LANES), U32),
                        pltpu.VMEM((EXPERT_BLOCK, ch, LANES), U32),
                        pltpu.SemaphoreType.DMA(()),
                        pltpu.SemaphoreType.DMA(())],
    )
    return pl.pallas_call(
        _dispatch_kernel,
        out_shape=(jax.ShapeDtypeStruct((n_slots, ch, LANES), U32), jax.ShapeDtypeStruct((T, D), F32)),
        grid_spec=grid_spec,
        compiler_params=_cparams(("arbitrary",)),
        name="moe_dispatch",
    )(pad_end.astype(jnp.int32), counts.astype(jnp.int32), n_used, slots, h, sg, su, sd)


def _expert_kernel(be_ref, nu_ref, par_ref, nxt_ref, xs_ref, wg_hbm, wu_hbm, wd_hbm, ys_ref,
                   wg_f, wu_f, wd_f, wg_s, wu_s, wd_s, sem, *, layer):
    i = pl.program_id(0)
    used = i < nu_ref[0]
    new_expert = jnp.logical_or(i == 0, be_ref[i] != be_ref[jnp.maximum(i - 1, 0)])

    def weight_copies(e, b):
        return (pltpu.make_async_copy(wg_hbm.at[layer, e], wg_f.at[b], sem.at[b]),
                pltpu.make_async_copy(wu_hbm.at[layer, e], wu_f.at[b], sem.at[b]),
                pltpu.make_async_copy(wd_hbm.at[layer, e], wd_f.at[b], sem.at[b]))

    @pl.when(i == 0)
    def _():
        for cp in weight_copies(be_ref[0], 0):
            cp.start()

    @pl.when(jnp.logical_not(used))
    def _():
        ys_ref[...] = jnp.zeros_like(ys_ref)

    @pl.when(jnp.logical_and(used, new_expert))
    def _():
        b = par_ref[i]
        for cp in weight_copies(be_ref[i], b):
            cp.wait()

        @pl.when(nxt_ref[i] >= 0)
        def _():
            for cp in weight_copies(nxt_ref[i], 1 - b):
                cp.start()

        wg_s[...] = wg_f[b].astype(BF16)
        wu_s[...] = wu_f[b].astype(BF16)
        wd_s[...] = wd_f[b].astype(BF16)

    @pl.when(used)
    def _():
        D = wg_s.shape[0]
        ch = D // 2 // LANES
        half = D // 2
        for part in range(xs_ref.shape[0] // (ch * MXU_ROWS)):
            base = part * MXU_ROWS * ch
            pairs = [_unpack_pair(xs_ref[pl.ds(base + j, MXU_ROWS, stride=ch), :]) for j in range(ch)]
            x = jnp.concatenate([lo.astype(BF16) for lo, _ in pairs] + [hi.astype(BF16) for _, hi in pairs],
                                axis=-1)
            g = jnp.dot(x, wg_s[...], preferred_element_type=F32)
            u = jnp.dot(x, wu_s[...], preferred_element_type=F32)
            a = (_silu(g) * u).astype(BF16)
            y = jnp.dot(a, wd_s[...], preferred_element_type=F32)
            for j in range(ch):
                ys_ref[pl.ds(base + j, MXU_ROWS, stride=ch), :] = _pack_pair(
                    y[:, j * LANES:(j + 1) * LANES], y[:, half + j * LANES:half + (j + 1) * LANES])


def routed_experts(xs, block_expert, n_used, counts, w_gate_all, w_up_all, w_down_all, layer):
    n_slots, ch, _ = xs.shape
    _, E, D, Fd = w_gate_all.shape
    nb = n_slots // EXPERT_BLOCK
    rows_blk = EXPERT_BLOCK * ch
    starts = jnp.concatenate([jnp.ones((1,), jnp.int32),
                              (block_expert[1:] != block_expert[:-1]).astype(jnp.int32)])
    parity = ((jnp.cumsum(starts) - 1) % 2).astype(jnp.int32)
    ids = jnp.arange(E, dtype=jnp.int32)
    later = jnp.where((counts[None, :] > 0) & (ids[None, :] > ids[:, None]), ids[None, :], E)
    next_used = jnp.min(later, axis=1)
    next_used = jnp.where(next_used == E, -1, next_used).astype(jnp.int32)
    next_expert = jnp.sum(jnp.where(block_expert[:, None] == ids[None, :], next_used[None, :], 0),
                          axis=1).astype(jnp.int32)
    idx = lambda f: (lambda i, be, nu, par, nxt: f(i, nu))
    grid_spec = pltpu.PrefetchScalarGridSpec(
        num_scalar_prefetch=4,
        grid=(nb,),
        in_specs=[pl.BlockSpec((rows_blk, LANES), idx(lambda i, nu: (jnp.minimum(i, nu[0] - 1), 0))),
                  pl.BlockSpec(memory_space=pl.ANY),
                  pl.BlockSpec(memory_space=pl.ANY),
                  pl.BlockSpec(memory_space=pl.ANY)],
        out_specs=pl.BlockSpec((rows_blk, LANES), idx(lambda i, nu: (i, 0))),
        scratch_shapes=[pltpu.VMEM((2, D, Fd), F32), pltpu.VMEM((2, D, Fd), F32), pltpu.VMEM((2, Fd, D), F32),
                        pltpu.VMEM((D, Fd), BF16), pltpu.VMEM((D, Fd), BF16), pltpu.VMEM((Fd, D), BF16),
                        pltpu.SemaphoreType.DMA((2,))],
    )
    ys = pl.pallas_call(
        functools.partial(_expert_kernel, layer=layer),
        out_shape=jax.ShapeDtypeStruct((n_slots * ch, LANES), U32),
        grid_spec=grid_spec,
        compiler_params=_cparams(("arbitrary",)),
        name="routed_experts",
    )(block_expert, n_used, parity, next_expert, xs.reshape(n_slots * ch, LANES),
      w_gate_all, w_up_all, w_down_all)
    return ys.reshape(n_slots, ch, LANES)


def _combine_kernel(slot_hbm, ys_hbm, wt_ref, sh_ref, x_ref, gate_ref, lnw_ref, lnb_ref, o_ref,
                    slot_smem, buf, acc, sem_idx, sem, *, alpha):
    tm = x_ref.shape[0]
    ch = buf.shape[2]
    i = pl.program_id(0)
    cur = i % 2

    last = pl.num_programs(0) - 1
    half = ch * LANES

    per_tile = TOPK * tm

    def load_slots(tile, b):
        dst = slot_smem.at[pl.ds(pl.multiple_of(b * per_tile, per_tile), per_tile)]
        idx_copy = pltpu.make_async_copy(slot_hbm.at[tile], dst, sem_idx)
        idx_copy.start()
        idx_copy.wait()

    def issue_rows(t, b):
        for k in range(TOPK):
            pltpu.make_async_copy(ys_hbm.at[slot_smem[b * per_tile + k * tm + t]], buf.at[b, k, :, t, :],
                                  sem.at[b]).start(priority=k % 2)

    def wait_rows(b):
        for k in range(TOPK):
            pltpu.make_async_copy(buf.at[b, k], buf.at[b, k], sem.at[b]).wait()

    @pl.when(i == 0)
    def _():
        load_slots(0, 0)

        def first(t, carry):
            issue_rows(t, 0)
            return carry

        lax.fori_loop(0, tm, first, 0)

    nxt = 1 - cur
    load_slots(jnp.minimum(i + 1, last), nxt)
    wait_rows(cur)

    def reduce_and_issue(sb, carry):
        r0 = pl.multiple_of(sb * SUBLANES, SUBLANES)
        rows = pl.ds(r0, SUBLANES)
        for tt in range(SUBLANES):
            issue_rows(r0 + tt, nxt)
        wt = wt_ref[rows, :]
        for j in range(ch):
            lo, hi = _unpack_pair(buf[cur, 0, j, rows, :])
            r_lo, r_hi = wt[:, 0:1] * lo, wt[:, 0:1] * hi
            for k in range(1, TOPK):
                lo, hi = _unpack_pair(buf[cur, k, j, rows, :])
                r_lo, r_hi = r_lo + wt[:, k:k + 1] * lo, r_hi + wt[:, k:k + 1] * hi
            acc[rows, j * LANES:(j + 1) * LANES] = r_lo
            acc[rows, half + j * LANES:half + (j + 1) * LANES] = r_hi
        return carry

    lax.fori_loop(0, tm // SUBLANES, reduce_and_issue, 0)

    f = sh_ref[...] + acc[...]
    o_ref[...] = _ln(alpha * x_ref[...] + gate_ref[0] * f) * lnw_ref[0] + lnb_ref[0]

    @pl.when(i == last)
    def _():
        wait_rows(nxt)


def combine(slots, ys, wt, shared, x, gate_tiles, ln_w_all, ln_b_all, layer, alpha):
    T, D = x.shape
    tm = TOKEN_TILE
    ch = ys.shape[1]
    depth = ln_w_all.shape[0]
    prm = pl.BlockSpec((1, 1, D), lambda i: (layer, 0, 0))
    return pl.pallas_call(
        functools.partial(_combine_kernel, alpha=alpha),
        out_shape=jax.ShapeDtypeStruct((T, D), F32),
        grid=(T // tm,),
        in_specs=[pl.BlockSpec(memory_space=pl.ANY),
                  pl.BlockSpec(memory_space=pl.ANY),
                  pl.BlockSpec((tm, LANES), lambda i: (i, 0)),
                  pl.BlockSpec((tm, D), lambda i: (i, 0)),
                  pl.BlockSpec((tm, D), lambda i: (i, 0)),
                  pl.BlockSpec((1, 1, D), lambda i: (i, 0, 0)),
                  prm, prm],
        out_specs=pl.BlockSpec((tm, D), lambda i: (i, 0)),
        scratch_shapes=[pltpu.SMEM((2 * TOPK * tm,), jnp.int32),
                        pltpu.VMEM((2, TOPK, ch, tm, LANES), U32),
                        pltpu.VMEM((tm, D), F32),
                        pltpu.SemaphoreType.DMA(()),
                        pltpu.SemaphoreType.DMA((2,))],
        compiler_params=_cparams(("arbitrary",)),
        name="moe_combine",
    )(slots, ys, wt, shared, x, gate_tiles, ln_w_all.reshape(depth, 1, D), ln_b_all.reshape(depth, 1, D))


def moe_sublayer(h, x, gate_tiles, router_w_t, router_bias, w_gate_all, w_up_all, w_down_all, sg, su, sd,
                 ln_w_all, ln_b_all, layer, alpha):
    T, D = h.shape
    top_e, w_col, rank, counts = router(h, router_w_t, router_bias)
    counts = counts[:, 0]
    padded = (counts + EXPERT_BLOCK - 1) // EXPERT_BLOCK * EXPERT_BLOCK
    pad_end = jnp.cumsum(padded)
    pad_start = pad_end - padded
    slots = assign_slots(top_e, rank, pad_start)
    n_tiles = T // TOKEN_TILE
    slots = slots.reshape(TOPK, n_tiles, TOKEN_TILE).transpose(1, 0, 2).reshape(n_tiles, TOPK * TOKEN_TILE)
    n_slots = -(-(T * TOPK + N_EXP * (EXPERT_BLOCK - 1)) // EXPERT_BLOCK) * EXPERT_BLOCK
    nb = n_slots // EXPERT_BLOCK
    block_start = jnp.arange(nb, dtype=jnp.int32) * EXPERT_BLOCK
    block_expert = jnp.minimum(jnp.sum(block_start[:, None] >= pad_end[None, :], axis=1),
                               N_EXP - 1).astype(jnp.int32)
    n_used = (pad_end[-1:] // EXPERT_BLOCK).astype(jnp.int32)

    xs, shared = dispatch(h, slots, n_slots, pad_end, counts, n_used, sg, su, sd, layer)
    ys = routed_experts(xs, block_expert, n_used, counts, w_gate_all, w_up_all, w_down_all, layer)
    return combine(slots, ys, w_col, shared, x, gate_tiles, ln_w_all, ln_b_all, layer, alpha)


def kernel(x, c, ctx, c_ctx, ada_w, ada_b, w_in, na_rpb, w_o_na, w_fourier, ret_decay_fwd, ret_decay_bwd,
           ret_gn_w, w_o_ret, w_out, ln_mix_w, ln_mix_b, router_w, router_bias, exp_w_gate, exp_w_up,
           exp_w_down, sh_w_gate, sh_w_up, sh_w_down, ln_ffn_w, ln_ffn_b):
    B, N, D = x.shape
    L = ctx.shape[1]
    depth = ada_w.shape[0]
    alpha = (2.0 * depth) ** 0.25
    rows = N // GRID_COLS
    attn_rows = ATTN_ROWS if (rows % ATTN_ROWS == 0 and rows >= _window_span(ATTN_ROWS)) else 1

    rope_tabs = rope_tables(N)
    cn, sn = (t.astype(BF16) for t in dft_tables(N))
    cl, sl = (t.astype(BF16) for t in dft_tables(L))
    cc, sc = channel_dft_tables()
    zero_state = jnp.zeros((B, RT_HEADS, RT_DK, RT_DV), F32)
    mod_rows = -(-(B + 1) // BF16_ROWS) * BF16_ROWS
    c_rows = jnp.concatenate([c, c_ctx[None, :], jnp.zeros((mod_rows - B - 1, D), F32)], axis=0)
    sg_all, su_all, sd_all = sh_w_gate.astype(BF16), sh_w_up.astype(BF16), sh_w_down.astype(BF16)

    x_rows = x.reshape(B * N, D)
    xc_rows = ctx.reshape(B * L, D)
    for l in range(depth):
        update_ctx = l < depth - 1
        mod = ada_mod(c_rows, ada_w, ada_b, l)
        sh1, sc1, g1, sh2, sc2, g2 = (m[:, None, :] for m in jnp.split(mod[:B], 6, axis=-1))
        mod_c = jnp.broadcast_to(mod[B:B + 1], (B, 6 * D))
        sh1c, sc1c, g1c, sh2c, sc2c, g2c = (m[:, None, :] for m in jnp.split(mod_c, 6, axis=-1))
        lg_f = jax.nn.log_sigmoid(ret_decay_fwd[l].astype(F32))
        lg_b = jax.nn.log_sigmoid(ret_decay_bwd[l].astype(F32))

        w_in_b = w_in[l].astype(BF16)
        proj = ln_proj(x_rows, B, N, sh1, sc1, w_in_b)
        proj_c = ln_proj(xc_rows, B, L, sh1c, sc1c, w_in_b)

        y_na = window_attention(proj, proj_c, *window_bias_tables(na_rpb[l], rows, attn_rows))
        y_fn = (fourier_mix_half if N % (2 * FOURIER_TILE) == 0 else fourier_mix)(proj, cn, sn, cc, sc)
        ob_c, s_b = retention(proj_c, lg_b, zero_state, backward=True)
        if update_ctx:
            yrt_c, s_f = retention(proj_c, lg_f, zero_state, backward=False, finish=(ob_c, ret_gn_w, l))
        else:
            _, s_f = retention(proj_c, lg_f, zero_state, backward=False)
        ob, _ = retention(proj, lg_b, s_b, backward=True, rope_tabs=rope_tabs)
        y_ret, _ = retention(proj, lg_f, s_f, backward=False, rope_tabs=rope_tabs,
                             finish=(ob, ret_gn_w, l))

        wts = (w_o_na[l].astype(BF16), w_fourier[l].astype(BF16), w_o_ret[l].astype(BF16),
               w_out[l].astype(BF16))
        x_all, h_all = merge_branches(y_na, y_fn, y_ret, proj, *wts, x_rows, g1, ln_mix_w, ln_mix_b, l,
                                      sh2, sc2, alpha, extra_rows=B * L if update_ctx else 0)
        gate_tiles = jnp.repeat(g2, N // TOKEN_TILE, axis=0)
        if update_ctx:
            yna_c = context_attention(proj_c)
            yfn_c = fourier_mix(proj_c, cl, sl, cc, sc)
            xc_new, h_c = merge_branches(yna_c, yfn_c, yrt_c, proj_c, *wts, xc_rows, g1c, ln_mix_w, ln_mix_b,
                                         l, sh2c, sc2c, alpha)
            x_all = lax.dynamic_update_slice(x_all, xc_new, (B * N, 0))
            h_all = lax.dynamic_update_slice(h_all, h_c, (B * N, 0))
            gate_tiles = jnp.concatenate([gate_tiles, jnp.repeat(g2c, L // TOKEN_TILE, axis=0)], axis=0)

        x_rows = moe_sublayer(h_all, x_all, gate_tiles, router_w[l].T.astype(BF16), router_bias[l],
                              exp_w_gate, exp_w_up, exp_w_down, sg_all, su_all, sd_all,
                              ln_ffn_w, ln_ffn_b, l, alpha)
        if update_ctx:
            xc_rows = x_rows[B * N:]
    return x_rows[:B * N].reshape(B, N, D)
```

```python
import functools
import math

import numpy as np
import jax
import jax.numpy as jnp
from jax import lax
from jax.experimental import pallas as pl
from jax.experimental.pallas import tpu as pltpu

F32 = jnp.float32
BF16 = jnp.bfloat16
U32 = jnp.uint32

GRID_COLS = 64
NA_HEADS = 8
NA_DH = 64
NA_W = NA_HEADS * NA_DH
NA_ROWS = 8
NA_COLS = 16
FN_GROUPS = 4
FN_GD = 128
FN_W = FN_GROUPS * FN_GD
RT_HEADS = 4
RT_DK = 128
RT_DV = 256
RT_CHUNK = 128
ROPE_THETA = 10000.0
N_EXP = 256
TOPK = 8
N_GRP = 8
TOPK_GRP = 4
GRP_SZ = N_EXP // N_GRP
ROUTED_SCALE = 2.5
LN_EPS = 1e-6
GN_EPS = 1e-5

COL_QA, COL_KA, COL_VA, COL_U, COL_QR, COL_KR, COL_VR, COL_GR, COL_GL = (
    0, 512, 1024, 1536, 2048, 2560, 3072, 4096, 5120)

VMEM_LIMIT = 56 * 1024 * 1024
LANES = 128
SUBLANES = 8
BF16_ROWS = 16
MXU_ROWS = 256
EXPERT_BLOCK = 2 * MXU_ROWS
WEIGHT_SLOTS = 3
TOKEN_TILE = 256
ATTN_ROWS = 4
FOURIER_TILE = 512
NEG = -1e30


def _cparams(sem):
    return pltpu.CompilerParams(dimension_semantics=sem, vmem_limit_bytes=VMEM_LIMIT)


def _ln(x):
    mu = jnp.mean(x, axis=-1, keepdims=True)
    xc = x - mu
    var = jnp.mean(xc * xc, axis=-1, keepdims=True)
    return xc * lax.rsqrt(var + LN_EPS)


def _silu(x):
    return x * jax.nn.sigmoid(x)


def _ada_kernel(c_ref, w_ref, b_ref, o_ref):
    a = _silu(c_ref[...]).astype(BF16)
    o_ref[...] = jnp.dot(a, w_ref[0].astype(BF16), preferred_element_type=F32) + b_ref[0]


def ada_mod(c_rows, w_all, b_all, layer):
    R, D = c_rows.shape
    W = w_all.shape[2]
    tn = 1024
    return pl.pallas_call(
        _ada_kernel,
        out_shape=jax.ShapeDtypeStruct((R, W), F32),
        grid=(W // tn,),
        in_specs=[pl.BlockSpec((R, D), lambda j: (0, 0)),
                  pl.BlockSpec((1, D, tn), lambda j: (layer, 0, j)),
                  pl.BlockSpec((1, 1, tn), lambda j: (layer, 0, j))],
        out_specs=pl.BlockSpec((R, tn), lambda j: (0, j)),
        compiler_params=_cparams(("parallel",)),
        name="ada_mod",
    )(c_rows, w_all, b_all.reshape(b_all.shape[0], 1, W))


def _ln_proj_kernel(x_ref, sh_ref, sc_ref, w_ref, o_ref, h_scr):
    @pl.when(pl.program_id(2) == 0)
    def _():
        h = _ln(x_ref[...]) * (1.0 + sc_ref[0]) + sh_ref[0]
        h_scr[...] = h.astype(BF16)

    o_ref[0] = jnp.dot(h_scr[...], w_ref[...], preferred_element_type=F32).astype(o_ref.dtype)


def ln_proj(x_rows, B, N, shift, scale, w_bf16):
    D = x_rows.shape[1]
    W = w_bf16.shape[1]
    tm = min(N, 1024)
    tn = 2048
    return pl.pallas_call(
        _ln_proj_kernel,
        out_shape=jax.ShapeDtypeStruct((B, N, W), BF16),
        grid=(B, N // tm, W // tn),
        in_specs=[pl.BlockSpec((tm, D), lambda b, i, j: (b * (N // tm) + i, 0)),
                  pl.BlockSpec((1, 1, D), lambda b, i, j: (b, 0, 0)),
                  pl.BlockSpec((1, 1, D), lambda b, i, j: (b, 0, 0)),
                  pl.BlockSpec((D, tn), lambda b, i, j: (0, j))],
        out_specs=pl.BlockSpec((1, tm, tn), lambda b, i, j: (b, i, j)),
        scratch_shapes=[pltpu.VMEM((tm, D), BF16)],
        compiler_params=_cparams(("parallel", "parallel", "arbitrary")),
        name="ln_proj",
    )(x_rows, shift, scale, w_bf16)


def _softmax_pv(s_parts, v):
    m = jnp.max(s_parts[0], axis=-1, keepdims=True)
    for s in s_parts[1:]:
        m = jnp.maximum(m, jnp.max(s, axis=-1, keepdims=True))
    ps = [jnp.exp(s - m) for s in s_parts]
    l = jnp.sum(ps[0], axis=-1, keepdims=True)
    for p in ps[1:]:
        l = l + jnp.sum(p, axis=-1, keepdims=True)
    p = ps[0] if len(ps) == 1 else jnp.concatenate(ps, axis=-1)
    return jnp.dot(p.astype(BF16), v, preferred_element_type=F32) / l


def _window_span(rpb):
    return NA_ROWS + rpb - 1


def _window_attn_kernel(pat_ref, q_ref, k_ref, v_ref, kc_ref, vc_ref, bias_ref, o_ref, kcat, vcat, *,
                        rows, rpb):
    del pat_ref
    L = kc_ref.shape[1]
    span = _window_span(rpb)
    win = span * GRID_COLS
    i = pl.program_id(1)

    @pl.when(i == 0)
    def _():
        kcat[0:L, :] = kc_ref[0]
        vcat[0:L, :] = vc_ref[0]

    first = jnp.clip(i * rpb - NA_ROWS // 2, 0, rows - span)
    start = pl.multiple_of(first * GRID_COLS, GRID_COLS)
    kcat[L:L + win, :] = k_ref[0, pl.ds(start, win), :]
    vcat[L:L + win, :] = v_ref[0, pl.ds(start, win), :]
    q = q_ref[0] * (NA_DH ** -0.5)
    dn = (((1,), (1,)), ((), ()))
    outs = []
    for h in range(NA_HEADS):
        hs = slice(h * NA_DH, (h + 1) * NA_DH)
        s = lax.dot_general(q[:, hs], kcat[:, hs], dn, preferred_element_type=F32)
        outs.append(_softmax_pv([s[:, :L], s[:, L:] + bias_ref[0, h]], vcat[:, hs]))
    o_ref[0] = jnp.concatenate(outs, axis=-1).astype(o_ref.dtype)


def _ctx_attn_kernel(q_ref, kc_ref, vc_ref, o_ref):
    q = q_ref[0] * (NA_DH ** -0.5)
    kc = kc_ref[0]
    vc = vc_ref[0]
    dn = (((1,), (1,)), ((), ()))
    outs = []
    for h in range(NA_HEADS):
        hs = slice(h * NA_DH, (h + 1) * NA_DH)
        s = lax.dot_general(q[:, hs], kc[:, hs], dn, preferred_element_type=F32)
        outs.append(_softmax_pv([s], vc[:, hs]))
    o_ref[0] = jnp.concatenate(outs, axis=-1).astype(o_ref.dtype)


def window_bias_tables(rpb_table, rows, rpb):
    H = rpb_table.shape[0]
    span = _window_span(rpb)
    cidx = np.arange(GRID_COLS)
    c0 = np.clip(cidx - NA_COLS // 2, 0, GRID_COLS - NA_COLS)
    col_ok = (cidx[None, :] >= c0[:, None]) & (cidx[None, :] < c0[:, None] + NA_COLS)
    dc = np.clip(cidx[None, :] - cidx[:, None], -(NA_COLS - 1), NA_COLS - 1) + (NA_COLS - 1)
    col_sel = (dc[:, :, None] == np.arange(2 * NA_COLS - 1)).astype(np.float32)

    keys, patterns, index = [], [], []
    for blk in range(rows // rpb):
        first = int(np.clip(blk * rpb - NA_ROWS // 2, 0, rows - span))
        key_row = first + np.arange(span)
        r = blk * rpb + np.arange(rpb)
        r0 = np.clip(r - NA_ROWS // 2, 0, rows - NA_ROWS)
        assert first <= r0.min() and r0.max() + NA_ROWS <= first + span
        valid = (key_row[None, :] >= r0[:, None]) & (key_row[None, :] < r0[:, None] + NA_ROWS)
        dr = np.where(valid, key_row[None, :] - r[:, None] + NA_ROWS - 1, -1)
        if dr.tobytes() not in keys:
            keys.append(dr.tobytes())
            patterns.append(dr)
        index.append(keys.index(dr.tobytes()))
    dr = np.stack(patterns)
    row_sel = (dr[..., None] == np.arange(2 * NA_ROWS - 1)).astype(np.float32)
    t = jnp.einsum('hab,prja,qkb->phrqjk', rpb_table.astype(F32), row_sel, col_sel,
                   precision=lax.Precision.HIGHEST)
    ok = (dr >= 0)[:, None, :, None, :, None] & col_ok[None, None, None, :, None, :]
    t = jnp.where(ok, t, NEG)
    tabs = t.reshape(len(patterns), H, rpb * GRID_COLS, span * GRID_COLS)
    return tabs, jnp.asarray(np.array(index, np.int32))


def window_attention(proj, proj_ctx, bias_tabs, bias_index):
    B, N, _ = proj.shape
    L = proj_ctx.shape[1]
    rows = N // GRID_COLS
    tq = bias_tabs.shape[2]
    rpb = tq // GRID_COLS
    win = _window_span(rpb) * GRID_COLS
    cb = lambda off: off // NA_W
    grid_spec = pltpu.PrefetchScalarGridSpec(
        num_scalar_prefetch=1,
        grid=(B, rows // rpb),
        in_specs=[pl.BlockSpec((1, tq, NA_W), lambda b, i, pat: (b, i, cb(COL_QA))),
                  pl.BlockSpec((1, N, NA_W), lambda b, i, pat: (b, 0, cb(COL_KA))),
                  pl.BlockSpec((1, N, NA_W), lambda b, i, pat: (b, 0, cb(COL_VA))),
                  pl.BlockSpec((1, L, NA_W), lambda b, i, pat: (b, 0, cb(COL_KA))),
                  pl.BlockSpec((1, L, NA_W), lambda b, i, pat: (b, 0, cb(COL_VA))),
                  pl.BlockSpec((1, NA_HEADS, tq, win), lambda b, i, pat: (pat[i], 0, 0, 0))],
        out_specs=pl.BlockSpec((1, tq, NA_W), lambda b, i, pat: (b, i, 0)),
        scratch_shapes=[pltpu.VMEM((L + win, NA_W), BF16), pltpu.VMEM((L + win, NA_W), BF16)],
    )
    return pl.pallas_call(
        functools.partial(_window_attn_kernel, rows=rows, rpb=rpb),
        out_shape=jax.ShapeDtypeStruct((B, N, NA_W), BF16),
        grid_spec=grid_spec,
        compiler_params=_cparams(("parallel", "arbitrary")),
        name="window_attention",
    )(bias_index, proj, proj, proj, proj_ctx, proj_ctx, bias_tabs)


def context_attention(proj_ctx):
    B, L, _ = proj_ctx.shape
    cb = lambda off: off // NA_W
    tq = min(L, 128)
    return pl.pallas_call(
        _ctx_attn_kernel,
        out_shape=jax.ShapeDtypeStruct((B, L, NA_W), BF16),
        grid=(B, L // tq),
        in_specs=[pl.BlockSpec((1, tq, NA_W), lambda b, r: (b, r, cb(COL_QA))),
                  pl.BlockSpec((1, L, NA_W), lambda b, r: (b, 0, cb(COL_KA))),
                  pl.BlockSpec((1, L, NA_W), lambda b, r: (b, 0, cb(COL_VA)))],
        out_specs=pl.BlockSpec((1, tq, NA_W), lambda b, r: (b, r, 0)),
        compiler_params=_cparams(("parallel", "arbitrary")),
        name="context_attention",
    )(proj_ctx, proj_ctx, proj_ctx)


def _dft_kernel(cn_ref, sn_ref, u_ref, cc_ref, sc_ref, o_ref, *, norm):
    u = u_ref[0]
    a = jnp.dot(cn_ref[...], u, preferred_element_type=F32).astype(BF16)
    b = jnp.dot(sn_ref[...], u, preferred_element_type=F32).astype(BF16)
    y = (jnp.dot(a, cc_ref[...], preferred_element_type=F32)
         - jnp.dot(b, sc_ref[...], preferred_element_type=F32))
    o_ref[0] = (y * norm).astype(o_ref.dtype)


def _unit_circle(rows, cols, period):
    ang = ((rows[:, None] * cols[None, :]) % period).astype(F32) * (2.0 * math.pi / period)
    return jnp.cos(ang), jnp.sin(ang)


def dft_tables(n):
    m = jnp.arange(n, dtype=jnp.int32)
    split = GRID_COLS
    if n <= split * split // 16 or n % split:
        return _unit_circle(m, m, n)
    hi_c, hi_s = _unit_circle(jnp.arange(n // split, dtype=jnp.int32), m, n // split)
    lo_c, lo_s = _unit_circle(jnp.arange(split, dtype=jnp.int32), m, n)
    cos = hi_c[:, None, :] * lo_c[None, :, :] - hi_s[:, None, :] * lo_s[None, :, :]
    sin = hi_s[:, None, :] * lo_c[None, :, :] + hi_c[:, None, :] * lo_s[None, :, :]
    return cos.reshape(n, n), sin.reshape(n, n)


def channel_dft_tables():
    c, s = dft_tables(FN_GD)
    eye = jnp.eye(FN_GROUPS, dtype=F32)
    return jnp.kron(eye, c).astype(BF16), jnp.kron(eye, s).astype(BF16)


def fourier_mix(proj, cn, sn, cc, sc):
    B, N, _ = proj.shape
    tk = min(N, FOURIER_TILE)
    norm = 1.0 / math.sqrt(N * FN_GD)
    return pl.pallas_call(
        functools.partial(_dft_kernel, norm=norm),
        out_shape=jax.ShapeDtypeStruct((B, N, FN_W), BF16),
        grid=(N // tk, B),
        in_specs=[pl.BlockSpec((tk, N), lambda i, b: (i, 0)),
                  pl.BlockSpec((tk, N), lambda i, b: (i, 0)),
                  pl.BlockSpec((1, N, FN_W), lambda i, b: (b, 0, COL_U // FN_W)),
                  pl.BlockSpec((FN_W, FN_W), lambda i, b: (0, 0)),
                  pl.BlockSpec((FN_W, FN_W), lambda i, b: (0, 0))],
        out_specs=pl.BlockSpec((1, tk, FN_W), lambda i, b: (b, i, 0)),
        compiler_params=_cparams(("parallel", "arbitrary")),
        name="fourier_mix",
    )(cn, sn, proj, cc, sc)


def _dft_half_kernel(cn_ref, sn_ref, cn1_ref, sn1_ref, u_ref, cc_ref, sc_ref, flip_ref, lo_ref, hi_ref, *,
                     norm):
    u = u_ref[0]

    def pq(c_rows, s_rows):
        a = jnp.dot(c_rows, u, preferred_element_type=F32).astype(BF16)
        b = jnp.dot(s_rows, u, preferred_element_type=F32).astype(BF16)
        return (jnp.dot(a, cc_ref[...], preferred_element_type=F32),
                jnp.dot(b, sc_ref[...], preferred_element_type=F32))

    p, q = pq(cn_ref[...], sn_ref[...])
    lo_ref[0] = ((p - q) * norm).astype(lo_ref.dtype)
    m = ((p + q) * norm).astype(BF16)
    mirrored = jnp.dot(flip_ref[...], m, preferred_element_type=F32)
    p1, q1 = pq(cn1_ref[...], sn1_ref[...])
    first = ((p1 + q1) * norm)[0:1]
    row = lax.broadcasted_iota(jnp.int32, mirrored.shape, 0)
    hi_ref[0] = jnp.where(row == 0, first, mirrored).astype(hi_ref.dtype)


def fourier_mix_half(proj, cn, sn, cc, sc):
    B, N, _ = proj.shape
    tk = FOURIER_TILE
    nt = N // (2 * tk)
    norm = 1.0 / math.sqrt(N * FN_GD)
    r = jnp.arange(tk, dtype=jnp.int32)
    flip = ((r[None, :] == tk - r[:, None]) & (r[:, None] >= 1)).astype(BF16)
    nxt = lambda t, b: ((t + 1) * (tk // BF16_ROWS), 0)
    half = jax.ShapeDtypeStruct((B, N // 2, FN_W), BF16)
    full = lambda a: pl.BlockSpec(a.shape, lambda t, b: (0,) * a.ndim)
    lo, hi = pl.pallas_call(
        functools.partial(_dft_half_kernel, norm=norm),
        out_shape=(half, half),
        grid=(nt, B),
        in_specs=[pl.BlockSpec((tk, N), lambda t, b: (t, 0)),
                  pl.BlockSpec((tk, N), lambda t, b: (t, 0)),
                  pl.BlockSpec((BF16_ROWS, N), nxt),
                  pl.BlockSpec((BF16_ROWS, N), nxt),
                  pl.BlockSpec((1, N, FN_W), lambda t, b: (b, 0, COL_U // FN_W)),
                  full(cc), full(sc), full(flip)],
        out_specs=(pl.BlockSpec((1, tk, FN_W), lambda t, b: (b, t, 0)),
                   pl.BlockSpec((1, tk, FN_W), lambda t, b: (b, nt - 1 - t, 0))),
        compiler_params=_cparams(("parallel", "arbitrary")),
        name="fourier_mix_half",
    )(cn, sn, cn, sn, proj, cc, sc, flip)
    return jnp.concatenate([lo, hi], axis=1)


def _retention_kernel(*refs, backward, rope, finish, cpb):
    it = iter(refs)
    lg_ref = next(it)
    q_ref, k_ref, v_ref = next(it), next(it), next(it)
    if rope:
        cos_ref, sin_ref, perm_ref = next(it), next(it), next(it)
    s0_ref = next(it)
    if finish:
        ob_ref, gate_ref, gn_ref = next(it), next(it), next(it)
    o_ref, sf_ref = next(it), next(it)
    state = next(it)

    C = RT_CHUNK
    step = pl.program_id(1)

    @pl.when(step == 0)
    def _():
        state[...] = s0_ref[...]

    pos_r = lax.broadcasted_iota(jnp.int32, (C, 1), 0).astype(F32)
    ci = lax.broadcasted_iota(jnp.int32, (C, C), 0)
    mi = lax.broadcasted_iota(jnp.int32, (C, C), 1)
    if backward:
        dist = jnp.maximum(mi - ci, 0).astype(F32)
        band = mi > ci
    else:
        dist = jnp.maximum(ci - mi, 0).astype(F32)
        band = ci >= mi
    heads = []
    for h in range(RT_HEADS):
        lg = lg_ref[h]
        if backward:
            k_dec = jnp.exp(lg * pos_r)
            q_dec = jnp.exp(lg * (C - pos_r))
        else:
            k_dec = jnp.exp(lg * (C - 1 - pos_r))
            q_dec = jnp.exp(lg * (pos_r + 1))
        heads.append((k_dec, q_dec, jnp.where(band, jnp.exp(lg * dist), 0.0), jnp.exp(lg * C)))
    if rope:
        cos = jnp.concatenate([cos_ref[...]] * RT_HEADS, axis=-1)
        sin = jnp.concatenate([sin_ref[...]] * RT_HEADS, axis=-1)
        perm = perm_ref[...]

    order = range(cpb - 1, -1, -1) if backward else range(cpb)
    for bi in range(q_ref.shape[0]):
        q = q_ref[bi]
        k = k_ref[bi]
        if rope:
            qf = q.astype(F32) * cos + jnp.dot(q, perm, preferred_element_type=F32) * sin
            kf = k.astype(F32) * cos + jnp.dot(k, perm, preferred_element_type=F32) * sin
        else:
            qf = q.astype(F32)
            kf = k.astype(F32)
        qf = qf * (RT_DK ** -0.5)
        for h in range(RT_HEADS):
            k_dec, q_dec, decay, chunk_dec = heads[h]
            kcol = slice(h * RT_DK, (h + 1) * RT_DK)
            vcol = slice(h * RT_DV, (h + 1) * RT_DV)
            s_prev = state[bi, h]
            for c in order:
                sl = slice(c * C, (c + 1) * C)
                qc = qf[sl, kcol]
                kc = kf[sl, kcol]
                vc = v_ref[bi, sl, vcol]
                scores = lax.dot_general(qc.astype(BF16), kc.astype(BF16), (((1,), (1,)), ((), ())),
                                         preferred_element_type=F32) * decay
                lhs = jnp.concatenate([scores, qc * q_dec], axis=-1).astype(BF16)
                rhs = jnp.concatenate([vc, s_prev.astype(BF16)], axis=0)
                o = jnp.dot(lhs, rhs, preferred_element_type=F32)
                kv = lax.dot_general((kc * k_dec).astype(BF16), vc, (((0,), (0,)), ((), ())),
                                     preferred_element_type=F32)
                s_prev = chunk_dec * s_prev + kv
                if finish:
                    o = o + ob_ref[bi, sl, vcol]
                    mu = jnp.mean(o, axis=-1, keepdims=True)
                    oc = o - mu
                    var = jnp.mean(oc * oc, axis=-1, keepdims=True)
                    y = oc * lax.rsqrt(var + GN_EPS) * gn_ref[0, :, vcol]
                    o = y * _silu(gate_ref[bi, sl, vcol].astype(F32))
                o_ref[bi, sl, vcol] = o.astype(o_ref.dtype)
            state[bi, h] = s_prev

    @pl.when(step == pl.num_programs(1) - 1)
    def _():
        sf_ref[...] = state[...]


def rope_tables(n):
    t = jnp.arange(n)
    row, col = t // GRID_COLS, t % GRID_COLS
    half = RT_DK // 2
    n_pairs = half // 2
    inv_freq = ROPE_THETA ** (-jnp.arange(n_pairs, dtype=F32) / n_pairs)

    def cs(pos):
        ang = pos.astype(F32)[:, None] * inv_freq[None, :]
        c, s = jnp.cos(ang), jnp.sin(ang)
        return jnp.concatenate([c, c], axis=-1), jnp.concatenate([-s, s], axis=-1)

    cr, sr = cs(row)
    cc, sc = cs(col)
    cos = jnp.concatenate([cr, cc], axis=-1)
    sin = jnp.concatenate([sr, sc], axis=-1)
    idx = jnp.arange(RT_DK)
    src = (idx // half) * half + (idx % half + n_pairs) % half
    perm = (idx[:, None] == src[None, :]).astype(F32)
    return cos, sin, jnp.kron(jnp.eye(RT_HEADS, dtype=F32), perm).astype(BF16)


def retention(proj, log_g, s0, *, backward, rope_tabs=None, finish=None):
    B, N, _ = proj.shape
    nc = N // RT_CHUNK
    cpb = min(nc, 4)
    tb = cpb * RT_CHUNK
    nsteps = nc // cpb
    bpb = 1
    blk = (lambda s: nsteps - 1 - s) if backward else (lambda s: s)
    rope = rope_tabs is not None
    fin = finish is not None

    qk_w = RT_HEADS * RT_DK
    v_w = RT_HEADS * RT_DV
    state_spec = pl.BlockSpec((bpb, RT_HEADS, RT_DK, RT_DV), lambda b, s: (b, 0, 0, 0))
    in_specs = [pl.BlockSpec(memory_space=pltpu.SMEM),
                pl.BlockSpec((bpb, tb, qk_w), lambda b, s: (b, blk(s), COL_QR // qk_w)),
                pl.BlockSpec((bpb, tb, qk_w), lambda b, s: (b, blk(s), COL_KR // qk_w)),
                pl.BlockSpec((bpb, tb, v_w), lambda b, s: (b, blk(s), COL_VR // v_w))]
    args = [log_g, proj, proj, proj]
    if rope:
        in_specs += [pl.BlockSpec((tb, RT_DK), lambda b, s: (blk(s), 0)),
                     pl.BlockSpec((tb, RT_DK), lambda b, s: (blk(s), 0)),
                     pl.BlockSpec((qk_w, qk_w), lambda b, s: (0, 0))]
        args += list(rope_tabs)
    in_specs.append(state_spec)
    args.append(s0)
    if fin:
        o_b, gn_all, layer = finish
        in_specs += [pl.BlockSpec((bpb, tb, v_w), lambda b, s: (b, blk(s), 0)),
                     pl.BlockSpec((bpb, tb, v_w), lambda b, s: (b, blk(s), COL_GR // v_w)),
                     pl.BlockSpec((1, 1, v_w), lambda b, s: (layer, 0, 0))]
        args += [o_b, proj, gn_all.reshape(gn_all.shape[0], 1, -1)]
    out_dtype = BF16 if fin else F32
    o, s_fin = pl.pallas_call(
        functools.partial(_retention_kernel, backward=backward, rope=rope, finish=fin, cpb=cpb),
        out_shape=(jax.ShapeDtypeStruct((B, N, v_w), out_dtype),
                   jax.ShapeDtypeStruct((B, RT_HEADS, RT_DK, RT_DV), F32)),
        grid=(B // bpb, nsteps),
        in_specs=in_specs,
        out_specs=(pl.BlockSpec((bpb, tb, v_w), lambda b, s: (b, blk(s), 0)), state_spec),
        scratch_shapes=[pltpu.VMEM((bpb, RT_HEADS, RT_DK, RT_DV), F32)],
        compiler_params=_cparams(("parallel", "arbitrary")),
        name="retention_bwd" if backward else "retention_fwd",
    )(*args)
    return o, s_fin


def _merge_kernel(yna_ref, yfn_ref, yrt_ref, g0_ref, g1_ref, g2_ref, wna_ref, wfn_ref, wrt_ref,
                  wout_ref, x_ref, gate_ref, lnw_ref, lnb_ref, sh_ref, sc_ref, xo_ref, h_ref, *, alpha, n_real):
    real = pl.program_id(0) < n_real

    @pl.when(real)
    def _():
        a = jnp.dot(yna_ref[0], wna_ref[...], preferred_element_type=F32)
        y = jax.nn.sigmoid(g0_ref[0].astype(F32)) * a
        b = jnp.dot(yfn_ref[0], wfn_ref[...], preferred_element_type=F32)
        y = y + jax.nn.sigmoid(g1_ref[0].astype(F32)) * b
        c = jnp.dot(yrt_ref[0], wrt_ref[...], preferred_element_type=F32)
        y = y + jax.nn.sigmoid(g2_ref[0].astype(F32)) * c
        y = jnp.dot(y.astype(BF16), wout_ref[...], preferred_element_type=F32)
        xn = _ln(alpha * x_ref[...] + gate_ref[0] * y) * lnw_ref[0] + lnb_ref[0]
        xo_ref[...] = xn
        h_ref[...] = _ln(xn) * (1.0 + sc_ref[0]) + sh_ref[0]

    @pl.when(jnp.logical_not(real))
    def _():
        xo_ref[...] = jnp.zeros_like(xo_ref)
        h_ref[...] = jnp.zeros_like(h_ref)


def merge_branches(y_na, y_fn, y_ret, proj, w_na, w_fn, w_rt, w_out, x_rows, gate, ln_w_all, ln_b_all, layer,
                   shift2, scale2, alpha, extra_rows=0):
    B, N, _ = y_na.shape
    D = x_rows.shape[1]
    tm = math.gcd(min(N, 512), extra_rows) if extra_rows else min(N, 512)
    nt = N // tm
    n_real = B * nt
    bi = lambda g: (jnp.minimum(g, n_real - 1) // nt, jnp.minimum(g, n_real - 1) % nt)
    row = lambda w: pl.BlockSpec((1, tm, w), lambda g: (*bi(g), 0))
    glb = lambda j: pl.BlockSpec((1, tm, D), lambda g: (*bi(g), COL_GL // D + j))
    full = lambda a: pl.BlockSpec(a.shape, lambda g: (0,) * a.ndim)
    vec = pl.BlockSpec((1, 1, D), lambda g: (bi(g)[0], 0, 0))
    prm = pl.BlockSpec((1, 1, D), lambda g: (layer, 0, 0))
    flat = pl.BlockSpec((tm, D), lambda g: (g, 0))
    depth = ln_w_all.shape[0]
    out = jax.ShapeDtypeStruct((B * N + extra_rows, D), F32)
    return pl.pallas_call(
        functools.partial(_merge_kernel, alpha=alpha, n_real=n_real),
        out_shape=(out, out),
        grid=(n_real + extra_rows // tm,),
        in_specs=[row(NA_W), row(FN_W), row(RT_HEADS * RT_DV), glb(0), glb(1), glb(2),
                  full(w_na), full(w_fn), full(w_rt), full(w_out),
                  pl.BlockSpec((tm, D), lambda g: (jnp.minimum(g, n_real - 1), 0)), vec, prm, prm, vec, vec],
        out_specs=(flat, flat),
        compiler_params=_cparams(("parallel",)),
        name="merge_branches",
    )(y_na, y_fn, y_ret, proj, proj, proj, w_na, w_fn, w_rt, w_out, x_rows, gate,
      ln_w_all.reshape(depth, 1, D), ln_b_all.reshape(depth, 1, D), shift2, scale2)


def _router_kernel(h_ref, w_ref, b_ref, e_ref, wcol_ref, rk_ref, cnt_ref, cnt_scr):
    tm = h_ref.shape[0]

    @pl.when(pl.program_id(0) == 0)
    def _():
        cnt_scr[...] = jnp.zeros_like(cnt_scr)

    logits = lax.dot_general(w_ref[...], h_ref[...].astype(BF16), (((1,), (1,)), ((), ())),
                             preferred_element_type=F32)
    scores = jax.nn.sigmoid(logits)
    sel = scores + b_ref[...]
    row = lax.broadcasted_iota(jnp.int32, (N_EXP, tm), 0)

    def first_max(vals, rows):
        m = jnp.max(vals, axis=0, keepdims=True)
        idx = jnp.min(jnp.where(vals == m, rows, N_EXP), axis=0, keepdims=True)
        return m, idx

    gscore = []
    grow = lax.broadcasted_iota(jnp.int32, (GRP_SZ, tm), 0)
    for g in range(N_GRP):
        gs = slice(g * GRP_SZ, (g + 1) * GRP_SZ)
        vals, rows = sel[gs], grow + g * GRP_SZ
        m1, i1 = first_max(vals, rows)
        m2, _ = first_max(jnp.where(rows == i1, -jnp.inf, vals), rows)
        gscore.append(m1 + m2)
    cand = []
    for g in range(N_GRP):
        beaten = jnp.zeros((1, tm), jnp.int32)
        for o in range(N_GRP):
            if o == g:
                continue
            ahead = (gscore[o] >= gscore[g]) if o < g else (gscore[o] > gscore[g])
            beaten = beaten + ahead.astype(jnp.int32)
        gs = slice(g * GRP_SZ, (g + 1) * GRP_SZ)
        cand.append(jnp.where(beaten < TOPK_GRP, sel[gs], -jnp.inf))
    cand = jnp.concatenate(cand, axis=0)

    chosen = jnp.zeros((N_EXP, tm), F32)
    ids, wts = [], []
    for _ in range(TOPK):
        _, idx = first_max(cand, row)
        hit = row == idx
        ids.append(idx)
        wts.append(jnp.sum(jnp.where(hit, scores, 0.0), axis=0, keepdims=True))
        chosen = jnp.where(hit, 1.0, chosen)
        cand = jnp.where(hit, -jnp.inf, cand)
    wsum = wts[0]
    for w in wts[1:]:
        wsum = wsum + w

    ti = lax.broadcasted_iota(jnp.int32, (tm, tm), 0)
    tj = lax.broadcasted_iota(jnp.int32, (tm, tm), 1)
    before = (ti < tj).astype(BF16)
    rank_full = jnp.dot(chosen.astype(BF16), before, preferred_element_type=F32) + cnt_scr[...]
    cnt_scr[...] = cnt_scr[...] + jnp.sum(chosen, axis=1, keepdims=True)

    ranks = [jnp.sum(jnp.where(row == ids[k], rank_full, 0.0), axis=0, keepdims=True) for k in range(TOPK)]
    e_ref[...] = jnp.concatenate(ids, axis=0)
    rk_ref[...] = jnp.concatenate(ranks, axis=0).astype(jnp.int32)
    wn = [w / wsum * ROUTED_SCALE for w in wts]
    wpad = jnp.concatenate(wn + [jnp.zeros((LANES - TOPK, tm), F32)], axis=0)
    wcol_ref[...] = wpad.T
    cnt_ref[...] = cnt_scr[...].astype(jnp.int32)


def router(h, w_t_bf16, bias):
    T, D = h.shape
    tm = TOKEN_TILE
    lane_out = pl.BlockSpec((TOPK, tm), lambda i: (0, i))
    return pl.pallas_call(
        _router_kernel,
        out_shape=(jax.ShapeDtypeStruct((TOPK, T), jnp.int32), jax.ShapeDtypeStruct((T, LANES), F32),
                   jax.ShapeDtypeStruct((TOPK, T), jnp.int32), jax.ShapeDtypeStruct((N_EXP, 1), jnp.int32)),
        grid=(T // tm,),
        in_specs=[pl.BlockSpec((tm, D), lambda i: (i, 0)),
                  pl.BlockSpec((N_EXP, D), lambda i: (0, 0)),
                  pl.BlockSpec((N_EXP, 1), lambda i: (0, 0))],
        out_specs=(lane_out, pl.BlockSpec((tm, LANES), lambda i: (i, 0)), lane_out,
                   pl.BlockSpec((N_EXP, 1), lambda i: (0, 0))),
        scratch_shapes=[pltpu.VMEM((N_EXP, 1), F32)],
        compiler_params=_cparams(("arbitrary",)),
        name="router",
    )(h, w_t_bf16, bias.reshape(N_EXP, 1))


def _slot_kernel(e_ref, rk_ref, ps_ref, o_ref):
    tm = e_ref.shape[1]
    row = lax.broadcasted_iota(jnp.int32, (N_EXP, tm), 0)
    ps = ps_ref[...].astype(F32)
    outs = []
    for k in range(TOPK):
        base = jnp.sum(jnp.where(row == e_ref[k:k + 1, :], ps, 0.0), axis=0, keepdims=True)
        outs.append(base.astype(jnp.int32) + rk_ref[k:k + 1, :])
    o_ref[...] = jnp.concatenate(outs, axis=0)


def assign_slots(top_e, rank, pad_start):
    K, T = top_e.shape
    tm = next(t for t in (2048, 1024, 512, 256) if T % t == 0)
    blk = pl.BlockSpec((K, tm), lambda i: (0, i))
    return pl.pallas_call(
        _slot_kernel,
        out_shape=jax.ShapeDtypeStruct((K, T), jnp.int32),
        grid=(T // tm,),
        in_specs=[blk, blk, pl.BlockSpec((N_EXP, 1), lambda i: (0, 0))],
        out_specs=blk,
        compiler_params=_cparams(("parallel",)),
        name="assign_slots",
    )(top_e, rank, pad_start.reshape(N_EXP, 1).astype(jnp.int32))


def _pack_pair(lo, hi):
    a = lax.bitcast_convert_type(lo.astype(BF16).astype(F32), U32) >> 16
    b = lax.bitcast_convert_type(hi.astype(BF16).astype(F32), U32) & jnp.uint32(0xFFFF0000)
    return a | b


def _unpack_pair(w):
    lo = lax.bitcast_convert_type(w << 16, F32)
    hi = lax.bitcast_convert_type(w & jnp.uint32(0xFFFF0000), F32)
    return lo, hi


def _dispatch_kernel(pe_ref, cnt_ref, nu_ref, slot_hbm, h_ref, sg_ref, su_ref, sd_ref, xs_hbm, sh_ref,
                     slot_smem, stage, zbuf, sem_idx, sem):
    tm, D = h_ref.shape
    ch = stage.shape[1]
    i = pl.program_id(0)
    nb = xs_hbm.shape[0] // EXPERT_BLOCK

    @pl.when(i == 0)
    def _():
        zbuf[...] = jnp.zeros_like(zbuf)

        def block_copy(b):
            start = pl.multiple_of(b * EXPERT_BLOCK, EXPERT_BLOCK)
            return pltpu.make_async_copy(zbuf, xs_hbm.at[pl.ds(start, EXPERT_BLOCK)], sem)

        def for_partial_blocks(fn):
            def body(e, carry):
                @pl.when(cnt_ref[e] % EXPERT_BLOCK != 0)
                def _():
                    fn(block_copy(pe_ref[e] // EXPERT_BLOCK - 1))
                return carry
            lax.fori_loop(0, N_EXP, body, 0)

        def for_unused_blocks(fn):
            def body(b, carry):
                fn(block_copy(b))
                return carry
            lax.fori_loop(nu_ref[0], nb, body, 0)

        for_partial_blocks(lambda cp: cp.start())
        for_unused_blocks(lambda cp: cp.start())
        for_partial_blocks(lambda cp: cp.wait())
        for_unused_blocks(lambda cp: cp.wait())

    idx_copy = pltpu.make_async_copy(slot_hbm.at[i], slot_smem, sem_idx)
    idx_copy.start()
    x = h_ref[...]
    half = D // 2
    for j in range(ch):
        stage[:, j, :] = _pack_pair(x[:, j * LANES:(j + 1) * LANES],
                                    x[:, half + j * LANES:half + (j + 1) * LANES])
    idx_copy.wait()

    def issue(t, carry):
        for k in range(TOPK):
            pltpu.make_async_copy(stage.at[t], xs_hbm.at[slot_smem[k * tm + t]], sem).start(priority=k % 2)
        return carry

    lax.fori_loop(0, tm, issue, 0)

    hb = x.astype(BF16)
    g = jnp.dot(hb, sg_ref[0], preferred_element_type=F32)
    u = jnp.dot(hb, su_ref[0], preferred_element_type=F32)
    sh_ref[...] = jnp.dot((_silu(g) * u).astype(BF16), sd_ref[0], preferred_element_type=F32)

    for k in range(TOPK):
        pltpu.make_async_copy(stage, xs_hbm.at[pl.ds(0, tm)], sem).wait()


def dispatch(h, slots, n_slots, pad_end, counts, n_used, sg, su, sd, layer):
    T, D = h.shape
    tm = TOKEN_TILE
    ch = D // 2 // LANES
    shw = lambda a: pl.BlockSpec((1,) + a.shape[1:], lambda i, pe, cnt, nu: (layer, 0, 0))
    grid_spec = pltpu.PrefetchScalarGridSpec(
        num_scalar_prefetch=3,
        grid=(T // tm,),
        in_specs=[pl.BlockSpec(memory_space=pl.ANY),
                  pl.BlockSpec((tm, D), lambda i, pe, cnt, nu: (i, 0)),
                  shw(sg), shw(su), shw(sd)],
        out_specs=(pl.BlockSpec(memory_space=pl.ANY),
                   pl.BlockSpec((tm, D), lambda i, pe, cnt, nu: (i, 0))),
        scratch_shapes=[pltpu.SMEM((TOPK * tm,), jnp.int32),
                        pltpu.VMEM((tm, ch, LANES), U32),
                        pltpu.VMEM((EXPERT_BLOCK, ch, LANES), U32),
                        pltpu.SemaphoreType.DMA(()),
                        pltpu.SemaphoreType.DMA(())],
    )
    return pl.pallas_call(
        _dispatch_kernel,
        out_shape=(jax.ShapeDtypeStruct((n_slots, ch, LANES), U32), jax.ShapeDtypeStruct((T, D), F32)),
        grid_spec=grid_spec,
        compiler_params=_cparams(("arbitrary",)),
        name="moe_dispatch",
    )(pad_end.astype(jnp.int32), counts.astype(jnp.int32), n_used, slots, h, sg, su, sd)


def _expert_kernel(be_ref, nu_ref, buf_ref, nxt1_ref, nxt2_ref, xs_ref, wg_hbm, wu_hbm, wd_hbm, ys_ref,
                   wg_f, wu_f, wd_f, wg_s, wu_s, wd_s, sem, *, layer):
    i = pl.program_id(0)
    used = i < nu_ref[0]
    new_expert = jnp.logical_or(i == 0, be_ref[i] != be_ref[jnp.maximum(i - 1, 0)])

    def weight_copies(e, b):
        return (pltpu.make_async_copy(wg_hbm.at[layer, e], wg_f.at[b], sem.at[b]),
                pltpu.make_async_copy(wu_hbm.at[layer, e], wu_f.at[b], sem.at[b]),
                pltpu.make_async_copy(wd_hbm.at[layer, e], wd_f.at[b], sem.at[b]))

    @pl.when(i == 0)
    def _():
        for cp in weight_copies(be_ref[0], 0):
            cp.start()

        @pl.when(nxt1_ref[0] >= 0)
        def _():
            for cp in weight_copies(nxt1_ref[0], 1):
                cp.start()

    @pl.when(jnp.logical_not(used))
    def _():
        ys_ref[...] = jnp.zeros_like(ys_ref)

    @pl.when(jnp.logical_and(used, new_expert))
    def _():
        b = buf_ref[i]
        for cp in weight_copies(be_ref[i], b):
            cp.wait()

        @pl.when(nxt2_ref[i] >= 0)
        def _():
            for cp in weight_copies(nxt2_ref[i], jnp.where(b == 0, WEIGHT_SLOTS - 1, b - 1)):
                cp.start()

        wg_s[...] = wg_f[b].astype(BF16)
        wu_s[...] = wu_f[b].astype(BF16)
        wd_s[...] = wd_f[b].astype(BF16)

    @pl.when(used)
    def _():
        D = wg_s.shape[0]
        ch = D // 2 // LANES
        half = D // 2
        for part in range(xs_ref.shape[0] // (ch * MXU_ROWS)):
            base = part * MXU_ROWS * ch
            pairs = [_unpack_pair(xs_ref[pl.ds(base + j, MXU_ROWS, stride=ch), :]) for j in range(ch)]
            x = jnp.concatenate([lo.astype(BF16) for lo, _ in pairs] + [hi.astype(BF16) for _, hi in pairs],
                                axis=-1)
            g = jnp.dot(x, wg_s[...], preferred_element_type=F32)
            u = jnp.dot(x, wu_s[...], preferred_element_type=F32)
            a = (_silu(g) * u).astype(BF16)
            y = jnp.dot(a, wd_s[...], preferred_element_type=F32)
            for j in range(ch):
                ys_ref[pl.ds(base + j, MXU_ROWS, stride=ch), :] = _pack_pair(
                    y[:, j * LANES:(j + 1) * LANES], y[:, half + j * LANES:half + (j + 1) * LANES])


def routed_experts(xs, block_expert, n_used, counts, w_gate_all, w_up_all, w_down_all, layer):
    n_slots, ch, _ = xs.shape
    _, E, D, Fd = w_gate_all.shape
    nb = n_slots // EXPERT_BLOCK
    rows_blk = EXPERT_BLOCK * ch
    starts = jnp.concatenate([jnp.ones((1,), jnp.int32),
                              (block_expert[1:] != block_expert[:-1]).astype(jnp.int32)])
    buffer_of = ((jnp.cumsum(starts) - 1) % WEIGHT_SLOTS).astype(jnp.int32)
    ids = jnp.arange(E, dtype=jnp.int32)
    later = jnp.where((counts[None, :] > 0) & (ids[None, :] > ids[:, None]), ids[None, :], E)
    next_used = jnp.min(later, axis=1)
    next_used = jnp.concatenate([next_used, jnp.full((1,), E, next_used.dtype)])
    lookup = lambda table, keys: jnp.sum(jnp.where(keys[:, None] == jnp.arange(E + 1)[None, :],
                                                   table[None, :], 0), axis=1)
    after1 = lookup(next_used, block_expert)
    after2 = lookup(next_used, after1)
    to_id = lambda a: jnp.where(a == E, -1, a).astype(jnp.int32)
    idx = lambda f: (lambda i, be, nu, buf, n1, n2: f(i, nu))
    grid_spec = pltpu.PrefetchScalarGridSpec(
        num_scalar_prefetch=5,
        grid=(nb,),
        in_specs=[pl.BlockSpec((rows_blk, LANES), idx(lambda i, nu: (jnp.minimum(i, nu[0] - 1), 0))),
                  pl.BlockSpec(memory_space=pl.ANY),
                  pl.BlockSpec(memory_space=pl.ANY),
                  pl.BlockSpec(memory_space=pl.ANY)],
        out_specs=pl.BlockSpec((rows_blk, LANES), idx(lambda i, nu: (i, 0))),
        scratch_shapes=[pltpu.VMEM((WEIGHT_SLOTS, D, Fd), F32), pltpu.VMEM((WEIGHT_SLOTS, D, Fd), F32),
                        pltpu.VMEM((WEIGHT_SLOTS, Fd, D), F32),
                        pltpu.VMEM((D, Fd), BF16), pltpu.VMEM((D, Fd), BF16), pltpu.VMEM((Fd, D), BF16),
                        pltpu.SemaphoreType.DMA((WEIGHT_SLOTS,))],
    )
    ys = pl.pallas_call(
        functools.partial(_expert_kernel, layer=layer),
        out_shape=jax.ShapeDtypeStruct((n_slots * ch, LANES), U32),
        grid_spec=grid_spec,
        compiler_params=_cparams(("arbitrary",)),
        name="routed_experts",
    )(block_expert, n_used, buffer_of, to_id(after1), to_id(after2), xs.reshape(n_slots * ch, LANES),
      w_gate_all, w_up_all, w_down_all)
    return ys.reshape(n_slots, ch, LANES)


def _combine_kernel(slot_hbm, ys_hbm, wt_ref, sh_ref, x_ref, gate_ref, lnw_ref, lnb_ref, o_ref,
                    slot_smem, buf, acc, sem_idx, sem, *, alpha):
    tm = x_ref.shape[0]
    ch = buf.shape[2]
    i = pl.program_id(0)
    cur = i % 2

    last = pl.num_programs(0) - 1
    half = ch * LANES

    per_tile = TOPK * tm

    def load_slots(tile, b):
        dst = slot_smem.at[pl.ds(pl.multiple_of(b * per_tile, per_tile), per_tile)]
        idx_copy = pltpu.make_async_copy(slot_hbm.at[tile], dst, sem_idx)
        idx_copy.start()
        idx_copy.wait()

    def issue_rows(t, b):
        for k in range(TOPK):
            pltpu.make_async_copy(ys_hbm.at[slot_smem[b * per_tile + k * tm + t]], buf.at[b, k, :, t, :],
                                  sem.at[b]).start(priority=k % 2)

    def wait_rows(b):
        for k in range(TOPK):
            pltpu.make_async_copy(buf.at[b, k], buf.at[b, k], sem.at[b]).wait()

    @pl.when(i == 0)
    def _():
        load_slots(0, 0)

        def first(t, carry):
            issue_rows(t, 0)
            return carry

        lax.fori_loop(0, tm, first, 0)

    nxt = 1 - cur
    load_slots(jnp.minimum(i + 1, last), nxt)
    wait_rows(cur)

    def reduce_and_issue(sb, carry):
        r0 = pl.multiple_of(sb * SUBLANES, SUBLANES)
        rows = pl.ds(r0, SUBLANES)
        for tt in range(SUBLANES):
            issue_rows(r0 + tt, nxt)
        wt = wt_ref[rows, :]
        for j in range(ch):
            lo, hi = _unpack_pair(buf[cur, 0, j, rows, :])
            r_lo, r_hi = wt[:, 0:1] * lo, wt[:, 0:1] * hi
            for k in range(1, TOPK):
                lo, hi = _unpack_pair(buf[cur, k, j, rows, :])
                r_lo, r_hi = r_lo + wt[:, k:k + 1] * lo, r_hi + wt[:, k:k + 1] * hi
            acc[rows, j * LANES:(j + 1) * LANES] = r_lo
            acc[rows, half + j * LANES:half + (j + 1) * LANES] = r_hi
        return carry

    lax.fori_loop(0, tm // SUBLANES, reduce_and_issue, 0)

    f = sh_ref[...] + acc[...]
    o_ref[...] = _ln(alpha * x_ref[...] + gate_ref[0] * f) * lnw_ref[0] + lnb_ref[0]

    @pl.when(i == last)
    def _():
        wait_rows(nxt)


def combine(slots, ys, wt, shared, x, gate_tiles, ln_w_all, ln_b_all, layer, alpha):
    T, D = x.shape
    tm = TOKEN_TILE
    ch = ys.shape[1]
    depth = ln_w_all.shape[0]
    prm = pl.BlockSpec((1, 1, D), lambda i: (layer, 0, 0))
    return pl.pallas_call(
        functools.partial(_combine_kernel, alpha=alpha),
        out_shape=jax.ShapeDtypeStruct((T, D), F32),
        grid=(T // tm,),
        in_specs=[pl.BlockSpec(memory_space=pl.ANY),
                  pl.BlockSpec(memory_space=pl.ANY),
                  pl.BlockSpec((tm, LANES), lambda i: (i, 0)),
                  pl.BlockSpec((tm, D), lambda i: (i, 0)),
                  pl.BlockSpec((tm, D), lambda i: (i, 0)),
                  pl.BlockSpec((1, 1, D), lambda i: (i, 0, 0)),
                  prm, prm],
        out_specs=pl.BlockSpec((tm, D), lambda i: (i, 0)),
        scratch_shapes=[pltpu.SMEM((2 * TOPK * tm,), jnp.int32),
                        pltpu.VMEM((2, TOPK, ch, tm, LANES), U32),
                        pltpu.VMEM((tm, D), F32),
                        pltpu.SemaphoreType.DMA(()),
                        pltpu.SemaphoreType.DMA((2,))],
        compiler_params=_cparams(("arbitrary",)),
        name="moe_combine",
    )(slots, ys, wt, shared, x, gate_tiles, ln_w_all.reshape(depth, 1, D), ln_b_all.reshape(depth, 1, D))


def moe_sublayer(h, x, gate_tiles, router_w_t, router_bias, w_gate_all, w_up_all, w_down_all, sg, su, sd,
                 ln_w_all, ln_b_all, layer, alpha):
    T, D = h.shape
    top_e, w_col, rank, counts = router(h, router_w_t, router_bias)
    counts = counts[:, 0]
    padded = (counts + EXPERT_BLOCK - 1) // EXPERT_BLOCK * EXPERT_BLOCK
    pad_end = jnp.cumsum(padded)
    pad_start = pad_end - padded
    slots = assign_slots(top_e, rank, pad_start)
    n_tiles = T // TOKEN_TILE
    slots = slots.reshape(TOPK, n_tiles, TOKEN_TILE).transpose(1, 0, 2).reshape(n_tiles, TOPK * TOKEN_TILE)
    n_slots = -(-(T * TOPK + N_EXP * (EXPERT_BLOCK - 1)) // EXPERT_BLOCK) * EXPERT_BLOCK
    nb = n_slots // EXPERT_BLOCK
    block_start = jnp.arange(nb, dtype=jnp.int32) * EXPERT_BLOCK
    block_expert = jnp.minimum(jnp.sum(block_start[:, None] >= pad_end[None, :], axis=1),
                               N_EXP - 1).astype(jnp.int32)
    n_used = (pad_end[-1:] // EXPERT_BLOCK).astype(jnp.int32)

    xs, shared = dispatch(h, slots, n_slots, pad_end, counts, n_used, sg, su, sd, layer)
    ys = routed_experts(xs, block_expert, n_used, counts, w_gate_all, w_up_all, w_down_all, layer)
    return combine(slots, ys, w_col, shared, x, gate_tiles, ln_w_all, ln_b_all, layer, alpha)


def kernel(x, c, ctx, c_ctx, ada_w, ada_b, w_in, na_rpb, w_o_na, w_fourier, ret_decay_fwd, ret_decay_bwd,
           ret_gn_w, w_o_ret, w_out, ln_mix_w, ln_mix_b, router_w, router_bias, exp_w_gate, exp_w_up,
           exp_w_down, sh_w_gate, sh_w_up, sh_w_down, ln_ffn_w, ln_ffn_b):
    B, N, D = x.shape
    L = ctx.shape[1]
    depth = ada_w.shape[0]
    alpha = (2.0 * depth) ** 0.25
    rows = N // GRID_COLS
    attn_rows = ATTN_ROWS if (rows % ATTN_ROWS == 0 and rows >= _window_span(ATTN_ROWS)) else 1

    rope_tabs = rope_tables(N)
    cn, sn = (t.astype(BF16) for t in dft_tables(N))
    cl, sl = (t.astype(BF16) for t in dft_tables(L))
    cc, sc = channel_dft_tables()
    zero_state = jnp.zeros((B, RT_HEADS, RT_DK, RT_DV), F32)
    mod_rows = -(-(B + 1) // BF16_ROWS) * BF16_ROWS
    c_rows = jnp.concatenate([c, c_ctx[None, :], jnp.zeros((mod_rows - B - 1, D), F32)], axis=0)
    sg_all, su_all, sd_all = sh_w_gate.astype(BF16), sh_w_up.astype(BF16), sh_w_down.astype(BF16)

    x_rows = x.reshape(B * N, D)
    xc_rows = ctx.reshape(B * L, D)
    for l in range(depth):
        update_ctx = l < depth - 1
        mod = ada_mod(c_rows, ada_w, ada_b, l)
        sh1, sc1, g1, sh2, sc2, g2 = (m[:, None, :] for m in jnp.split(mod[:B], 6, axis=-1))
        mod_c = jnp.broadcast_to(mod[B:B + 1], (B, 6 * D))
        sh1c, sc1c, g1c, sh2c, sc2c, g2c = (m[:, None, :] for m in jnp.split(mod_c, 6, axis=-1))
        lg_f = jax.nn.log_sigmoid(ret_decay_fwd[l].astype(F32))
        lg_b = jax.nn.log_sigmoid(ret_decay_bwd[l].astype(F32))

        w_in_b = w_in[l].astype(BF16)
        proj = ln_proj(x_rows, B, N, sh1, sc1, w_in_b)
        proj_c = ln_proj(xc_rows, 1, B * L, sh1c[:1], sc1c[:1], w_in_b).reshape(B, L, -1)

        y_na = window_attention(proj, proj_c, *window_bias_tables(na_rpb[l], rows, attn_rows))
        y_fn = (fourier_mix_half if N % (2 * FOURIER_TILE) == 0 else fourier_mix)(proj, cn, sn, cc, sc)
        ob_c, s_b = retention(proj_c, lg_b, zero_state, backward=True)
        if update_ctx:
            yrt_c, s_f = retention(proj_c, lg_f, zero_state, backward=False, finish=(ob_c, ret_gn_w, l))
        else:
            _, s_f = retention(proj_c, lg_f, zero_state, backward=False)
        ob, _ = retention(proj, lg_b, s_b, backward=True, rope_tabs=rope_tabs)
        y_ret, _ = retention(proj, lg_f, s_f, backward=False, rope_tabs=rope_tabs,
                             finish=(ob, ret_gn_w, l))

        wts = (w_o_na[l].astype(BF16), w_fourier[l].astype(BF16), w_o_ret[l].astype(BF16),
               w_out[l].astype(BF16))
        x_all, h_all = merge_branches(y_na, y_fn, y_ret, proj, *wts, x_rows, g1, ln_mix_w, ln_mix_b, l,
                                      sh2, sc2, alpha, extra_rows=B * L if update_ctx else 0)
        gate_tiles = jnp.repeat(g2, N // TOKEN_TILE, axis=0)
        if update_ctx:
            yna_c = context_attention(proj_c)
            yfn_c = fourier_mix(proj_c, cl, sl, cc, sc)
            xc_new, h_c = merge_branches(yna_c, yfn_c, yrt_c, proj_c, *wts, xc_rows, g1c, ln_mix_w, ln_mix_b,
                                         l, sh2c, sc2c, alpha)
            x_all = lax.dynamic_update_slice(x_all, xc_new, (B * N, 0))
            h_all = lax.dynamic_update_slice(h_all, h_c, (B * N, 0))
            gate_tiles = jnp.concatenate([gate_tiles, jnp.repeat(g2c, L // TOKEN_TILE, axis=0)], axis=0)

        x_rows = moe_sublayer(h_all, x_all, gate_tiles, router_w[l].T.astype(BF16), router_bias[l],
                              exp_w_gate, exp_w_up, exp_w_down, sg_all, su_all, sd_all,
                              ln_ffn_w, ln_ffn_b, l, alpha)
        if update_ctx:
            xc_rows = x_rows[B * N:]
    return x_rows[:B * N].reshape(B, N, D)
```

```python
import functools
import math

import numpy as np
import jax
import jax.numpy as jnp
from jax import lax
from jax.experimental import pallas as pl
from jax.experimental.pallas import tpu as pltpu

F32 = jnp.float32
BF16 = jnp.bfloat16
U32 = jnp.uint32

GRID_COLS = 64
NA_HEADS = 8
NA_DH = 64
NA_W = NA_HEADS * NA_DH
NA_ROWS = 8
NA_COLS = 16
FN_GROUPS = 4
FN_GD = 128
FN_W = FN_GROUPS * FN_GD
RT_HEADS = 4
RT_DK = 128
RT_DV = 256
RT_CHUNK = 128
ROPE_THETA = 10000.0
N_EXP = 256
TOPK = 8
N_GRP = 8
TOPK_GRP = 4
GRP_SZ = N_EXP // N_GRP
ROUTED_SCALE = 2.5
LN_EPS = 1e-6
GN_EPS = 1e-5

COL_QA, COL_KA, COL_VA, COL_U, COL_QR, COL_KR, COL_VR, COL_GR, COL_GL = (
    0, 512, 1024, 1536, 2048, 2560, 3072, 4096, 5120)

VMEM_LIMIT = 56 * 1024 * 1024
LANES = 128
SUBLANES = 8
BF16_ROWS = 16
EXPERT_BLOCK = 512
WEIGHT_SLOTS = 3
TOKEN_TILE = 256
ATTN_ROWS = 4
FOURIER_TILE = 512
NEG = -1e30


def _cparams(sem):
    return pltpu.CompilerParams(dimension_semantics=sem, vmem_limit_bytes=VMEM_LIMIT)


def _ln(x):
    mu = jnp.mean(x, axis=-1, keepdims=True)
    xc = x - mu
    var = jnp.mean(xc * xc, axis=-1, keepdims=True)
    return xc * lax.rsqrt(var + LN_EPS)


def _silu(x):
    return x * jax.nn.sigmoid(x)


def _ada_kernel(c_ref, w_ref, b_ref, o_ref):
    a = _silu(c_ref[...]).astype(BF16)
    o_ref[...] = jnp.dot(a, w_ref[0].astype(BF16), preferred_element_type=F32) + b_ref[0]


def ada_mod(c_rows, w_all, b_all, layer):
    R, D = c_rows.shape
    W = w_all.shape[2]
    tn = 1024
    return pl.pallas_call(
        _ada_kernel,
        out_shape=jax.ShapeDtypeStruct((R, W), F32),
        grid=(W // tn,),
        in_specs=[pl.BlockSpec((R, D), lambda j: (0, 0)),
                  pl.BlockSpec((1, D, tn), lambda j: (layer, 0, j)),
                  pl.BlockSpec((1, 1, tn), lambda j: (layer, 0, j))],
        out_specs=pl.BlockSpec((R, tn), lambda j: (0, j)),
        compiler_params=_cparams(("parallel",)),
        name="ada_mod",
    )(c_rows, w_all, b_all.reshape(b_all.shape[0], 1, W))


def _ln_proj_kernel(x_ref, sh_ref, sc_ref, w_ref, o_ref, h_scr):
    @pl.when(pl.program_id(2) == 0)
    def _():
        h = _ln(x_ref[...]) * (1.0 + sc_ref[0]) + sh_ref[0]
        h_scr[...] = h.astype(BF16)

    o_ref[0] = jnp.dot(h_scr[...], w_ref[...], preferred_element_type=F32).astype(o_ref.dtype)


def ln_proj(x_rows, B, N, shift, scale, w_bf16):
    D = x_rows.shape[1]
    W = w_bf16.shape[1]
    tm = min(N, 1024)
    tn = 2048
    return pl.pallas_call(
        _ln_proj_kernel,
        out_shape=jax.ShapeDtypeStruct((B, N, W), BF16),
        grid=(B, N // tm, W // tn),
        in_specs=[pl.BlockSpec((tm, D), lambda b, i, j: (b * (N // tm) + i, 0)),
                  pl.BlockSpec((1, 1, D), lambda b, i, j: (b, 0, 0)),
                  pl.BlockSpec((1, 1, D), lambda b, i, j: (b, 0, 0)),
                  pl.BlockSpec((D, tn), lambda b, i, j: (0, j))],
        out_specs=pl.BlockSpec((1, tm, tn), lambda b, i, j: (b, i, j)),
        scratch_shapes=[pltpu.VMEM((tm, D), BF16)],
        compiler_params=_cparams(("parallel", "parallel", "arbitrary")),
        name="ln_proj",
    )(x_rows, shift, scale, w_bf16)


def _softmax_pv(s_parts, v):
    m = jnp.max(s_parts[0], axis=-1, keepdims=True)
    for s in s_parts[1:]:
        m = jnp.maximum(m, jnp.max(s, axis=-1, keepdims=True))
    ps = [jnp.exp(s - m) for s in s_parts]
    l = jnp.sum(ps[0], axis=-1, keepdims=True)
    for p in ps[1:]:
        l = l + jnp.sum(p, axis=-1, keepdims=True)
    p = ps[0] if len(ps) == 1 else jnp.concatenate(ps, axis=-1)
    return jnp.dot(p.astype(BF16), v, preferred_element_type=F32) / l


def _window_span(rpb):
    return NA_ROWS + rpb - 1


def _window_attn_kernel(pat_ref, q_ref, k_ref, v_ref, kc_ref, vc_ref, bias_ref, o_ref, kcat, vcat, *,
                        rows, rpb):
    del pat_ref
    L = kc_ref.shape[1]
    span = _window_span(rpb)
    win = span * GRID_COLS
    i = pl.program_id(1)

    @pl.when(i == 0)
    def _():
        kcat[0:L, :] = kc_ref[0]
        vcat[0:L, :] = vc_ref[0]

    first = jnp.clip(i * rpb - NA_ROWS // 2, 0, rows - span)
    start = pl.multiple_of(first * GRID_COLS, GRID_COLS)
    kcat[L:L + win, :] = k_ref[0, pl.ds(start, win), :]
    vcat[L:L + win, :] = v_ref[0, pl.ds(start, win), :]
    q = q_ref[0] * (NA_DH ** -0.5)
    dn = (((1,), (1,)), ((), ()))
    outs = []
    for h in range(NA_HEADS):
        hs = slice(h * NA_DH, (h + 1) * NA_DH)
        s = lax.dot_general(q[:, hs], kcat[:, hs], dn, preferred_element_type=F32)
        outs.append(_softmax_pv([s[:, :L], s[:, L:] + bias_ref[0, h]], vcat[:, hs]))
    o_ref[0] = jnp.concatenate(outs, axis=-1).astype(o_ref.dtype)


def _ctx_attn_kernel(q_ref, kc_ref, vc_ref, o_ref):
    q = q_ref[0] * (NA_DH ** -0.5)
    kc = kc_ref[0]
    vc = vc_ref[0]
    dn = (((1,), (1,)), ((), ()))
    outs = []
    for h in range(NA_HEADS):
        hs = slice(h * NA_DH, (h + 1) * NA_DH)
        s = lax.dot_general(q[:, hs], kc[:, hs], dn, preferred_element_type=F32)
        outs.append(_softmax_pv([s], vc[:, hs]))
    o_ref[0] = jnp.concatenate(outs, axis=-1).astype(o_ref.dtype)


def window_bias_tables(rpb_table, rows, rpb):
    H = rpb_table.shape[0]
    span = _window_span(rpb)
    cidx = np.arange(GRID_COLS)
    c0 = np.clip(cidx - NA_COLS // 2, 0, GRID_COLS - NA_COLS)
    col_ok = (cidx[None, :] >= c0[:, None]) & (cidx[None, :] < c0[:, None] + NA_COLS)
    dc = np.clip(cidx[None, :] - cidx[:, None], -(NA_COLS - 1), NA_COLS - 1) + (NA_COLS - 1)
    col_sel = (dc[:, :, None] == np.arange(2 * NA_COLS - 1)).astype(np.float32)

    keys, patterns, index = [], [], []
    for blk in range(rows // rpb):
        first = int(np.clip(blk * rpb - NA_ROWS // 2, 0, rows - span))
        key_row = first + np.arange(span)
        r = blk * rpb + np.arange(rpb)
        r0 = np.clip(r - NA_ROWS // 2, 0, rows - NA_ROWS)
        assert first <= r0.min() and r0.max() + NA_ROWS <= first + span
        valid = (key_row[None, :] >= r0[:, None]) & (key_row[None, :] < r0[:, None] + NA_ROWS)
        dr = np.where(valid, key_row[None, :] - r[:, None] + NA_ROWS - 1, -1)
        if dr.tobytes() not in keys:
            keys.append(dr.tobytes())
            patterns.append(dr)
        index.append(keys.index(dr.tobytes()))
    dr = np.stack(patterns)
    row_sel = (dr[..., None] == np.arange(2 * NA_ROWS - 1)).astype(np.float32)
    t = jnp.einsum('hab,prja,qkb->phrqjk', rpb_table.astype(F32), row_sel, col_sel,
                   precision=lax.Precision.HIGHEST)
    ok = (dr >= 0)[:, None, :, None, :, None] & col_ok[None, None, None, :, None, :]
    t = jnp.where(ok, t, NEG)
    tabs = t.reshape(len(patterns), H, rpb * GRID_COLS, span * GRID_COLS)
    return tabs, jnp.asarray(np.array(index, np.int32))


def window_attention(proj, proj_ctx, bias_tabs, bias_index):
    B, N, _ = proj.shape
    L = proj_ctx.shape[1]
    rows = N // GRID_COLS
    tq = bias_tabs.shape[2]
    rpb = tq // GRID_COLS
    win = _window_span(rpb) * GRID_COLS
    cb = lambda off: off // NA_W
    grid_spec = pltpu.PrefetchScalarGridSpec(
        num_scalar_prefetch=1,
        grid=(B, rows // rpb),
        in_specs=[pl.BlockSpec((1, tq, NA_W), lambda b, i, pat: (b, i, cb(COL_QA))),
                  pl.BlockSpec((1, N, NA_W), lambda b, i, pat: (b, 0, cb(COL_KA))),
                  pl.BlockSpec((1, N, NA_W), lambda b, i, pat: (b, 0, cb(COL_VA))),
                  pl.BlockSpec((1, L, NA_W), lambda b, i, pat: (b, 0, cb(COL_KA))),
                  pl.BlockSpec((1, L, NA_W), lambda b, i, pat: (b, 0, cb(COL_VA))),
                  pl.BlockSpec((1, NA_HEADS, tq, win), lambda b, i, pat: (pat[i], 0, 0, 0))],
        out_specs=pl.BlockSpec((1, tq, NA_W), lambda b, i, pat: (b, i, 0)),
        scratch_shapes=[pltpu.VMEM((L + win, NA_W), BF16), pltpu.VMEM((L + win, NA_W), BF16)],
    )
    return pl.pallas_call(
        functools.partial(_window_attn_kernel, rows=rows, rpb=rpb),
        out_shape=jax.ShapeDtypeStruct((B, N, NA_W), BF16),
        grid_spec=grid_spec,
        compiler_params=_cparams(("parallel", "arbitrary")),
        name="window_attention",
    )(bias_index, proj, proj, proj, proj_ctx, proj_ctx, bias_tabs)


def context_attention(proj_ctx):
    B, L, _ = proj_ctx.shape
    cb = lambda off: off // NA_W
    tq = min(L, 128)
    return pl.pallas_call(
        _ctx_attn_kernel,
        out_shape=jax.ShapeDtypeStruct((B, L, NA_W), BF16),
        grid=(B, L // tq),
        in_specs=[pl.BlockSpec((1, tq, NA_W), lambda b, r: (b, r, cb(COL_QA))),
                  pl.BlockSpec((1, L, NA_W), lambda b, r: (b, 0, cb(COL_KA))),
                  pl.BlockSpec((1, L, NA_W), lambda b, r: (b, 0, cb(COL_VA)))],
        out_specs=pl.BlockSpec((1, tq, NA_W), lambda b, r: (b, r, 0)),
        compiler_params=_cparams(("parallel", "arbitrary")),
        name="context_attention",
    )(proj_ctx, proj_ctx, proj_ctx)


def _dft_kernel(cn_ref, sn_ref, u_ref, cc_ref, sc_ref, o_ref, *, norm):
    u = u_ref[0]
    a = jnp.dot(cn_ref[...], u, preferred_element_type=F32).astype(BF16)
    b = jnp.dot(sn_ref[...], u, preferred_element_type=F32).astype(BF16)
    y = (jnp.dot(a, cc_ref[...], preferred_element_type=F32)
         - jnp.dot(b, sc_ref[...], preferred_element_type=F32))
    o_ref[0] = (y * norm).astype(o_ref.dtype)


def _unit_circle(rows, cols, period):
    ang = ((rows[:, None] * cols[None, :]) % period).astype(F32) * (2.0 * math.pi / period)
    return jnp.cos(ang), jnp.sin(ang)


def dft_tables(n):
    m = jnp.arange(n, dtype=jnp.int32)
    split = GRID_COLS
    if n <= split * split // 16 or n % split:
        return _unit_circle(m, m, n)
    hi_c, hi_s = _unit_circle(jnp.arange(n // split, dtype=jnp.int32), m, n // split)
    lo_c, lo_s = _unit_circle(jnp.arange(split, dtype=jnp.int32), m, n)
    cos = hi_c[:, None, :] * lo_c[None, :, :] - hi_s[:, None, :] * lo_s[None, :, :]
    sin = hi_s[:, None, :] * lo_c[None, :, :] + hi_c[:, None, :] * lo_s[None, :, :]
    return cos.reshape(n, n), sin.reshape(n, n)


def channel_dft_tables():
    c, s = dft_tables(FN_GD)
    eye = jnp.eye(FN_GROUPS, dtype=F32)
    return jnp.kron(eye, c).astype(BF16), jnp.kron(eye, s).astype(BF16)


def fourier_mix(proj, cn, sn, cc, sc):
    B, N, _ = proj.shape
    tk = min(N, FOURIER_TILE)
    norm = 1.0 / math.sqrt(N * FN_GD)
    return pl.pallas_call(
        functools.partial(_dft_kernel, norm=norm),
        out_shape=jax.ShapeDtypeStruct((B, N, FN_W), BF16),
        grid=(N // tk, B),
        in_specs=[pl.BlockSpec((tk, N), lambda i, b: (i, 0)),
                  pl.BlockSpec((tk, N), lambda i, b: (i, 0)),
                  pl.BlockSpec((1, N, FN_W), lambda i, b: (b, 0, COL_U // FN_W)),
                  pl.BlockSpec((FN_W, FN_W), lambda i, b: (0, 0)),
                  pl.BlockSpec((FN_W, FN_W), lambda i, b: (0, 0))],
        out_specs=pl.BlockSpec((1, tk, FN_W), lambda i, b: (b, i, 0)),
        compiler_params=_cparams(("parallel", "arbitrary")),
        name="fourier_mix",
    )(cn, sn, proj, cc, sc)


def _dft_half_kernel(cn_ref, sn_ref, cn1_ref, sn1_ref, u_ref, cc_ref, sc_ref, flip_ref, lo_ref, hi_ref, *,
                     norm):
    u = u_ref[0]

    def pq(c_rows, s_rows):
        a = jnp.dot(c_rows, u, preferred_element_type=F32).astype(BF16)
        b = jnp.dot(s_rows, u, preferred_element_type=F32).astype(BF16)
        return (jnp.dot(a, cc_ref[...], preferred_element_type=F32),
                jnp.dot(b, sc_ref[...], preferred_element_type=F32))

    p, q = pq(cn_ref[...], sn_ref[...])
    lo_ref[0] = ((p - q) * norm).astype(lo_ref.dtype)
    m = ((p + q) * norm).astype(BF16)
    mirrored = jnp.dot(flip_ref[...], m, preferred_element_type=F32)
    p1, q1 = pq(cn1_ref[...], sn1_ref[...])
    first = ((p1 + q1) * norm)[0:1]
    row = lax.broadcasted_iota(jnp.int32, mirrored.shape, 0)
    hi_ref[0] = jnp.where(row == 0, first, mirrored).astype(hi_ref.dtype)


def fourier_mix_half(proj, cn, sn, cc, sc):
    B, N, _ = proj.shape
    tk = FOURIER_TILE
    nt = N // (2 * tk)
    norm = 1.0 / math.sqrt(N * FN_GD)
    r = jnp.arange(tk, dtype=jnp.int32)
    flip = ((r[None, :] == tk - r[:, None]) & (r[:, None] >= 1)).astype(BF16)
    nxt = lambda t, b: ((t + 1) * (tk // BF16_ROWS), 0)
    half = jax.ShapeDtypeStruct((B, N // 2, FN_W), BF16)
    full = lambda a: pl.BlockSpec(a.shape, lambda t, b: (0,) * a.ndim)
    lo, hi = pl.pallas_call(
        functools.partial(_dft_half_kernel, norm=norm),
        out_shape=(half, half),
        grid=(nt, B),
        in_specs=[pl.BlockSpec((tk, N), lambda t, b: (t, 0)),
                  pl.BlockSpec((tk, N), lambda t, b: (t, 0)),
                  pl.BlockSpec((BF16_ROWS, N), nxt),
                  pl.BlockSpec((BF16_ROWS, N), nxt),
                  pl.BlockSpec((1, N, FN_W), lambda t, b: (b, 0, COL_U // FN_W)),
                  full(cc), full(sc), full(flip)],
        out_specs=(pl.BlockSpec((1, tk, FN_W), lambda t, b: (b, t, 0)),
                   pl.BlockSpec((1, tk, FN_W), lambda t, b: (b, nt - 1 - t, 0))),
        compiler_params=_cparams(("parallel", "arbitrary")),
        name="fourier_mix_half",
    )(cn, sn, cn, sn, proj, cc, sc, flip)
    return jnp.concatenate([lo, hi], axis=1)


def _retention_kernel(*refs, backward, rope, finish, cpb):
    it = iter(refs)
    lg_ref = next(it)
    q_ref, k_ref, v_ref = next(it), next(it), next(it)
    if rope:
        cos_ref, sin_ref, perm_ref = next(it), next(it), next(it)
    s0_ref = next(it)
    if finish:
        ob_ref, gate_ref, gn_ref = next(it), next(it), next(it)
    o_ref, sf_ref = next(it), next(it)
    state = next(it)

    C = RT_CHUNK
    step = pl.program_id(1)

    @pl.when(step == 0)
    def _():
        state[...] = s0_ref[...]

    pos_r = lax.broadcasted_iota(jnp.int32, (C, 1), 0).astype(F32)
    ci = lax.broadcasted_iota(jnp.int32, (C, C), 0)
    mi = lax.broadcasted_iota(jnp.int32, (C, C), 1)
    if backward:
        dist = jnp.maximum(mi - ci, 0).astype(F32)
        band = mi > ci
    else:
        dist = jnp.maximum(ci - mi, 0).astype(F32)
        band = ci >= mi
    heads = []
    for h in range(RT_HEADS):
        lg = lg_ref[h]
        if backward:
            k_dec = jnp.exp(lg * pos_r)
            q_dec = jnp.exp(lg * (C - pos_r))
        else:
            k_dec = jnp.exp(lg * (C - 1 - pos_r))
            q_dec = jnp.exp(lg * (pos_r + 1))
        heads.append((k_dec, q_dec, jnp.where(band, jnp.exp(lg * dist), 0.0), jnp.exp(lg * C)))
    if rope:
        cos = jnp.concatenate([cos_ref[...]] * RT_HEADS, axis=-1)
        sin = jnp.concatenate([sin_ref[...]] * RT_HEADS, axis=-1)
        perm = perm_ref[...]

    order = range(cpb - 1, -1, -1) if backward else range(cpb)
    for bi in range(q_ref.shape[0]):
        q = q_ref[bi]
        k = k_ref[bi]
        if rope:
            qf = q.astype(F32) * cos + jnp.dot(q, perm, preferred_element_type=F32) * sin
            kf = k.astype(F32) * cos + jnp.dot(k, perm, preferred_element_type=F32) * sin
        else:
            qf = q.astype(F32)
            kf = k.astype(F32)
        qf = qf * (RT_DK ** -0.5)
        for h in range(RT_HEADS):
            k_dec, q_dec, decay, chunk_dec = heads[h]
            kcol = slice(h * RT_DK, (h + 1) * RT_DK)
            vcol = slice(h * RT_DV, (h + 1) * RT_DV)
            s_prev = state[bi, h]
            for c in order:
                sl = slice(c * C, (c + 1) * C)
                qc = qf[sl, kcol]
                kc = kf[sl, kcol]
                vc = v_ref[bi, sl, vcol]
                scores = lax.dot_general(qc.astype(BF16), kc.astype(BF16), (((1,), (1,)), ((), ())),
                                         preferred_element_type=F32) * decay
                lhs = jnp.concatenate([scores, qc * q_dec], axis=-1).astype(BF16)
                rhs = jnp.concatenate([vc, s_prev.astype(BF16)], axis=0)
                o = jnp.dot(lhs, rhs, preferred_element_type=F32)
                kv = lax.dot_general((kc * k_dec).astype(BF16), vc, (((0,), (0,)), ((), ())),
                                     preferred_element_type=F32)
                s_prev = chunk_dec * s_prev + kv
                if finish:
                    o = o + ob_ref[bi, sl, vcol]
                    mu = jnp.mean(o, axis=-1, keepdims=True)
                    oc = o - mu
                    var = jnp.mean(oc * oc, axis=-1, keepdims=True)
                    y = oc * lax.rsqrt(var + GN_EPS) * gn_ref[0, :, vcol]
                    o = y * _silu(gate_ref[bi, sl, vcol].astype(F32))
                o_ref[bi, sl, vcol] = o.astype(o_ref.dtype)
            state[bi, h] = s_prev

    @pl.when(step == pl.num_programs(1) - 1)
    def _():
        sf_ref[...] = state[...]


def rope_tables(n):
    t = jnp.arange(n)
    row, col = t // GRID_COLS, t % GRID_COLS
    half = RT_DK // 2
    n_pairs = half // 2
    inv_freq = ROPE_THETA ** (-jnp.arange(n_pairs, dtype=F32) / n_pairs)

    def cs(pos):
        ang = pos.astype(F32)[:, None] * inv_freq[None, :]
        c, s = jnp.cos(ang), jnp.sin(ang)
        return jnp.concatenate([c, c], axis=-1), jnp.concatenate([-s, s], axis=-1)

    cr, sr = cs(row)
    cc, sc = cs(col)
    cos = jnp.concatenate([cr, cc], axis=-1)
    sin = jnp.concatenate([sr, sc], axis=-1)
    idx = jnp.arange(RT_DK)
    src = (idx // half) * half + (idx % half + n_pairs) % half
    perm = (idx[:, None] == src[None, :]).astype(F32)
    return cos, sin, jnp.kron(jnp.eye(RT_HEADS, dtype=F32), perm).astype(BF16)


def retention(proj, log_g, s0, *, backward, rope_tabs=None, finish=None):
    B, N, _ = proj.shape
    nc = N // RT_CHUNK
    cpb = min(nc, 4)
    tb = cpb * RT_CHUNK
    nsteps = nc // cpb
    bpb = 1
    blk = (lambda s: nsteps - 1 - s) if backward else (lambda s: s)
    rope = rope_tabs is not None
    fin = finish is not None

    qk_w = RT_HEADS * RT_DK
    v_w = RT_HEADS * RT_DV
    state_spec = pl.BlockSpec((bpb, RT_HEADS, RT_DK, RT_DV), lambda b, s: (b, 0, 0, 0))
    in_specs = [pl.BlockSpec(memory_space=pltpu.SMEM),
                pl.BlockSpec((bpb, tb, qk_w), lambda b, s: (b, blk(s), COL_QR // qk_w)),
                pl.BlockSpec((bpb, tb, qk_w), lambda b, s: (b, blk(s), COL_KR // qk_w)),
                pl.BlockSpec((bpb, tb, v_w), lambda b, s: (b, blk(s), COL_VR // v_w))]
    args = [log_g, proj, proj, proj]
    if rope:
        in_specs += [pl.BlockSpec((tb, RT_DK), lambda b, s: (blk(s), 0)),
                     pl.BlockSpec((tb, RT_DK), lambda b, s: (blk(s), 0)),
                     pl.BlockSpec((qk_w, qk_w), lambda b, s: (0, 0))]
        args += list(rope_tabs)
    in_specs.append(state_spec)
    args.append(s0)
    if fin:
        o_b, gn_all, layer = finish
        in_specs += [pl.BlockSpec((bpb, tb, v_w), lambda b, s: (b, blk(s), 0)),
                     pl.BlockSpec((bpb, tb, v_w), lambda b, s: (b, blk(s), COL_GR // v_w)),
                     pl.BlockSpec((1, 1, v_w), lambda b, s: (layer, 0, 0))]
        args += [o_b, proj, gn_all.reshape(gn_all.shape[0], 1, -1)]
    out_dtype = BF16 if fin else F32
    o, s_fin = pl.pallas_call(
        functools.partial(_retention_kernel, backward=backward, rope=rope, finish=fin, cpb=cpb),
        out_shape=(jax.ShapeDtypeStruct((B, N, v_w), out_dtype),
                   jax.ShapeDtypeStruct((B, RT_HEADS, RT_DK, RT_DV), F32)),
        grid=(B // bpb, nsteps),
        in_specs=in_specs,
        out_specs=(pl.BlockSpec((bpb, tb, v_w), lambda b, s: (b, blk(s), 0)), state_spec),
        scratch_shapes=[pltpu.VMEM((bpb, RT_HEADS, RT_DK, RT_DV), F32)],
        compiler_params=_cparams(("parallel", "arbitrary")),
        name="retention_bwd" if backward else "retention_fwd",
    )(*args)
    return o, s_fin


def _merge_kernel(yna_ref, yfn_ref, yrt_ref, g0_ref, g1_ref, g2_ref, wna_ref, wfn_ref, wrt_ref,
                  wout_ref, x_ref, gate_ref, lnw_ref, lnb_ref, sh_ref, sc_ref, xo_ref, h_ref, *, alpha, n_real):
    real = pl.program_id(0) < n_real

    @pl.when(real)
    def _():
        a = jnp.dot(yna_ref[0], wna_ref[...], preferred_element_type=F32)
        y = jax.nn.sigmoid(g0_ref[0].astype(F32)) * a
        b = jnp.dot(yfn_ref[0], wfn_ref[...], preferred_element_type=F32)
        y = y + jax.nn.sigmoid(g1_ref[0].astype(F32)) * b
        c = jnp.dot(yrt_ref[0], wrt_ref[...], preferred_element_type=F32)
        y = y + jax.nn.sigmoid(g2_ref[0].astype(F32)) * c
        y = jnp.dot(y.astype(BF16), wout_ref[...], preferred_element_type=F32)
        xn = _ln(alpha * x_ref[...] + gate_ref[0] * y) * lnw_ref[0] + lnb_ref[0]
        xo_ref[...] = xn
        h_ref[...] = _ln(xn) * (1.0 + sc_ref[0]) + sh_ref[0]

    @pl.when(jnp.logical_not(real))
    def _():
        xo_ref[...] = jnp.zeros_like(xo_ref)
        h_ref[...] = jnp.zeros_like(h_ref)


def merge_branches(y_na, y_fn, y_ret, proj, w_na, w_fn, w_rt, w_out, x_rows, gate, ln_w_all, ln_b_all, layer,
                   shift2, scale2, alpha, extra_rows=0):
    B, N, _ = y_na.shape
    D = x_rows.shape[1]
    tm = math.gcd(min(N, 512), extra_rows) if extra_rows else min(N, 512)
    nt = N // tm
    n_real = B * nt
    bi = lambda g: (jnp.minimum(g, n_real - 1) // nt, jnp.minimum(g, n_real - 1) % nt)
    row = lambda w: pl.BlockSpec((1, tm, w), lambda g: (*bi(g), 0))
    glb = lambda j: pl.BlockSpec((1, tm, D), lambda g: (*bi(g), COL_GL // D + j))
    full = lambda a: pl.BlockSpec(a.shape, lambda g: (0,) * a.ndim)
    vec = pl.BlockSpec((1, 1, D), lambda g: (bi(g)[0], 0, 0))
    prm = pl.BlockSpec((1, 1, D), lambda g: (layer, 0, 0))
    flat = pl.BlockSpec((tm, D), lambda g: (g, 0))
    depth = ln_w_all.shape[0]
    out = jax.ShapeDtypeStruct((B * N + extra_rows, D), F32)
    return pl.pallas_call(
        functools.partial(_merge_kernel, alpha=alpha, n_real=n_real),
        out_shape=(out, out),
        grid=(n_real + extra_rows // tm,),
        in_specs=[row(NA_W), row(FN_W), row(RT_HEADS * RT_DV), glb(0), glb(1), glb(2),
                  full(w_na), full(w_fn), full(w_rt), full(w_out),
                  pl.BlockSpec((tm, D), lambda g: (jnp.minimum(g, n_real - 1), 0)), vec, prm, prm, vec, vec],
        out_specs=(flat, flat),
        compiler_params=_cparams(("parallel",)),
        name="merge_branches",
    )(y_na, y_fn, y_ret, proj, proj, proj, w_na, w_fn, w_rt, w_out, x_rows, gate,
      ln_w_all.reshape(depth, 1, D), ln_b_all.reshape(depth, 1, D), shift2, scale2)


def _router_kernel(h_ref, w_ref, b_ref, e_ref, wcol_ref, rk_ref, cnt_ref, cnt_scr):
    tm = h_ref.shape[0]

    @pl.when(pl.program_id(0) == 0)
    def _():
        cnt_scr[...] = jnp.zeros_like(cnt_scr)

    logits = lax.dot_general(w_ref[...], h_ref[...].astype(BF16), (((1,), (1,)), ((), ())),
                             preferred_element_type=F32)
    scores = jax.nn.sigmoid(logits)
    sel = scores + b_ref[...]
    row = lax.broadcasted_iota(jnp.int32, (N_EXP, tm), 0)

    def first_max(vals, rows):
        m = jnp.max(vals, axis=0, keepdims=True)
        idx = jnp.min(jnp.where(vals == m, rows, N_EXP), axis=0, keepdims=True)
        return m, idx

    gscore = []
    grow = lax.broadcasted_iota(jnp.int32, (GRP_SZ, tm), 0)
    for g in range(N_GRP):
        gs = slice(g * GRP_SZ, (g + 1) * GRP_SZ)
        vals, rows = sel[gs], grow + g * GRP_SZ
        m1, i1 = first_max(vals, rows)
        m2, _ = first_max(jnp.where(rows == i1, -jnp.inf, vals), rows)
        gscore.append(m1 + m2)
    cand = []
    for g in range(N_GRP):
        beaten = jnp.zeros((1, tm), jnp.int32)
        for o in range(N_GRP):
            if o == g:
                continue
            ahead = (gscore[o] >= gscore[g]) if o < g else (gscore[o] > gscore[g])
            beaten = beaten + ahead.astype(jnp.int32)
        gs = slice(g * GRP_SZ, (g + 1) * GRP_SZ)
        cand.append(jnp.where(beaten < TOPK_GRP, sel[gs], -jnp.inf))
    cand = jnp.concatenate(cand, axis=0)

    chosen = jnp.zeros((N_EXP, tm), F32)
    ids, wts = [], []
    for _ in range(TOPK):
        _, idx = first_max(cand, row)
        hit = row == idx
        ids.append(idx)
        wts.append(jnp.sum(jnp.where(hit, scores, 0.0), axis=0, keepdims=True))
        chosen = jnp.where(hit, 1.0, chosen)
        cand = jnp.where(hit, -jnp.inf, cand)
    wsum = wts[0]
    for w in wts[1:]:
        wsum = wsum + w

    ti = lax.broadcasted_iota(jnp.int32, (tm, tm), 0)
    tj = lax.broadcasted_iota(jnp.int32, (tm, tm), 1)
    before = (ti < tj).astype(BF16)
    rank_full = jnp.dot(chosen.astype(BF16), before, preferred_element_type=F32) + cnt_scr[...]
    cnt_scr[...] = cnt_scr[...] + jnp.sum(chosen, axis=1, keepdims=True)

    ranks = [jnp.sum(jnp.where(row == ids[k], rank_full, 0.0), axis=0, keepdims=True) for k in range(TOPK)]
    e_ref[...] = jnp.concatenate(ids, axis=0)
    rk_ref[...] = jnp.concatenate(ranks, axis=0).astype(jnp.int32)
    wn = [w / wsum * ROUTED_SCALE for w in wts]
    wpad = jnp.concatenate(wn + [jnp.zeros((LANES - TOPK, tm), F32)], axis=0)
    wcol_ref[...] = wpad.T
    cnt_ref[...] = cnt_scr[...].astype(jnp.int32)


def router(h, w_t_bf16, bias):
    T, D = h.shape
    tm = TOKEN_TILE
    lane_out = pl.BlockSpec((TOPK, tm), lambda i: (0, i))
    return pl.pallas_call(
        _router_kernel,
        out_shape=(jax.ShapeDtypeStruct((TOPK, T), jnp.int32), jax.ShapeDtypeStruct((T, LANES), F32),
                   jax.ShapeDtypeStruct((TOPK, T), jnp.int32), jax.ShapeDtypeStruct((N_EXP, 1), jnp.int32)),
        grid=(T // tm,),
        in_specs=[pl.BlockSpec((tm, D), lambda i: (i, 0)),
                  pl.BlockSpec((N_EXP, D), lambda i: (0, 0)),
                  pl.BlockSpec((N_EXP, 1), lambda i: (0, 0))],
        out_specs=(lane_out, pl.BlockSpec((tm, LANES), lambda i: (i, 0)), lane_out,
                   pl.BlockSpec((N_EXP, 1), lambda i: (0, 0))),
        scratch_shapes=[pltpu.VMEM((N_EXP, 1), F32)],
        compiler_params=_cparams(("arbitrary",)),
        name="router",
    )(h, w_t_bf16, bias.reshape(N_EXP, 1))


def _slot_kernel(e_ref, rk_ref, ps_ref, o_ref):
    tm = e_ref.shape[1]
    row = lax.broadcasted_iota(jnp.int32, (N_EXP, tm), 0)
    ps = ps_ref[...].astype(F32)
    outs = []
    for k in range(TOPK):
        base = jnp.sum(jnp.where(row == e_ref[k:k + 1, :], ps, 0.0), axis=0, keepdims=True)
        outs.append(base.astype(jnp.int32) + rk_ref[k:k + 1, :])
    o_ref[...] = jnp.concatenate(outs, axis=0)


def assign_slots(top_e, rank, pad_start):
    K, T = top_e.shape
    tm = next(t for t in (2048, 1024, 512, 256) if T % t == 0)
    blk = pl.BlockSpec((K, tm), lambda i: (0, i))
    return pl.pallas_call(
        _slot_kernel,
        out_shape=jax.ShapeDtypeStruct((K, T), jnp.int32),
        grid=(T // tm,),
        in_specs=[blk, blk, pl.BlockSpec((N_EXP, 1), lambda i: (0, 0))],
        out_specs=blk,
        compiler_params=_cparams(("parallel",)),
        name="assign_slots",
    )(top_e, rank, pad_start.reshape(N_EXP, 1).astype(jnp.int32))


def _pack_pair(lo, hi):
    a = lax.bitcast_convert_type(lo.astype(BF16).astype(F32), U32) >> 16
    b = lax.bitcast_convert_type(hi.astype(BF16).astype(F32), U32) & jnp.uint32(0xFFFF0000)
    return a | b


def _unpack_pair(w):
    lo = lax.bitcast_convert_type(w << 16, F32)
    hi = lax.bitcast_convert_type(w & jnp.uint32(0xFFFF0000), F32)
    return lo, hi


def _dispatch_kernel(pe_ref, cnt_ref, nu_ref, slot_hbm, h_ref, sg_ref, su_ref, sd_ref, xs_hbm, sh_ref,
                     slot_smem, stage, zbuf, sem_idx, sem):
    tm, D = h_ref.shape
    ch = stage.shape[1]
    i = pl.program_id(0)
    nb = xs_hbm.shape[0] // EXPERT_BLOCK

    @pl.when(i == 0)
    def _():
        zbuf[...] = jnp.zeros_like(zbuf)

        def block_copy(b):
            start = pl.multiple_of(b * EXPERT_BLOCK, EXPERT_BLOCK)
            return pltpu.make_async_copy(zbuf, xs_hbm.at[pl.ds(start, EXPERT_BLOCK)], sem)

        def for_partial_blocks(fn):
            def body(e, carry):
                @pl.when(cnt_ref[e] % EXPERT_BLOCK != 0)
                def _():
                    fn(block_copy(pe_ref[e] // EXPERT_BLOCK - 1))
                return carry
            lax.fori_loop(0, N_EXP, body, 0)

        def for_unused_blocks(fn):
            def body(b, carry):
                fn(block_copy(b))
                return carry
            lax.fori_loop(nu_ref[0], nb, body, 0)

        for_partial_blocks(lambda cp: cp.start())
        for_unused_blocks(lambda cp: cp.start())
        for_partial_blocks(lambda cp: cp.wait())
        for_unused_blocks(lambda cp: cp.wait())

    idx_copy = pltpu.make_async_copy(slot_hbm.at[i], slot_smem, sem_idx)
    idx_copy.start()
    x = h_ref[...]
    half = D // 2
    for j in range(ch):
        stage[:, j, :] = _pack_pair(x[:, j * LANES:(j + 1) * LANES],
                                    x[:, half + j * LANES:half + (j + 1) * LANES])
    idx_copy.wait()

    def issue(t, carry):
        for k in range(TOPK):
            pltpu.make_async_copy(stage.at[t], xs_hbm.at[slot_smem[k * tm + t]], sem).start(priority=k % 2)
        return carry

    lax.fori_loop(0, tm, issue, 0)

    hb = x.astype(BF16)
    g = jnp.dot(hb, sg_ref[0], preferred_element_type=F32)
    u = jnp.dot(hb, su_ref[0], preferred_element_type=F32)
    sh_ref[...] = jnp.dot((_silu(g) * u).astype(BF16), sd_ref[0], preferred_element_type=F32)

    for k in range(TOPK):
        pltpu.make_async_copy(stage, xs_hbm.at[pl.ds(0, tm)], sem).wait()


def dispatch(h, slots, n_slots, pad_end, counts, n_used, sg, su, sd, layer):
    T, D = h.shape
    tm = TOKEN_TILE
    ch = D // 2 // LANES
    shw = lambda a: pl.BlockSpec((1,) + a.shape[1:], lambda i, pe, cnt, nu: (layer, 0, 0))
    grid_spec = pltpu.PrefetchScalarGridSpec(
        num_scalar_prefetch=3,
        grid=(T // tm,),
        in_specs=[pl.BlockSpec(memory_space=pl.ANY),
                  pl.BlockSpec((tm, D), lambda i, pe, cnt, nu: (i, 0)),
                  shw(sg), shw(su), shw(sd)],
        out_specs=(pl.BlockSpec(memory_space=pl.ANY),
                   pl.BlockSpec((tm, D), lambda i, pe, cnt, nu: (i, 0))),
        scratch_shapes=[pltpu.SMEM((TOPK * tm,), jnp.int32),
                        pltpu.VMEM((tm, ch, LANES), U32),
                        pltpu.VMEM((EXPERT_BLOCK, ch, LANES), U32),
                        pltpu.SemaphoreType.DMA(()),
                        pltpu.SemaphoreType.DMA(())],
    )
    return pl.pallas_call(
        _dispatch_kernel,
        out_shape=(jax.ShapeDtypeStruct((n_slots, ch, LANES), U32), jax.ShapeDtypeStruct((T, D), F32)),
        grid_spec=grid_spec,
        compiler_params=_cparams(("arbitrary",)),
        name="moe_dispatch",
    )(pad_end.astype(jnp.int32), counts.astype(jnp.int32), n_used, slots, h, sg, su, sd)


def _expert_kernel(be_ref, nu_ref, buf_ref, nxt1_ref, nxt2_ref, xs_ref, wg_hbm, wu_hbm, wd_hbm, ys_ref,
                   wg_f, wu_f, wd_f, wg_s, wu_s, wd_s, sem, *, layer):
    i = pl.program_id(0)
    used = i < nu_ref[0]
    new_expert = jnp.logical_or(i == 0, be_ref[i] != be_ref[jnp.maximum(i - 1, 0)])

    def weight_copies(e, b):
        return (pltpu.make_async_copy(wg_hbm.at[layer, e], wg_f.at[b], sem.at[b]),
                pltpu.make_async_copy(wu_hbm.at[layer, e], wu_f.at[b], sem.at[b]),
                pltpu.make_async_copy(wd_hbm.at[layer, e], wd_f.at[b], sem.at[b]))

    @pl.when(i == 0)
    def _():
        for cp in weight_copies(be_ref[0], 0):
            cp.start()

        @pl.when(nxt1_ref[0] >= 0)
        def _():
            for cp in weight_copies(nxt1_ref[0], 1):
                cp.start()

    @pl.when(jnp.logical_not(used))
    def _():
        ys_ref[...] = jnp.zeros_like(ys_ref)

    @pl.when(jnp.logical_and(used, new_expert))
    def _():
        b = buf_ref[i]
        for cp in weight_copies(be_ref[i], b):
            cp.wait()

        @pl.when(nxt2_ref[i] >= 0)
        def _():
            for cp in weight_copies(nxt2_ref[i], jnp.where(b == 0, WEIGHT_SLOTS - 1, b - 1)):
                cp.start()

        wg_s[...] = wg_f[b].astype(BF16)
        wu_s[...] = wu_f[b].astype(BF16)
        wd_s[...] = wd_f[b].astype(BF16)

    @pl.when(used)
    def _():
        D = wg_s.shape[0]
        ch = D // 2 // LANES
        half = D // 2
        nrow = xs_ref.shape[0] // ch
        pairs = [_unpack_pair(xs_ref[pl.ds(j, nrow, stride=ch), :]) for j in range(ch)]
        x = jnp.concatenate([lo.astype(BF16) for lo, _ in pairs] + [hi.astype(BF16) for _, hi in pairs],
                            axis=-1)
        g = jnp.dot(x, wg_s[...], preferred_element_type=F32)
        u = jnp.dot(x, wu_s[...], preferred_element_type=F32)
        a = (_silu(g) * u).astype(BF16)
        y = jnp.dot(a, wd_s[...], preferred_element_type=F32)
        for j in range(ch):
            ys_ref[pl.ds(j, nrow, stride=ch), :] = _pack_pair(
                y[:, j * LANES:(j + 1) * LANES], y[:, half + j * LANES:half + (j + 1) * LANES])


def routed_experts(xs, block_expert, n_used, counts, w_gate_all, w_up_all, w_down_all, layer):
    n_slots, ch, _ = xs.shape
    _, E, D, Fd = w_gate_all.shape
    nb = n_slots // EXPERT_BLOCK
    rows_blk = EXPERT_BLOCK * ch
    starts = jnp.concatenate([jnp.ones((1,), jnp.int32),
                              (block_expert[1:] != block_expert[:-1]).astype(jnp.int32)])
    buffer_of = ((jnp.cumsum(starts) - 1) % WEIGHT_SLOTS).astype(jnp.int32)
    ids = jnp.arange(E, dtype=jnp.int32)
    later = jnp.where((counts[None, :] > 0) & (ids[None, :] > ids[:, None]), ids[None, :], E)
    next_used = jnp.min(later, axis=1)
    next_used = jnp.concatenate([next_used, jnp.full((1,), E, next_used.dtype)])
    lookup = lambda table, keys: jnp.sum(jnp.where(keys[:, None] == jnp.arange(E + 1)[None, :],
                                                   table[None, :], 0), axis=1)
    after1 = lookup(next_used, block_expert)
    after2 = lookup(next_used, after1)
    to_id = lambda a: jnp.where(a == E, -1, a).astype(jnp.int32)
    idx = lambda f: (lambda i, be, nu, buf, n1, n2: f(i, nu))
    grid_spec = pltpu.PrefetchScalarGridSpec(
        num_scalar_prefetch=5,
        grid=(nb,),
        in_specs=[pl.BlockSpec((rows_blk, LANES), idx(lambda i, nu: (jnp.minimum(i, nu[0] - 1), 0))),
                  pl.BlockSpec(memory_space=pl.ANY),
                  pl.BlockSpec(memory_space=pl.ANY),
                  pl.BlockSpec(memory_space=pl.ANY)],
        out_specs=pl.BlockSpec((rows_blk, LANES), idx(lambda i, nu: (i, 0))),
        scratch_shapes=[pltpu.VMEM((WEIGHT_SLOTS, D, Fd), F32), pltpu.VMEM((WEIGHT_SLOTS, D, Fd), F32),
                        pltpu.VMEM((WEIGHT_SLOTS, Fd, D), F32),
                        pltpu.VMEM((D, Fd), BF16), pltpu.VMEM((D, Fd), BF16), pltpu.VMEM((Fd, D), BF16),
                        pltpu.SemaphoreType.DMA((WEIGHT_SLOTS,))],
    )
    ys = pl.pallas_call(
        functools.partial(_expert_kernel, layer=layer),
        out_shape=jax.ShapeDtypeStruct((n_slots * ch, LANES), U32),
        grid_spec=grid_spec,
        compiler_params=_cparams(("arbitrary",)),
        name="routed_experts",
    )(block_expert, n_used, buffer_of, to_id(after1), to_id(after2), xs.reshape(n_slots * ch, LANES),
      w_gate_all, w_up_all, w_down_all)
    return ys.reshape(n_slots, ch, LANES)


def _combine_kernel(slot_hbm, ys_hbm, wt_ref, sh_ref, x_ref, gate_ref, lnw_ref, lnb_ref, o_ref,
                    slot_smem, buf, acc, sem_idx, sem, *, alpha):
    tm = x_ref.shape[0]
    ch = buf.shape[2]
    i = pl.program_id(0)
    cur = i % 2

    last = pl.num_programs(0) - 1
    half = ch * LANES

    per_tile = TOPK * tm

    def load_slots(tile, b):
        dst = slot_smem.at[pl.ds(pl.multiple_of(b * per_tile, per_tile), per_tile)]
        idx_copy = pltpu.make_async_copy(slot_hbm.at[tile], dst, sem_idx)
        idx_copy.start()
        idx_copy.wait()

    def issue_rows(t, b):
        for k in range(TOPK):
            pltpu.make_async_copy(ys_hbm.at[slot_smem[b * per_tile + k * tm + t]], buf.at[b, k, :, t, :],
                                  sem.at[b]).start(priority=k % 2)

    def wait_rows(b):
        for k in range(TOPK):
            pltpu.make_async_copy(buf.at[b, k], buf.at[b, k], sem.at[b]).wait()

    @pl.when(i == 0)
    def _():
        load_slots(0, 0)

        def first(t, carry):
            issue_rows(t, 0)
            return carry

        lax.fori_loop(0, tm, first, 0)

    nxt = 1 - cur
    load_slots(jnp.minimum(i + 1, last), nxt)
    wait_rows(cur)

    def reduce_and_issue(sb, carry):
        r0 = pl.multiple_of(sb * SUBLANES, SUBLANES)
        rows = pl.ds(r0, SUBLANES)
        for tt in range(SUBLANES):
            issue_rows(r0 + tt, nxt)
        wt = wt_ref[rows, :]
        for j in range(ch):
            lo, hi = _unpack_pair(buf[cur, 0, j, rows, :])
            r_lo, r_hi = wt[:, 0:1] * lo, wt[:, 0:1] * hi
            for k in range(1, TOPK):
                lo, hi = _unpack_pair(buf[cur, k, j, rows, :])
                r_lo, r_hi = r_lo + wt[:, k:k + 1] * lo, r_hi + wt[:, k:k + 1] * hi
            acc[rows, j * LANES:(j + 1) * LANES] = r_lo
            acc[rows, half + j * LANES:half + (j + 1) * LANES] = r_hi
        return carry

    lax.fori_loop(0, tm // SUBLANES, reduce_and_issue, 0)

    f = sh_ref[...] + acc[...]
    o_ref[...] = _ln(alpha * x_ref[...] + gate_ref[0] * f) * lnw_ref[0] + lnb_ref[0]

    @pl.when(i == last)
    def _():
        wait_rows(nxt)


def combine(slots, ys, wt, shared, x, gate_tiles, ln_w_all, ln_b_all, layer, alpha):
    T, D = x.shape
    tm = TOKEN_TILE
    ch = ys.shape[1]
    depth = ln_w_all.shape[0]
    prm = pl.BlockSpec((1, 1, D), lambda i: (layer, 0, 0))
    return pl.pallas_call(
        functools.partial(_combine_kernel, alpha=alpha),
        out_shape=jax.ShapeDtypeStruct((T, D), F32),
        grid=(T // tm,),
        in_specs=[pl.BlockSpec(memory_space=pl.ANY),
                  pl.BlockSpec(memory_space=pl.ANY),
                  pl.BlockSpec((tm, LANES), lambda i: (i, 0)),
                  pl.BlockSpec((tm, D), lambda i: (i, 0)),
                  pl.BlockSpec((tm, D), lambda i: (i, 0)),
                  pl.BlockSpec((1, 1, D), lambda i: (i, 0, 0)),
                  prm, prm],
        out_specs=pl.BlockSpec((tm, D), lambda i: (i, 0)),
        scratch_shapes=[pltpu.SMEM((2 * TOPK * tm,), jnp.int32),
                        pltpu.VMEM((2, TOPK, ch, tm, LANES), U32),
                        pltpu.VMEM((tm, D), F32),
                        pltpu.SemaphoreType.DMA(()),
                        pltpu.SemaphoreType.DMA((2,))],
        compiler_params=_cparams(("arbitrary",)),
        name="moe_combine",
    )(slots, ys, wt, shared, x, gate_tiles, ln_w_all.reshape(depth, 1, D), ln_b_all.reshape(depth, 1, D))


def moe_sublayer(h, x, gate_tiles, router_w_t, router_bias, w_gate_all, w_up_all, w_down_all, sg, su, sd,
                 ln_w_all, ln_b_all, layer, alpha):
    T, D = h.shape
    top_e, w_col, rank, counts = router(h, router_w_t, router_bias)
    counts = counts[:, 0]
    padded = (counts + EXPERT_BLOCK - 1) // EXPERT_BLOCK * EXPERT_BLOCK
    pad_end = jnp.cumsum(padded)
    pad_start = pad_end - padded
    slots = assign_slots(top_e, rank, pad_start)
    n_tiles = T // TOKEN_TILE
    slots = slots.reshape(TOPK, n_tiles, TOKEN_TILE).transpose(1, 0, 2).reshape(n_tiles, TOPK * TOKEN_TILE)
    n_slots = -(-(T * TOPK + N_EXP * (EXPERT_BLOCK - 1)) // EXPERT_BLOCK) * EXPERT_BLOCK
    nb = n_slots // EXPERT_BLOCK
    block_start = jnp.arange(nb, dtype=jnp.int32) * EXPERT_BLOCK
    block_expert = jnp.minimum(jnp.sum(block_start[:, None] >= pad_end[None, :], axis=1),
                               N_EXP - 1).astype(jnp.int32)
    n_used = (pad_end[-1:] // EXPERT_BLOCK).astype(jnp.int32)

    xs, shared = dispatch(h, slots, n_slots, pad_end, counts, n_used, sg, su, sd, layer)
    ys = routed_experts(xs, block_expert, n_used, counts, w_gate_all, w_up_all, w_down_all, layer)
    return combine(slots, ys, w_col, shared, x, gate_tiles, ln_w_all, ln_b_all, layer, alpha)


def kernel(x, c, ctx, c_ctx, ada_w, ada_b, w_in, na_rpb, w_o_na, w_fourier, ret_decay_fwd, ret_decay_bwd,
           ret_gn_w, w_o_ret, w_out, ln_mix_w, ln_mix_b, router_w, router_bias, exp_w_gate, exp_w_up,
           exp_w_down, sh_w_gate, sh_w_up, sh_w_down, ln_ffn_w, ln_ffn_b):
    B, N, D = x.shape
    L = ctx.shape[1]
    depth = ada_w.shape[0]
    alpha = (2.0 * depth) ** 0.25
    rows = N // GRID_COLS
    attn_rows = ATTN_ROWS if (rows % ATTN_ROWS == 0 and rows >= _window_span(ATTN_ROWS)) else 1

    rope_tabs = rope_tables(N)
    cn, sn = (t.astype(BF16) for t in dft_tables(N))
    cl, sl = (t.astype(BF16) for t in dft_tables(L))
    cc, sc = channel_dft_tables()
    zero_state = jnp.zeros((B, RT_HEADS, RT_DK, RT_DV), F32)
    mod_rows = -(-(B + 1) // BF16_ROWS) * BF16_ROWS
    c_rows = jnp.concatenate([c, c_ctx[None, :], jnp.zeros((mod_rows - B - 1, D), F32)], axis=0)
    sg_all, su_all, sd_all = sh_w_gate.astype(BF16), sh_w_up.astype(BF16), sh_w_down.astype(BF16)

    x_rows = x.reshape(B * N, D)
    xc_rows = ctx.reshape(B * L, D)
    for l in range(depth):
        update_ctx = l < depth - 1
        mod = ada_mod(c_rows, ada_w, ada_b, l)
        sh1, sc1, g1, sh2, sc2, g2 = (m[:, None, :] for m in jnp.split(mod[:B], 6, axis=-1))
        mod_c = jnp.broadcast_to(mod[B:B + 1], (B, 6 * D))
        sh1c, sc1c, g1c, sh2c, sc2c, g2c = (m[:, None, :] for m in jnp.split(mod_c, 6, axis=-1))
        lg_f = jax.nn.log_sigmoid(ret_decay_fwd[l].astype(F32))
        lg_b = jax.nn.log_sigmoid(ret_decay_bwd[l].astype(F32))

        w_in_b = w_in[l].astype(BF16)
        proj = ln_proj(x_rows, B, N, sh1, sc1, w_in_b)
        proj_c = ln_proj(xc_rows, 1, B * L, sh1c[:1], sc1c[:1], w_in_b).reshape(B, L, -1)

        y_na = window_attention(proj, proj_c, *window_bias_tables(na_rpb[l], rows, attn_rows))
        y_fn = (fourier_mix_half if N % (2 * FOURIER_TILE) == 0 else fourier_mix)(proj, cn, sn, cc, sc)
        ob_c, s_b = retention(proj_c, lg_b, zero_state, backward=True)
        if update_ctx:
            yrt_c, s_f = retention(proj_c, lg_f, zero_state, backward=False, finish=(ob_c, ret_gn_w, l))
        else:
            _, s_f = retention(proj_c, lg_f, zero_state, backward=False)
        ob, _ = retention(proj, lg_b, s_b, backward=True, rope_tabs=rope_tabs)
        y_ret, _ = retention(proj, lg_f, s_f, backward=False, rope_tabs=rope_tabs,
                             finish=(ob, ret_gn_w, l))

        wts = (w_o_na[l].astype(BF16), w_fourier[l].astype(BF16), w_o_ret[l].astype(BF16),
               w_out[l].astype(BF16))
        x_all, h_all = merge_branches(y_na, y_fn, y_ret, proj, *wts, x_rows, g1, ln_mix_w, ln_mix_b, l,
                                      sh2, sc2, alpha, extra_rows=B * L if update_ctx else 0)
        gate_tiles = jnp.repeat(g2, N // TOKEN_TILE, axis=0)
        if update_ctx:
            yna_c = context_attention(proj_c)
            yfn_c = fourier_mix(proj_c, cl, sl, cc, sc)
            xc_new, h_c = merge_branches(yna_c, yfn_c, yrt_c, proj_c, *wts, xc_rows, g1c, ln_mix_w, ln_mix_b,
                                         l, sh2c, sc2c, alpha)
            x_all = lax.dynamic_update_slice(x_all, xc_new, (B * N, 0))
            h_all = lax.dynamic_update_slice(h_all, h_c, (B * N, 0))
            gate_tiles = jnp.concatenate([gate_tiles, jnp.repeat(g2c, L // TOKEN_TILE, axis=0)], axis=0)

        x_rows = moe_sublayer(h_all, x_all, gate_tiles, router_w[l].T.astype(BF16), router_bias[l],
                              exp_w_gate, exp_w_up, exp_w_down, sg_all, su_all, sd_all,
                              ln_ffn_w, ln_ffn_b, l, alpha)
        if update_ctx:
            xc_rows = x_rows[B * N:]
    return x_rows[:B * N].reshape(B, N, D)
```
